```python
import math
import jax, jax.numpy as jnp
from jax import lax
import numpy as np

D_MODEL = 1024
BATCH = 1
SEQ = 16384
DEPTH = 1
DEC_BATCH = 8
DEC_SEQ = 16
PAST_LEN = 4096

CHUNK = 64
SSD_HEADS = 16
SSD_HEAD_DIM = 64
SSD_WIDTH = SSD_HEADS * SSD_HEAD_DIM
SSD_GROUPS = 2
SSD_STATE = 128
SSD_CONV = 4
SSD_CONV_DIM = SSD_WIDTH + 2 * SSD_GROUPS * SSD_STATE
SSD_SCAN_CHUNK = CHUNK
ATT_HEADS = 16
ATT_HEAD_DIM = 64
ATT_WIDTH = ATT_HEADS * ATT_HEAD_DIM
N_LEFT_CHUNKS = 8
BAND = (N_LEFT_CHUNKS + 1) * CHUNK
REL_CLIP = 128
MIX_WIDTH = SSD_WIDTH + ATT_WIDTH
IN_SPLITS = (SSD_WIDTH,
             SSD_WIDTH + SSD_CONV_DIM,
             SSD_WIDTH + SSD_CONV_DIM + SSD_HEADS,
             SSD_WIDTH + SSD_CONV_DIM + SSD_HEADS + ATT_WIDTH,
             SSD_WIDTH + SSD_CONV_DIM + SSD_HEADS + 2 * ATT_WIDTH)
IN_WIDTH = SSD_WIDTH + SSD_CONV_DIM + SSD_HEADS + 3 * ATT_WIDTH
N_MEM = 256
MEM_HEADS = 4
MEM_HEAD_DIM = D_MODEL // MEM_HEADS
D_FF = 4 * D_MODEL
EPS = 1e-6

kernel_name = "hybrid_ssd_chunkband_stream_step"


def rms_norm(x, g):
    xf = x.astype(jnp.float32)
    y = xf * lax.rsqrt(jnp.mean(xf * xf, axis=-1, keepdims=True) + EPS)
    return (y * g.astype(jnp.float32)).astype(x.dtype)


def causal_dwconv(u, prev, w, b):
    L = u.shape[1]
    upad = jnp.concatenate([prev.astype(u.dtype), u], axis=1)
    out = b + w[0] * upad[:, 0:L]
    for k in range(1, SSD_CONV):
        out = out + w[k] * upad[:, k:k + L]
    return out, upad[:, -(SSD_CONV - 1):]


def ssd_scan(xh, dt, A, Bm, Cm, h0, chunk):
    b, L, H, P = xh.shape
    G, N, R = SSD_GROUPS, SSD_STATE, H // SSD_GROUPS
    nc = L // chunk
    a = (dt * A).reshape(b, nc, chunk, G, R)
    X = (xh.astype(jnp.float32) * dt[..., None]).reshape(b, nc, chunk, G, R, P)
    Bc = Bm.astype(jnp.float32).reshape(b, nc, chunk, G, N)
    Cc = Cm.astype(jnp.float32).reshape(b, nc, chunk, G, N)
    a_cum = jnp.cumsum(a, axis=2)
    seg = a_cum[:, :, :, None] - a_cum[:, :, None, :]
    causal = jnp.tril(jnp.ones((chunk, chunk), dtype=bool))[:, :, None, None]
    decay = jnp.exp(jnp.where(causal, seg, -jnp.inf))
    CB = jnp.einsum("bclgn,bcsgn->bclsg", Cc, Bc)
    y_diag = jnp.einsum("bclsg,bclsgr,bcsgrp->bclgrp", CB, decay, X)
    decay_to_end = jnp.exp(a_cum[:, :, -1:] - a_cum)
    chunk_states = jnp.einsum("bclgn,bclgr,bclgrp->bcgrpn", Bc, decay_to_end, X)
    chunk_decay = jnp.exp(a_cum[:, :, -1])

    def step(h, inp):
        s, d = inp
        return d[..., None, None] * h + s, h

    h_final, h_in = lax.scan(
        step, h0.astype(jnp.float32).reshape(b, G, R, P, N),
        (jnp.moveaxis(chunk_states, 1, 0), jnp.moveaxis(chunk_decay, 1, 0)))
    h_in = jnp.moveaxis(h_in, 0, 1)
    y_off = jnp.einsum("bclgn,bcgrpn,bclgr->bclgrp", Cc, h_in, jnp.exp(a_cum))
    y = (y_diag + y_off).reshape(b, L, H, P)
    return y, h_final.reshape(b, H, P, N)


def ssd_mixer(z, xBC, dt_raw, conv_prev, h0, conv_w, conv_b, A_log, dt_bias, D_skip, g_out, chunk):
    b, L, _ = xBC.shape
    xc, conv_new = causal_dwconv(xBC, conv_prev, conv_w, conv_b)
    xc = jax.nn.silu(xc)
    xs, Bm, Cm = jnp.split(xc, [SSD_WIDTH, SSD_WIDTH + SSD_GROUPS * SSD_STATE], axis=-1)
    xh = xs.reshape(b, L, SSD_HEADS, SSD_HEAD_DIM)
    Bm = Bm.reshape(b, L, SSD_GROUPS, SSD_STATE)
    Cm = Cm.reshape(b, L, SSD_GROUPS, SSD_STATE)
    dt = jax.nn.softplus(dt_raw.astype(jnp.float32) + dt_bias.astype(jnp.float32))
    A = -jnp.exp(A_log.astype(jnp.float32))
    y, h_new = ssd_scan(xh, dt, A, Bm, Cm, h0, chunk)
    y = y + D_skip.astype(jnp.float32)[:, None] * xh.astype(jnp.float32)
    y = y.reshape(b, L, SSD_WIDTH).astype(z.dtype)
    y = rms_norm(y * jax.nn.silu(z), g_out)
    return y, h_new, conv_new


def rel_bias(table, d):
    return table[:, jnp.clip(d, -REL_CLIP, REL_CLIP) + REL_CLIP].astype(jnp.float32)


def band_attention_prompt(q, k, v, table):
    b, L, H, Dh = q.shape
    nc = L // CHUNK
    past = N_LEFT_CHUNKS * CHUNK
    kp = jnp.pad(k, ((0, 0), (past, 0), (0, 0), (0, 0)))
    vp = jnp.pad(v, ((0, 0), (past, 0), (0, 0), (0, 0)))
    qc = q.reshape(b, nc, CHUNK, H, Dh)
    qi = jnp.arange(CHUNK)
    kj = jnp.arange(BAND)
    bias = rel_bias(table, qi[:, None] + past - kj[None, :])
    scale = Dh ** -0.5

    def one_chunk(c):
        qb = lax.dynamic_index_in_dim(qc, c, axis=1, keepdims=False)
        kb = lax.dynamic_slice_in_dim(kp, c * CHUNK, BAND, axis=1)
        vb = lax.dynamic_slice_in_dim(vp, c * CHUNK, BAND, axis=1)
        s = jnp.einsum("bqhd,bkhd->bhqk", qb, kb).astype(jnp.float32) * scale + bias
        valid = (c * CHUNK - past + kj) >= 0
        s = jnp.where(valid[None, None, None, :], s, -jnp.inf)
        p = jax.nn.softmax(s, axis=-1).astype(vb.dtype)
        return jnp.einsum("bhqk,bkhd->bqhd", p, vb)

    out = lax.map(one_chunk, jnp.arange(nc))
    return jnp.moveaxis(out, 0, 1).reshape(b, L, H * Dh)


def band_attention_sample(q, k_new, v_new, k_cache, v_cache, table):
    b, Lq, H, Dh = q.shape
    Lc = k_cache.shape[1]
    k = jnp.concatenate([k_cache.astype(k_new.dtype), k_new], axis=1)
    v = jnp.concatenate([v_cache.astype(v_new.dtype), v_new], axis=1)
    qi = jnp.arange(Lq)
    kj = jnp.arange(Lc + Lq)
    bias = rel_bias(table, qi[:, None] + Lc - kj[None, :])
    s = jnp.einsum("bqhd,bkhd->bhqk", q, k).astype(jnp.float32) * (Dh ** -0.5) + bias
    p = jax.nn.softmax(s, axis=-1).astype(v.dtype)
    return jnp.einsum("bhqk,bkhd->bqhd", p, v).reshape(b, Lq, H * Dh)


def memory_kv(mem, g_src, w_mk, w_mv, g_mk):
    b = mem.shape[0]
    m = rms_norm(mem, g_src)
    k = rms_norm((m @ w_mk).reshape(b, N_MEM, MEM_HEADS, MEM_HEAD_DIM), g_mk)
    v = (m @ w_mv).reshape(b, N_MEM, MEM_HEADS, MEM_HEAD_DIM)
    return k, v


def memory_attention(hn, mem_k, mem_v, w_mq, g_mq, w_mo):
    b, L, _ = hn.shape
    q = rms_norm((hn @ w_mq).reshape(b, L, MEM_HEADS, MEM_HEAD_DIM), g_mq)
    s = jnp.einsum("bqhd,bkhd->bhqk", q, mem_k.astype(q.dtype)).astype(jnp.float32) * (MEM_HEAD_DIM ** -0.5)
    p = jax.nn.softmax(s, axis=-1).astype(hn.dtype)
    o = jnp.einsum("bhqk,bkhd->bqhd", p, mem_v.astype(hn.dtype)).reshape(b, L, D_MODEL)
    return o @ w_mo


def layer_forward(x, conv_prev, h0, k_cache, v_cache, mem_k, mem_v, w, chunk):
    b, L, _ = x.shape
    xn = rms_norm(x, w["g_mix"])
    proj = xn @ w["w_in"]
    z, xBC, dt_raw, q, k, v = jnp.split(proj, IN_SPLITS, axis=-1)
    y_ssd, h_new, conv_new = ssd_mixer(z, xBC, dt_raw, conv_prev, h0, w["conv_w"], w["conv_b"],
                                       w["A_log"], w["dt_bias"], w["D"], w["g_ssd"], chunk)
    q = rms_norm(q.reshape(b, L, ATT_HEADS, ATT_HEAD_DIM), w["g_q"])
    k = rms_norm(k.reshape(b, L, ATT_HEADS, ATT_HEAD_DIM), w["g_k"])
    v = v.reshape(b, L, ATT_HEADS, ATT_HEAD_DIM)
    if k_cache is None:
        y_att = band_attention_prompt(q, k, v, w["rel"])
        past = N_LEFT_CHUNKS * CHUNK
        k_rows, v_rows = k[:, -past:], v[:, -past:]
    else:
        y_att = band_attention_sample(q, k, v, k_cache, v_cache, w["rel"])
        k_rows, v_rows = k, v
    h = x + jnp.concatenate([y_ssd.astype(x.dtype), y_att.astype(x.dtype)], axis=-1) @ w["w_out"]
    h = h + memory_attention(rms_norm(h, w["g_mem_x"]), mem_k, mem_v, w["w_mq"], w["g_mq"], w["w_mo"])
    hn = rms_norm(h, w["g_ffn"])
    y = h + jnp.square(jax.nn.relu(hn @ w["w_ff1"])) @ w["w_ff2"]
    return y, h_new, conv_new, k_rows, v_rows


def setup_inputs(seed: int = 0) -> dict:
    key = jax.random.key(seed)
    ks = iter(list(jax.random.split(key, 48)))

    def nrm(shape, scale):
        return scale * jax.random.normal(next(ks), shape, jnp.float32)

    def gain(shape):
        return 1.0 + 0.05 * jax.random.normal(next(ks), shape, jnp.float32)

    att_cache_len = min(N_LEFT_CHUNKS * CHUNK, PAST_LEN)
    dt0 = jnp.exp(jax.random.uniform(next(ks), (DEPTH, SSD_HEADS), jnp.float32,
                                     math.log(1e-3), math.log(1e-1)))
    a0 = jax.random.uniform(next(ks), (DEPTH, SSD_HEADS), jnp.float32, 1.0, 16.0)
    return {
        "x_prompt": nrm((BATCH, SEQ, D_MODEL), 1.0),
        "x_sample": nrm((DEC_BATCH, DEC_SEQ, D_MODEL), 1.0),
        "mem_prompt": nrm((BATCH, N_MEM, D_MODEL), 1.0),
        "state_ssd": nrm((DEPTH, DEC_BATCH, SSD_HEADS, SSD_HEAD_DIM, SSD_STATE), 0.5),
        "state_conv": nrm((DEPTH, DEC_BATCH, SSD_CONV - 1, SSD_CONV_DIM), 1.0),
        "cache_attn_k": nrm((DEPTH, DEC_BATCH, att_cache_len, ATT_HEADS, ATT_HEAD_DIM), 1.0),
        "cache_attn_v": nrm((DEPTH, DEC_BATCH, att_cache_len, ATT_HEADS, ATT_HEAD_DIM), 1.0),
        "cache_mem_k": nrm((DEPTH, DEC_BATCH, N_MEM, MEM_HEADS, MEM_HEAD_DIM), 1.0),
        "cache_mem_v": nrm((DEPTH, DEC_BATCH, N_MEM, MEM_HEADS, MEM_HEAD_DIM), 1.0),
        "g_mix": gain((DEPTH, D_MODEL)),
        "w_in": nrm((DEPTH, D_MODEL, IN_WIDTH), D_MODEL ** -0.5),
        "conv_w": nrm((DEPTH, SSD_CONV, SSD_CONV_DIM), SSD_CONV ** -0.5),
        "conv_b": nrm((DEPTH, SSD_CONV_DIM), 0.02),
        "ssd_A_log": jnp.log(a0),
        "ssd_dt_bias": dt0 + jnp.log(-jnp.expm1(-dt0)),
        "ssd_D": gain((DEPTH, SSD_HEADS)),
        "ssd_g_out": gain((DEPTH, SSD_WIDTH)),
        "att_g_q": gain((DEPTH, ATT_HEAD_DIM)),
        "att_g_k": gain((DEPTH, ATT_HEAD_DIM)),
        "att_rel_bias": nrm((DEPTH, ATT_HEADS, 2 * REL_CLIP + 1), 0.5),
        "w_out": nrm((DEPTH, MIX_WIDTH, D_MODEL), MIX_WIDTH ** -0.5),
        "g_mem_x": gain((DEPTH, D_MODEL)),
        "g_mem_src": gain((DEPTH, D_MODEL)),
        "w_mq": nrm((DEPTH, D_MODEL, D_MODEL), D_MODEL ** -0.5),
        "w_mk": nrm((DEPTH, D_MODEL, D_MODEL), D_MODEL ** -0.5),
        "w_mv": nrm((DEPTH, D_MODEL, D_MODEL), D_MODEL ** -0.5),
        "g_mq": gain((DEPTH, MEM_HEAD_DIM)),
        "g_mk": gain((DEPTH, MEM_HEAD_DIM)),
        "w_mo": nrm((DEPTH, D_MODEL, D_MODEL), D_MODEL ** -0.5),
        "g_ffn": gain((DEPTH, D_MODEL)),
        "w_ff1": nrm((DEPTH, D_MODEL, D_FF), D_MODEL ** -0.5),
        "w_ff2": nrm((DEPTH, D_FF, D_MODEL), D_FF ** -0.5),
    }


def reference(x_prompt, x_sample, mem_prompt, state_ssd, state_conv, cache_attn_k, cache_attn_v,
              cache_mem_k, cache_mem_v, g_mix, w_in, conv_w, conv_b, ssd_A_log, ssd_dt_bias, ssd_D,
              ssd_g_out, att_g_q, att_g_k, att_rel_bias, w_out, g_mem_x, g_mem_src, w_mq, w_mk,
              w_mv, g_mq, g_mk, w_mo, g_ffn, w_ff1, w_ff2):
    b_p = x_prompt.shape[0]
    chunk_sample = x_sample.shape[1]
    yp, ys = x_prompt, x_sample
    p_ssd, p_conv, p_k, p_v, p_mk, p_mv = [], [], [], [], [], []
    s_ssd, s_conv, s_k, s_v = [], [], [], []
    for l in range(DEPTH):
        w = {"g_mix": g_mix[l], "w_in": w_in[l], "conv_w": conv_w[l], "conv_b": conv_b[l],
             "A_log": ssd_A_log[l], "dt_bias": ssd_dt_bias[l], "D": ssd_D[l], "g_ssd": ssd_g_out[l],
             "g_q": att_g_q[l], "g_k": att_g_k[l], "rel": att_rel_bias[l], "w_out": w_out[l],
             "g_mem_x": g_mem_x[l], "w_mq": w_mq[l], "g_mq": g_mq[l], "w_mo": w_mo[l],
             "g_ffn": g_ffn[l], "w_ff1": w_ff1[l], "w_ff2": w_ff2[l]}
        mk, mv = memory_kv(mem_prompt, g_mem_src[l], w_mk[l], w_mv[l], g_mk[l])
        conv0 = jnp.zeros((b_p, SSD_CONV - 1, SSD_CONV_DIM), x_prompt.dtype)
        h00 = jnp.zeros((b_p, SSD_HEADS, SSD_HEAD_DIM, SSD_STATE), jnp.float32)
        yp, hp, cp, kp, vp = layer_forward(yp, conv0, h00, None, None, mk, mv, w, SSD_SCAN_CHUNK)
        p_ssd.append(hp); p_conv.append(cp); p_k.append(kp); p_v.append(vp)
        p_mk.append(mk); p_mv.append(mv)
        ys, hs, cs, ks_, vs_ = layer_forward(ys, state_conv[l], state_ssd[l], cache_attn_k[l],
                                             cache_attn_v[l], cache_mem_k[l], cache_mem_v[l], w,
                                             chunk_sample)
        s_ssd.append(hs); s_conv.append(cs); s_k.append(ks_); s_v.append(vs_)
    return (yp, ys,
            jnp.stack(p_ssd), jnp.stack(p_conv), jnp.stack(p_k), jnp.stack(p_v),
            jnp.stack(p_mk), jnp.stack(p_mv),
            jnp.stack(s_ssd), jnp.stack(s_conv), jnp.stack(s_k), jnp.stack(s_v))
```

```python
import functools

import jax
import jax.numpy as jnp
from jax import lax
from jax.experimental import pallas as pl
from jax.experimental.pallas import tpu as pltpu

F32 = jnp.float32
BF16 = jnp.bfloat16

D_MODEL = 1024
CHUNK = 64
SSD_HEADS = 16
SSD_HEAD_DIM = 64
SSD_WIDTH = SSD_HEADS * SSD_HEAD_DIM
SSD_GROUPS = 2
SSD_STATE = 128
SSD_CONV = 4
SSD_CONV_DIM = SSD_WIDTH + 2 * SSD_GROUPS * SSD_STATE
ATT_HEADS = 16
ATT_HEAD_DIM = 64
ATT_WIDTH = ATT_HEADS * ATT_HEAD_DIM
N_LEFT_CHUNKS = 8
PAST = N_LEFT_CHUNKS * CHUNK
REL_CLIP = 128
N_MEM = 256
MEM_HEADS = 4
MEM_HEAD_DIM = D_MODEL // MEM_HEADS
D_FF = 4 * D_MODEL
EPS = 1e-6

LANES = 128
DT_PAD = LANES
IN_PAD_WIDTH = SSD_WIDTH + SSD_CONV_DIM + DT_PAD + 3 * ATT_WIDTH
NEG = -1e30
VMEM_LIMIT = 56 * 1024 * 1024


def _rms(x, g):
    return x * lax.rsqrt(jnp.mean(x * x, axis=-1, keepdims=True) + EPS) * g


def _silu(x):
    return x * jax.nn.sigmoid(x)


def _split2(x):
    hi = x.astype(BF16)
    lo = (x - hi.astype(F32)).astype(BF16)
    return hi, lo


def _split3(x):
    hi = x.astype(BF16)
    r = x - hi.astype(F32)
    mid = r.astype(BF16)
    lo = (r - mid.astype(F32)).astype(BF16)
    return hi, mid, lo


def _dot(a, b):
    return jnp.dot(a, b, preferred_element_type=F32)


def _dot_nt(a, b):
    return lax.dot_general(a, b, (((1,), (1,)), ((), ())), preferred_element_type=F32)


def _inproj_kernel(x_ref, gmix_ref, w_ref, gq_ref, gk_ref,
                   z_ref, xbc_ref, dt_ref, q_ref, k_ref, v_ref, kt_ref, vt_ref, *, n_tail):
    i = pl.program_id(0)
    n = pl.num_programs(0)
    tr = x_ref.shape[0]
    xn = _rms(x_ref[...], gmix_ref[...]).astype(BF16)

    def proj(lo, width):
        return _dot(xn, w_ref[:, lo:lo + width])

    o_xbc = SSD_WIDTH
    o_dt = o_xbc + SSD_CONV_DIM
    o_q = o_dt + DT_PAD
    o_k = o_q + ATT_WIDTH
    o_v = o_k + ATT_WIDTH
    z_ref[...] = proj(0, SSD_WIDTH)
    xbc_ref[...] = proj(o_xbc, SSD_CONV_DIM)
    dt_ref[...] = proj(o_dt, DT_PAD)

    lane = lax.broadcasted_iota(jnp.int32, (tr, LANES), 1)
    first = lane < ATT_HEAD_DIM

    def head_norm(y, g_ref, c, scale):
        blk = y[:, c * LANES:(c + 1) * LANES]
        sq = blk * blk
        s0 = jnp.sum(jnp.where(first, sq, 0.0), axis=-1, keepdims=True)
        s1 = jnp.sum(jnp.where(first, 0.0, sq), axis=-1, keepdims=True)
        r = jnp.where(first, lax.rsqrt(s0 * (1.0 / ATT_HEAD_DIM) + EPS),
                      lax.rsqrt(s1 * (1.0 / ATT_HEAD_DIM) + EPS))
        return blk * r * (g_ref[:, c * LANES:(c + 1) * LANES] * scale)

    is_tail = i >= n - n_tail
    q = proj(o_q, ATT_WIDTH)
    for c in range(ATT_WIDTH // LANES):
        q_ref[:, c * LANES:(c + 1) * LANES] = head_norm(q, gq_ref, c, ATT_HEAD_DIM ** -0.5).astype(BF16)
    k = proj(o_k, ATT_WIDTH)
    for c in range(ATT_WIDTH // LANES):
        kn = head_norm(k, gk_ref, c, 1.0)
        k_ref[:, c * LANES:(c + 1) * LANES] = kn.astype(BF16)

        @pl.when(is_tail)
        def _():
            kt_ref[:, c * LANES:(c + 1) * LANES] = kn
    v = proj(o_v, ATT_WIDTH)
    v_ref[...] = v.astype(BF16)

    @pl.when(is_tail)
    def _():
        vt_ref[...] = v


def _in_proj(x2d, g_mix, w_cat, gq_t, gk_t, *, tr, tail_rows):
    rows = x2d.shape[0]
    n = rows // tr
    n_tail = tail_rows // tr
    assert n * tr == rows and n_tail * tr == tail_rows

    def row(width):
        return pl.BlockSpec((tr, width), lambda i: (i, 0))

    def const(shape):
        return pl.BlockSpec(shape, lambda i: (0,) * len(shape))

    tail = pl.BlockSpec((tr, ATT_WIDTH), lambda i: (jnp.maximum(i - (n - n_tail), 0), 0))
    return pl.pallas_call(
        functools.partial(_inproj_kernel, n_tail=n_tail),
        grid=(n,),
        in_specs=[row(D_MODEL), const((1, D_MODEL)), const((D_MODEL, IN_PAD_WIDTH)),
                  const((1, ATT_WIDTH)), const((1, ATT_WIDTH))],
        out_specs=[row(SSD_WIDTH), row(SSD_CONV_DIM), row(DT_PAD), row(ATT_WIDTH), row(ATT_WIDTH),
                   row(ATT_WIDTH), tail, tail],
        out_shape=[jax.ShapeDtypeStruct((rows, SSD_WIDTH), F32),
                   jax.ShapeDtypeStruct((rows, SSD_CONV_DIM), F32),
                   jax.ShapeDtypeStruct((rows, DT_PAD), F32),
                   jax.ShapeDtypeStruct((rows, ATT_WIDTH), BF16),
                   jax.ShapeDtypeStruct((rows, ATT_WIDTH), BF16),
                   jax.ShapeDtypeStruct((rows, ATT_WIDTH), BF16),
                   jax.ShapeDtypeStruct((tail_rows, ATT_WIDTH), F32),
                   jax.ShapeDtypeStruct((tail_rows, ATT_WIDTH), F32)],
        compiler_params=pltpu.CompilerParams(dimension_semantics=("arbitrary",),
                                             vmem_limit_bytes=VMEM_LIMIT),
        name="in_proj",
    )(x2d, g_mix, w_cat, gq_t, gk_t)


def _ssd_kernel(xbc_ref, dtraw_ref, z_ref, cprev_ref, h0_ref, convw_ref, convb_ref, dtb_ref,
                alog_ref, dskip_ref, gout_ref, expand_ref,
                y_ref, hfin_ref,
                ext_ref, xc_ref, ht_ref, ybuf_ref, *, t, lb):
    c = pl.program_id(1)
    nc = pl.num_programs(1)
    halo = 8

    @pl.when(c == 0)
    def _():
        ext_ref[0:halo, :] = cprev_ref[0]
        ht_ref[...] = h0_ref[0].T

    @pl.when(c > 0)
    def _():
        ext_ref[0:halo, :] = ext_ref[t:t + halo, :]

    ext_ref[halo:halo + lb, :] = xbc_ref[0]
    if lb < t:
        ext_ref[halo + lb:halo + t, :] = jnp.zeros((t - lb, SSD_CONV_DIM), F32)

    for cb in range(SSD_CONV_DIM // LANES):
        sl = slice(cb * LANES, (cb + 1) * LANES)
        acc = convb_ref[:, sl] + convw_ref[SSD_CONV - 1:SSD_CONV, sl] * ext_ref[halo:halo + t, sl]
        for j in range(1, SSD_CONV):
            acc = acc + convw_ref[SSD_CONV - 1 - j:SSD_CONV - j, sl] * ext_ref[halo - j:halo - j + t, sl]
        xc_ref[:, sl] = _silu(acc)

    lane = lax.broadcasted_iota(jnp.int32, (t, LANES), 1)
    rowi = lax.broadcasted_iota(jnp.int32, (t, LANES), 0)
    dtraw = dtraw_ref[0]
    if lb < t:
        dtraw = jnp.concatenate([dtraw, jnp.zeros((t - lb, DT_PAD), F32)], axis=0)
    dt = jax.nn.softplus(dtraw + dtb_ref[...])
    dt = jnp.where((lane < SSD_HEADS) & (rowi < lb), dt, 0.0)
    a_neg = -jnp.exp(alog_ref[...])
    a = dt * a_neg

    rr = lax.broadcasted_iota(jnp.int32, (t, t), 0)
    cc = lax.broadcasted_iota(jnp.int32, (t, t), 1)
    causal = rr >= cc
    tril = jnp.where(causal, 1.0, 0.0).astype(BF16)
    a1, a2, a3 = _split3(a)
    a_cum = _dot(tril, a1) + _dot(tril, a2) + _dot(tril, a3)
    a_last = a_cum[t - 1:t, :]
    ea = jnp.exp(a_cum)
    cd = jnp.exp(a_last)
    a_t = a_cum.T
    dt_t = dt.T
    w_t = dt_t * jnp.exp(a_t[:, t - 1:t] - a_t)

    stacked = jnp.concatenate([ea, jnp.broadcast_to(cd, (8, LANES))], axis=0)
    s_hi, s_lo = _split2(stacked)
    expanded = _dot(s_hi, expand_ref[...]) + _dot(s_lo, expand_ref[...])
    ea_x = expanded[0:t, :]
    cd_x = expanded[t:t + 1, :]

    first = lane < SSD_HEAD_DIM
    heads_per_group = SSD_HEADS // SSD_GROUPS
    o_b = SSD_WIDTH
    o_c = SSD_WIDTH + SSD_GROUPS * SSD_STATE
    ssq = jnp.zeros((lb, 1), F32)
    for g in range(SSD_GROUPS):
        bg = xc_ref[:, o_b + g * SSD_STATE:o_b + (g + 1) * SSD_STATE]
        cg = xc_ref[:, o_c + g * SSD_STATE:o_c + (g + 1) * SSD_STATE].astype(BF16)
        cb_mat = _dot_nt(cg, bg.astype(BF16))
        bg_t = bg.T
        for jp in range(heads_per_group // 2):
            j = g * (heads_per_group // 2) + jp
            sl = slice(j * LANES, (j + 1) * LANES)
            xh = xc_ref[:, sl]
            y_pair = None
            s_pair = None
            for hh in range(2):
                h = 2 * j + hh
                xm = jnp.where(first if hh == 0 else jnp.logical_not(first), xh, 0.0).astype(BF16)
                seg = a_cum[:, h:h + 1] - a_t[h:h + 1, :]
                decay = jnp.exp(jnp.where(causal, seg, NEG))
                m = (cb_mat * decay * dt_t[h:h + 1, :]).astype(BF16)
                yd = _dot(m, xm)
                bw = (bg_t * w_t[h:h + 1, :]).astype(BF16)
                sd = _dot(bw, xm)
                y_pair = yd if y_pair is None else y_pair + yd
                s_pair = sd if s_pair is None else s_pair + sd
            h_in = ht_ref[:, sl]
            y_off = _dot(cg, h_in.astype(BF16)) * ea_x[:, sl]
            ht_ref[:, sl] = cd_x[:, sl] * h_in + s_pair
            y = y_pair + y_off + dskip_ref[:, sl] * xh
            yg = y[0:lb, :] * _silu(z_ref[0, :, sl])
            ybuf_ref[:, sl] = yg
            ssq = ssq + jnp.sum(yg * yg, axis=-1, keepdims=True)

    r = lax.rsqrt(ssq * (1.0 / SSD_WIDTH) + EPS)
    y_ref[0] = (ybuf_ref[...] * r * gout_ref[...]).astype(BF16)

    @pl.when(c == nc - 1)
    def _():
        hfin_ref[0] = ht_ref[...].T


def _ssd_mixer(xbc, dtraw, z, cprev8, h0, conv_w, conv_b, dtb, alog, dskip, gout, expand, *, t, lb):
    b, length, _ = xbc.shape
    nc = length // lb
    assert nc * lb == length and (lb == t or nc == 1)

    def seq(width):
        return pl.BlockSpec((1, lb, width), lambda bi, ci: (bi, ci, 0))

    def per_b(shape):
        return pl.BlockSpec((1,) + shape, lambda bi, ci: (bi, 0, 0))

    def const(shape):
        return pl.BlockSpec(shape, lambda bi, ci: (0,) * len(shape))

    hp = SSD_WIDTH
    return pl.pallas_call(
        functools.partial(_ssd_kernel, t=t, lb=lb),
        grid=(b, nc),
        in_specs=[seq(SSD_CONV_DIM), seq(DT_PAD), seq(SSD_WIDTH), per_b((8, SSD_CONV_DIM)),
                  per_b((hp, SSD_STATE)), const((SSD_CONV, SSD_CONV_DIM)), const((1, SSD_CONV_DIM)),
                  const((1, DT_PAD)), const((1, DT_PAD)), const((1, SSD_WIDTH)), const((1, SSD_WIDTH)),
                  const((LANES, SSD_WIDTH))],
        out_specs=[seq(SSD_WIDTH), per_b((hp, SSD_STATE))],
        out_shape=[jax.ShapeDtypeStruct((b, length, SSD_WIDTH), BF16),
                   jax.ShapeDtypeStruct((b, hp, SSD_STATE), F32)],
        scratch_shapes=[pltpu.VMEM((t + 8, SSD_CONV_DIM), F32),
                        pltpu.VMEM((t, SSD_CONV_DIM), F32),
                        pltpu.VMEM((SSD_STATE, hp), F32),
                        pltpu.VMEM((lb, SSD_WIDTH), F32)],
        compiler_params=pltpu.CompilerParams(dimension_semantics=("arbitrary", "arbitrary"),
                                             vmem_limit_bytes=VMEM_LIMIT),
        name="ssd_mixer",
    )(xbc, dtraw, z, cprev8, h0, conv_w, conv_b, dtb, alog, dskip, gout, expand)


def _band_prompt_kernel(q_ref, k_ref, v_ref, bias_ref, o_ref, *, tq):
    i = pl.program_id(1)
    w = tq + PAST
    start = pl.multiple_of(i * tq, tq)
    kw = k_ref[pl.ds(start, w), :]
    vw = v_ref[pl.ds(start, w), :]
    q = q_ref[...]
    lane = lax.broadcasted_iota(jnp.int32, (tq, LANES), 1)
    first = lane < ATT_HEAD_DIM
    col = lax.broadcasted_iota(jnp.int32, (tq, w), 1)
    before_start = col < PAST - start
    outs = []
    for hh in range(2):
        qm = jnp.where(first if hh == 0 else jnp.logical_not(first), q, jnp.zeros_like(q))
        s = _dot_nt(qm, kw) + bias_ref[hh]
        s = jnp.where(before_start, NEG, s)
        m = jnp.max(s, axis=-1, keepdims=True)
        p = jnp.exp(s - m)
        l = jnp.sum(p, axis=-1, keepdims=True)
        outs.append(_dot(p.astype(BF16), vw) * (1.0 / l))
    o_ref[...] = jnp.where(first, outs[0], outs[1]).astype(BF16)


def _band_prompt(q, kpad, vpad, bias, *, tq):
    length = q.shape[0]
    nq = length // tq
    w = tq + PAST
    lp = kpad.shape[0]
    return pl.pallas_call(
        functools.partial(_band_prompt_kernel, tq=tq),
        grid=(ATT_WIDTH // LANES, nq),
        in_specs=[pl.BlockSpec((tq, LANES), lambda c, i: (i, c)),
                  pl.BlockSpec((lp, LANES), lambda c, i: (0, c)),
                  pl.BlockSpec((lp, LANES), lambda c, i: (0, c)),
                  pl.BlockSpec((2, tq, w), lambda c, i: (c, 0, 0))],
        out_specs=pl.BlockSpec((tq, LANES), lambda c, i: (i, c)),
        out_shape=jax.ShapeDtypeStruct((length, ATT_WIDTH), BF16),
        compiler_params=pltpu.CompilerParams(dimension_semantics=("arbitrary", "arbitrary"),
                                             vmem_limit_bytes=VMEM_LIMIT),
        name="band_prompt",
    )(q, kpad, vpad, bias)


def _band_sample_kernel(q_ref, kn_ref, vn_ref, kc_ref, vc_ref, bc_ref, bn_ref, o_ref):
    lq = q_ref.shape[0]
    lane = lax.broadcasted_iota(jnp.int32, (lq, LANES), 1)
    first = lane < ATT_HEAD_DIM
    for j in range(ATT_WIDTH // LANES):
        sl = slice(j * LANES, (j + 1) * LANES)
        q = q_ref[:, sl]
        kc = kc_ref[0, :, sl].astype(BF16)
        vc = vc_ref[0, :, sl].astype(BF16)
        kn = kn_ref[:, sl]
        vn = vn_ref[:, sl]
        outs = []
        for hh in range(2):
            h = 2 * j + hh
            qm = jnp.where(first if hh == 0 else jnp.logical_not(first), q, jnp.zeros_like(q))
            sc = _dot_nt(qm, kc) + bc_ref[h]
            sn = _dot_nt(qm, kn) + bn_ref[h]
            m = jnp.maximum(jnp.max(sc, axis=-1, keepdims=True), jnp.max(sn, axis=-1, keepdims=True))
            pc = jnp.exp(sc - m)
            pn = jnp.exp(sn - m)
            l = jnp.sum(pc, axis=-1, keepdims=True) + jnp.sum(pn, axis=-1, keepdims=True)
            o = _dot(pc.astype(BF16), vc) + _dot(pn.astype(BF16), vn)
            outs.append(o * (1.0 / l))
        o_ref[:, sl] = jnp.where(first, outs[0], outs[1]).astype(BF16)


def _band_sample(q, kn, vn, kc, vc, bias_c, bias_n, *, lq):
    rows = q.shape[0]
    nb = rows // lq
    lc = kc.shape[1]
    blk = pl.BlockSpec((lq, ATT_WIDTH), lambda b: (b, 0))
    cache = pl.BlockSpec((1, lc, ATT_WIDTH), lambda b: (b, 0, 0))
    return pl.pallas_call(
        _band_sample_kernel,
        grid=(nb,),
        in_specs=[blk, blk, blk, cache, cache,
                  pl.BlockSpec((ATT_HEADS, lq, lc), lambda b: (0, 0, 0)),
                  pl.BlockSpec((ATT_HEADS, lq, lq), lambda b: (0, 0, 0))],
        out_specs=blk,
        out_shape=jax.ShapeDtypeStruct((rows, ATT_WIDTH), BF16),
        compiler_params=pltpu.CompilerParams(dimension_semantics=("arbitrary",),
                                             vmem_limit_bytes=VMEM_LIMIT),
        name="band_sample",
    )(q, kn, vn, kc, vc, bias_c, bias_n)


def _mem_head_norm(y, g_ref, hd, scale):
    blk = y[:, hd * MEM_HEAD_DIM:(hd + 1) * MEM_HEAD_DIM]
    r = lax.rsqrt(jnp.mean(blk * blk, axis=-1, keepdims=True) + EPS)
    return blk * r * (g_ref[...] * scale)


def _memkv_kernel(mem_ref, gsrc_ref, wk_ref, wv_ref, gk_ref, k_ref, v_ref):
    m = _rms(mem_ref[...], gsrc_ref[...]).astype(BF16)
    k = _dot(m, wk_ref[...])
    for hd in range(MEM_HEADS):
        k_ref[:, hd * MEM_HEAD_DIM:(hd + 1) * MEM_HEAD_DIM] = _mem_head_norm(k, gk_ref, hd, 1.0)
    v_ref[...] = _dot(m, wv_ref[...])


def _memory_kv(mem2d, g_src, w_mk, w_mv, g_mk):
    rows = mem2d.shape[0]

    def full(shape):
        return pl.BlockSpec(shape, lambda i: (0,) * len(shape))

    return pl.pallas_call(
        _memkv_kernel,
        grid=(1,),
        in_specs=[full((rows, D_MODEL)), full((1, D_MODEL)), full((D_MODEL, D_MODEL)),
                  full((D_MODEL, D_MODEL)), full((1, MEM_HEAD_DIM))],
        out_specs=[full((rows, D_MODEL)), full((rows, D_MODEL))],
        out_shape=[jax.ShapeDtypeStruct((rows, D_MODEL), F32)] * 2,
        compiler_params=pltpu.CompilerParams(dimension_semantics=("arbitrary",),
                                             vmem_limit_bytes=VMEM_LIMIT),
        name="memory_kv",
    )(mem2d, g_src, w_mk, w_mv, g_mk)


def _outproj_mem_kernel(x_ref, ys_ref, ya_ref, wo_ref, gmx_ref, wq_ref, gmq_ref, mk_ref, mv_ref,
                        wmo_ref, h_ref, obuf_ref, *, nb, rb):
    h = x_ref[...] + _dot(ys_ref[...], wo_ref[0:SSD_WIDTH, :]) + _dot(ya_ref[...], wo_ref[SSD_WIDTH:, :])
    hn = _rms(h, gmx_ref[...]).astype(BF16)
    q = _dot(hn, wq_ref[...])
    for hd in range(MEM_HEADS):
        sl = slice(hd * MEM_HEAD_DIM, (hd + 1) * MEM_HEAD_DIM)
        qn = _mem_head_norm(q, gmq_ref, hd, MEM_HEAD_DIM ** -0.5).astype(BF16)
        for b in range(nb):
            rows = slice(b * rb, (b + 1) * rb)
            s = _dot_nt(qn[rows, :], mk_ref[b, :, sl])
            m = jnp.max(s, axis=-1, keepdims=True)
            p = jnp.exp(s - m)
            l = jnp.sum(p, axis=-1, keepdims=True)
            o = _dot(p.astype(BF16), mv_ref[b, :, sl]) * (1.0 / l)
            obuf_ref[rows, sl] = o.astype(BF16)
    h_ref[...] = h + _dot(obuf_ref[...], wmo_ref[...])


def _outproj_mem(x2d, ys, ya, w_out, g_mem_x, w_mq, g_mq, mk, mv, w_mo, *, tr, seq_len):
    rows = x2d.shape[0]
    n = rows // tr
    nb = max(tr // seq_len, 1)
    tiles_per_stream = max(seq_len // tr, 1)
    rb = tr // nb
    assert n * tr == rows and nb * rb == tr and mk.shape[0] * seq_len == rows

    def row(width):
        return pl.BlockSpec((tr, width), lambda i: (i, 0))

    def const(shape):
        return pl.BlockSpec(shape, lambda i: (0,) * len(shape))

    mem = pl.BlockSpec((nb, N_MEM, D_MODEL), lambda i: (i // tiles_per_stream, 0, 0))
    return pl.pallas_call(
        functools.partial(_outproj_mem_kernel, nb=nb, rb=rb),
        grid=(n,),
        in_specs=[row(D_MODEL), row(SSD_WIDTH), row(ATT_WIDTH), const((SSD_WIDTH + ATT_WIDTH, D_MODEL)),
                  const((1, D_MODEL)), const((D_MODEL, D_MODEL)), const((1, MEM_HEAD_DIM)), mem, mem,
                  const((D_MODEL, D_MODEL))],
        out_specs=row(D_MODEL),
        out_shape=jax.ShapeDtypeStruct((rows, D_MODEL), F32),
        scratch_shapes=[pltpu.VMEM((tr, D_MODEL), BF16)],
        compiler_params=pltpu.CompilerParams(dimension_semantics=("arbitrary",),
                                             vmem_limit_bytes=VMEM_LIMIT),
        name="outproj_mem",
    )(x2d, ys, ya, w_out, g_mem_x, w_mq, g_mq, mk, mv, w_mo)


FF_SLAB = 1024


def _ffn_kernel(h_ref, g_ref, w1_ref, w2_ref, y_ref):
    h = h_ref[...]
    hn = _rms(h, g_ref[...]).astype(BF16)
    acc = h
    for s in range(D_FF // FF_SLAB):
        u = jnp.maximum(_dot(hn, w1_ref[:, s * FF_SLAB:(s + 1) * FF_SLAB]), 0.0)
        acc = acc + _dot((u * u).astype(BF16), w2_ref[s * FF_SLAB:(s + 1) * FF_SLAB, :])
    y_ref[...] = acc


def _ffn(h2d, g_ffn, w1, w2, *, tr):
    rows = h2d.shape[0]
    n = rows // tr
    assert n * tr == rows

    def const(shape):
        return pl.BlockSpec(shape, lambda i: (0,) * len(shape))

    row = pl.BlockSpec((tr, D_MODEL), lambda i: (i, 0))
    return pl.pallas_call(
        _ffn_kernel,
        grid=(n,),
        in_specs=[row, const((1, D_MODEL)), const((D_MODEL, D_FF)), const((D_FF, D_MODEL))],
        out_specs=row,
        out_shape=jax.ShapeDtypeStruct((rows, D_MODEL), F32),
        compiler_params=pltpu.CompilerParams(dimension_semantics=("arbitrary",),
                                             vmem_limit_bytes=VMEM_LIMIT),
        name="ffn",
    )(h2d, g_ffn, w1, w2)


def _rel_bias_prompt(table, tq):
    w = tq + PAST
    qi = jnp.arange(tq)[:, None]
    kj = jnp.arange(w)[None, :]
    d = jnp.clip(qi + PAST - kj, -REL_CLIP, REL_CLIP) + REL_CLIP
    bias = table[:, d].astype(F32)
    qc = qi // CHUNK
    kc = kj // CHUNK
    valid = (kc >= qc) & (kc <= qc + N_LEFT_CHUNKS)
    return jnp.where(valid[None], bias, NEG)


def _rel_bias_sample(table, lq, lc):
    qi = jnp.arange(lq)[:, None]
    kj = jnp.arange(lc + lq)[None, :]
    d = jnp.clip(qi + lc - kj, -REL_CLIP, REL_CLIP) + REL_CLIP
    bias = table[:, d].astype(F32)
    return bias[:, :, :lc], bias[:, :, lc:]


def _prep_weights(g_mix, w_in, conv_w, conv_b, ssd_A_log, ssd_dt_bias, ssd_D, ssd_g_out, att_g_q, att_g_k,
                  w_out, g_mem_x, g_mem_src, w_mq, w_mk, w_mv, g_mq, g_mk, w_mo, g_ffn, w_ff1, w_ff2):
    o_dt = SSD_WIDTH + SSD_CONV_DIM
    w_cat = jnp.concatenate(
        [w_in[:, :o_dt], jnp.pad(w_in[:, o_dt:o_dt + SSD_HEADS], ((0, 0), (0, DT_PAD - SSD_HEADS))),
         w_in[:, o_dt + SSD_HEADS:]], axis=1).astype(BF16)
    pad_h = lambda v: jnp.pad(v, (0, DT_PAD - SSD_HEADS)).reshape(1, DT_PAD)
    row = lambda v: v.reshape(1, -1)
    expand = (jnp.arange(LANES)[:, None] == jnp.arange(SSD_WIDTH)[None, :] // SSD_HEAD_DIM).astype(BF16)
    return dict(
        g_mix=row(g_mix), w_cat=w_cat,
        gq_t=row(jnp.tile(att_g_q, ATT_HEADS)), gk_t=row(jnp.tile(att_g_k, ATT_HEADS)),
        conv_w=conv_w, conv_b=row(conv_b), dtb=pad_h(ssd_dt_bias), alog=pad_h(ssd_A_log),
        dskip=row(jnp.repeat(ssd_D, SSD_HEAD_DIM)), gout=row(ssd_g_out), expand=expand,
        w_out=w_out.astype(BF16), g_mem_x=row(g_mem_x), g_mem_src=row(g_mem_src),
        w_mq=w_mq.astype(BF16), w_mk=w_mk.astype(BF16), w_mv=w_mv.astype(BF16),
        g_mq=row(g_mq), g_mk=row(g_mk), w_mo=w_mo.astype(BF16), g_ffn=row(g_ffn),
        w_ff1=w_ff1.astype(BF16), w_ff2=w_ff2.astype(BF16))


def _layer(x, conv_prev, h0, k_cache, v_cache, mem_k, mem_v, p, rel, *, tr, t_scan, tq):
    b, length, _ = x.shape
    rows = b * length
    x2d = x.reshape(rows, D_MODEL)
    prompt = k_cache is None
    tail_rows = PAST if prompt else rows
    z, xbc, dtraw, q, k, v, k_tail, v_tail = _in_proj(
        x2d, p["g_mix"], p["w_cat"], p["gq_t"], p["gk_t"], tr=tr, tail_rows=tail_rows)

    cprev8 = jnp.pad(conv_prev, ((0, 0), (8 - (SSD_CONV - 1), 0), (0, 0)))
    lb = t_scan if prompt else length
    y_ssd, h_fin = _ssd_mixer(
        xbc.reshape(b, length, SSD_CONV_DIM), dtraw.reshape(b, length, DT_PAD),
        z.reshape(b, length, SSD_WIDTH), cprev8, h0.reshape(b, SSD_WIDTH, SSD_STATE),
        p["conv_w"], p["conv_b"], p["dtb"], p["alog"], p["dskip"], p["gout"], p["expand"],
        t=t_scan, lb=lb)
    conv_new = xbc.reshape(b, length, SSD_CONV_DIM)[:, length - (SSD_CONV - 1):]

    if prompt:
        kpad = jnp.pad(k, ((PAST, 0), (0, 0)))
        vpad = jnp.pad(v, ((PAST, 0), (0, 0)))
        y_att = _band_prompt(q, kpad, vpad, _rel_bias_prompt(rel, tq), tq=tq)
        k_rows = k_tail.reshape(b, PAST, ATT_HEADS, ATT_HEAD_DIM)
        v_rows = v_tail.reshape(b, PAST, ATT_HEADS, ATT_HEAD_DIM)
    else:
        lc = k_cache.shape[1]
        bias_c, bias_n = _rel_bias_sample(rel, length, lc)
        y_att = _band_sample(q, k, v, k_cache.reshape(b, lc, ATT_WIDTH), v_cache.reshape(b, lc, ATT_WIDTH),
                             bias_c, bias_n, lq=length)
        k_rows = k_tail.reshape(b, length, ATT_HEADS, ATT_HEAD_DIM)
        v_rows = v_tail.reshape(b, length, ATT_HEADS, ATT_HEAD_DIM)

    h = _outproj_mem(x2d, y_ssd.reshape(rows, SSD_WIDTH), y_att, p["w_out"], p["g_mem_x"], p["w_mq"],
                     p["g_mq"], mem_k.astype(BF16), mem_v.astype(BF16), p["w_mo"], tr=tr, seq_len=length)
    y = _ffn(h, p["g_ffn"], p["w_ff1"], p["w_ff2"], tr=tr)
    return (y.reshape(b, length, D_MODEL), h_fin.reshape(b, SSD_HEADS, SSD_HEAD_DIM, SSD_STATE),
            conv_new, k_rows, v_rows)


def kernel(x_prompt, x_sample, mem_prompt, state_ssd, state_conv, cache_attn_k, cache_attn_v, cache_mem_k,
           cache_mem_v, g_mix, w_in, conv_w, conv_b, ssd_A_log, ssd_dt_bias, ssd_D, ssd_g_out, att_g_q,
           att_g_k, att_rel_bias, w_out, g_mem_x, g_mem_src, w_mq, w_mk, w_mv, g_mq, g_mk, w_mo, g_ffn,
           w_ff1, w_ff2):
    depth = g_mix.shape[0]
    b_p, seq, _ = x_prompt.shape
    b_s, dec_seq, _ = x_sample.shape
    yp, ys = x_prompt, x_sample
    outs = [[] for _ in range(10)]
    for l in range(depth):
        p = _prep_weights(g_mix[l], w_in[l], conv_w[l], conv_b[l], ssd_A_log[l], ssd_dt_bias[l], ssd_D[l],
                          ssd_g_out[l], att_g_q[l], att_g_k[l], w_out[l], g_mem_x[l], g_mem_src[l],
                          w_mq[l], w_mk[l], w_mv[l], g_mq[l], g_mk[l], w_mo[l], g_ffn[l], w_ff1[l], w_ff2[l])
        rel = att_rel_bias[l]
        mk, mv = _memory_kv(mem_prompt.reshape(b_p * N_MEM, D_MODEL), p["g_mem_src"], p["w_mk"], p["w_mv"],
                            p["g_mk"])
        mk = mk.reshape(b_p, N_MEM, D_MODEL)
        mv = mv.reshape(b_p, N_MEM, D_MODEL)
        conv0 = jnp.zeros((b_p, SSD_CONV - 1, SSD_CONV_DIM), F32)
        h00 = jnp.zeros((b_p, SSD_HEADS, SSD_HEAD_DIM, SSD_STATE), F32)
        yp, hp, cp, kp, vp = _layer(yp, conv0, h00, None, None, mk, mv, p, rel,
                                    tr=min(256, seq), t_scan=min(256, seq), tq=min(256, seq))
        ys, hs, cs, ks_, vs_ = _layer(ys, state_conv[l], state_ssd[l], cache_attn_k[l], cache_attn_v[l],
                                      cache_mem_k[l].reshape(b_s, N_MEM, D_MODEL),
                                      cache_mem_v[l].reshape(b_s, N_MEM, D_MODEL), p, rel,
                                      tr=b_s * dec_seq, t_scan=128, tq=None)
        for lst, val in zip(outs, (hp, cp, kp, vp,
                                   mk.reshape(b_p, N_MEM, MEM_HEADS, MEM_HEAD_DIM),
                                   mv.reshape(b_p, N_MEM, MEM_HEADS, MEM_HEAD_DIM),
                                   hs, cs, ks_, vs_)):
            lst.append(val)
    return (yp, ys) + tuple(jnp.stack(o) for o in outs)
```

```python
import functools

import jax
import jax.numpy as jnp
from jax import lax
from jax.experimental import pallas as pl
from jax.experimental.pallas import tpu as pltpu

F32 = jnp.float32
BF16 = jnp.bfloat16

D_MODEL = 1024
CHUNK = 64
SSD_HEADS = 16
SSD_HEAD_DIM = 64
SSD_WIDTH = SSD_HEADS * SSD_HEAD_DIM
SSD_GROUPS = 2
SSD_STATE = 128
SSD_CONV = 4
SSD_CONV_DIM = SSD_WIDTH + 2 * SSD_GROUPS * SSD_STATE
ATT_HEADS = 16
ATT_HEAD_DIM = 64
ATT_WIDTH = ATT_HEADS * ATT_HEAD_DIM
N_LEFT_CHUNKS = 8
PAST = N_LEFT_CHUNKS * CHUNK
REL_CLIP = 128
N_MEM = 256
MEM_HEADS = 4
MEM_HEAD_DIM = D_MODEL // MEM_HEADS
D_FF = 4 * D_MODEL
EPS = 1e-6

LANES = 128
DT_PAD = LANES
IN_WIDTH = SSD_WIDTH + SSD_CONV_DIM + SSD_HEADS + 3 * ATT_WIDTH
IN_PAD_WIDTH = SSD_WIDTH + SSD_CONV_DIM + DT_PAD + 3 * ATT_WIDTH
NEG = -1e30
LOG2E = 1.4426950408889634
VMEM_LIMIT = 56 * 1024 * 1024


def _rms(x, g):
    return x * lax.rsqrt(jnp.mean(x * x, axis=-1, keepdims=True) + EPS) * g


def _silu(x):
    return x * jax.nn.sigmoid(x)


def _split2(x):
    hi = x.astype(BF16)
    lo = (x - hi.astype(F32)).astype(BF16)
    return hi, lo


def _split3(x):
    hi = x.astype(BF16)
    r = x - hi.astype(F32)
    mid = r.astype(BF16)
    lo = (r - mid.astype(F32)).astype(BF16)
    return hi, mid, lo


def _dot(a, b):
    return jnp.dot(a, b, preferred_element_type=F32)


def _dot_nt(a, b):
    return lax.dot_general(a, b, (((1,), (1,)), ((), ())), preferred_element_type=F32)


def _inproj_kernel(x_ref, gmix_ref, w_ref, wvt_ref, gq_ref, gk_ref,
                   z_ref, xbc_ref, dt_ref, q_ref, k_ref, v_ref, kt_ref, vt_ref, *, n_tail, v_feature_major):
    i = pl.program_id(0)
    n = pl.num_programs(0)
    tr = x_ref.shape[0]
    xn = _rms(x_ref[...], gmix_ref[...]).astype(BF16)

    def proj(lo, width):
        return _dot(xn, w_ref[:, lo:lo + width])

    o_xbc = SSD_WIDTH
    o_dt = o_xbc + SSD_CONV_DIM
    o_q = o_dt + DT_PAD
    o_k = o_q + ATT_WIDTH
    o_v = o_k + ATT_WIDTH
    z_ref[...] = proj(0, SSD_WIDTH)
    xbc_ref[...] = proj(o_xbc, SSD_CONV_DIM)
    dt_ref[...] = proj(o_dt, DT_PAD)

    lane = lax.broadcasted_iota(jnp.int32, (tr, LANES), 1)
    first = lane < ATT_HEAD_DIM

    def head_norm(y, g_ref, c, scale):
        blk = y[:, c * LANES:(c + 1) * LANES]
        sq = blk * blk
        s0 = jnp.sum(jnp.where(first, sq, 0.0), axis=-1, keepdims=True)
        s1 = jnp.sum(jnp.where(first, 0.0, sq), axis=-1, keepdims=True)
        r = jnp.where(first, lax.rsqrt(s0 * (1.0 / ATT_HEAD_DIM) + EPS),
                      lax.rsqrt(s1 * (1.0 / ATT_HEAD_DIM) + EPS))
        return blk * r * (g_ref[:, c * LANES:(c + 1) * LANES] * scale)

    is_tail = i >= n - n_tail
    q = proj(o_q, ATT_WIDTH)
    for c in range(ATT_WIDTH // LANES):
        q_ref[:, c * LANES:(c + 1) * LANES] = head_norm(
            q, gq_ref, c, ATT_HEAD_DIM ** -0.5 * LOG2E).astype(BF16)
    k = proj(o_k, ATT_WIDTH)
    for c in range(ATT_WIDTH // LANES):
        kn = head_norm(k, gk_ref, c, 1.0)
        k_ref[:, c * LANES:(c + 1) * LANES] = kn.astype(BF16)

        @pl.when(is_tail)
        def _():
            kt_ref[:, c * LANES:(c + 1) * LANES] = kn
    if v_feature_major:
        v_ref[...] = _dot_nt(wvt_ref[...], xn).astype(BF16)

        @pl.when(is_tail)
        def _():
            vt_ref[...] = proj(o_v, ATT_WIDTH)
    else:
        v = proj(o_v, ATT_WIDTH)
        v_ref[...] = v.astype(BF16)

        @pl.when(is_tail)
        def _():
            vt_ref[...] = v


def _in_proj(x2d, g_mix, w_cat, w_vt, gq_t, gk_t, *, tr, tail_rows, v_feature_major):
    rows = x2d.shape[0]
    n = rows // tr
    n_tail = tail_rows // tr
    assert n * tr == rows and n_tail * tr == tail_rows

    def row(width):
        return pl.BlockSpec((tr, width), lambda i: (i, 0))

    def const(shape):
        return pl.BlockSpec(shape, lambda i: (0,) * len(shape))

    tail = pl.BlockSpec((tr, ATT_WIDTH), lambda i: (jnp.maximum(i - (n - n_tail), 0), 0))
    if v_feature_major:
        v_spec = pl.BlockSpec((ATT_WIDTH, tr), lambda i: (0, i))
        v_shape = jax.ShapeDtypeStruct((ATT_WIDTH, rows), BF16)
    else:
        v_spec = row(ATT_WIDTH)
        v_shape = jax.ShapeDtypeStruct((rows, ATT_WIDTH), BF16)
    return pl.pallas_call(
        functools.partial(_inproj_kernel, n_tail=n_tail, v_feature_major=v_feature_major),
        grid=(n,),
        in_specs=[row(D_MODEL), const((1, D_MODEL)), const((D_MODEL, IN_PAD_WIDTH)),
                  const((ATT_WIDTH, D_MODEL)), const((1, ATT_WIDTH)), const((1, ATT_WIDTH))],
        out_specs=[row(SSD_WIDTH), row(SSD_CONV_DIM), row(DT_PAD), row(ATT_WIDTH), row(ATT_WIDTH),
                   v_spec, tail, tail],
        out_shape=[jax.ShapeDtypeStruct((rows, SSD_WIDTH), F32),
                   jax.ShapeDtypeStruct((rows, SSD_CONV_DIM), F32),
                   jax.ShapeDtypeStruct((rows, DT_PAD), F32),
                   jax.ShapeDtypeStruct((rows, ATT_WIDTH), BF16),
                   jax.ShapeDtypeStruct((rows, ATT_WIDTH), BF16),
                   v_shape,
                   jax.ShapeDtypeStruct((tail_rows, ATT_WIDTH), F32),
                   jax.ShapeDtypeStruct((tail_rows, ATT_WIDTH), F32)],
        compiler_params=pltpu.CompilerParams(dimension_semantics=("arbitrary",),
                                             vmem_limit_bytes=VMEM_LIMIT),
        name="in_proj",
    )(x2d, g_mix, w_cat, w_vt, gq_t, gk_t)


def _ssd_kernel(xbc_ref, dtraw_ref, z_ref, cprev_ref, h0_ref, convw_ref, convb_ref, dtb_ref,
                alog_ref, dskip_ref, gout_ref, expand_ref,
                y_ref, hfin_ref,
                ext_ref, xc_ref, ht_ref, ybuf_ref, *, t, lb):
    c = pl.program_id(1)
    nc = pl.num_programs(1)
    halo = 8

    @pl.when(c == 0)
    def _():
        ext_ref[0:halo, :] = cprev_ref[0]
        ht_ref[...] = h0_ref[0].T

    @pl.when(c > 0)
    def _():
        ext_ref[0:halo, :] = ext_ref[t:t + halo, :]

    ext_ref[halo:halo + lb, :] = xbc_ref[0]
    if lb < t:
        ext_ref[halo + lb:halo + t, :] = jnp.zeros((t - lb, SSD_CONV_DIM), F32)

    for cb in range(SSD_CONV_DIM // LANES):
        sl = slice(cb * LANES, (cb + 1) * LANES)
        acc = convb_ref[:, sl] + convw_ref[SSD_CONV - 1:SSD_CONV, sl] * ext_ref[halo:halo + t, sl]
        for j in range(1, SSD_CONV):
            acc = acc + convw_ref[SSD_CONV - 1 - j:SSD_CONV - j, sl] * ext_ref[halo - j:halo - j + t, sl]
        xc_ref[:, sl] = _silu(acc)

    lane = lax.broadcasted_iota(jnp.int32, (t, LANES), 1)
    rowi = lax.broadcasted_iota(jnp.int32, (t, LANES), 0)
    dtraw = dtraw_ref[0]
    if lb < t:
        dtraw = jnp.concatenate([dtraw, jnp.zeros((t - lb, DT_PAD), F32)], axis=0)
    dt = jax.nn.softplus(dtraw + dtb_ref[...])
    dt = jnp.where((lane < SSD_HEADS) & (rowi < lb), dt, 0.0)
    a_neg = -jnp.exp(alog_ref[...])
    a = dt * a_neg

    rr = lax.broadcasted_iota(jnp.int32, (t, t), 0)
    cc = lax.broadcasted_iota(jnp.int32, (t, t), 1)
    causal = rr >= cc
    tril = jnp.where(causal, 1.0, 0.0).astype(BF16)
    a1, a2, a3 = _split3(a)
    a_cum = _dot(tril, a1) + _dot(tril, a2) + _dot(tril, a3)
    a_last = a_cum[t - 1:t, :]
    ea = jnp.exp(a_cum)
    cd = jnp.exp(a_last)
    a_t = a_cum.T
    dt_t = dt.T
    w_t = dt_t * jnp.exp(a_t[:, t - 1:t] - a_t)

    stacked = jnp.concatenate([ea, jnp.broadcast_to(cd, (8, LANES))], axis=0)
    s_hi, s_lo = _split2(stacked)
    expanded = _dot(s_hi, expand_ref[...]) + _dot(s_lo, expand_ref[...])
    ea_x = expanded[0:t, :]
    cd_x = expanded[t:t + 1, :]

    first = lane < SSD_HEAD_DIM
    heads_per_group = SSD_HEADS // SSD_GROUPS
    o_b = SSD_WIDTH
    o_c = SSD_WIDTH + SSD_GROUPS * SSD_STATE
    ssq = jnp.zeros((lb, 1), F32)
    for g in range(SSD_GROUPS):
        bg = xc_ref[:, o_b + g * SSD_STATE:o_b + (g + 1) * SSD_STATE]
        cg = xc_ref[:, o_c + g * SSD_STATE:o_c + (g + 1) * SSD_STATE].astype(BF16)
        cb_mat = _dot_nt(cg, bg.astype(BF16))
        bg_t = bg.T
        for jp in range(heads_per_group // 2):
            j = g * (heads_per_group // 2) + jp
            sl = slice(j * LANES, (j + 1) * LANES)
            xh = xc_ref[:, sl]
            y_pair = None
            s_pair = None
            for hh in range(2):
                h = 2 * j + hh
                xm = jnp.where(first if hh == 0 else jnp.logical_not(first), xh, 0.0).astype(BF16)
                seg = a_cum[:, h:h + 1] - a_t[h:h + 1, :]
                decay = jnp.exp(jnp.where(causal, seg, NEG))
                m = (cb_mat * decay * dt_t[h:h + 1, :]).astype(BF16)
                yd = _dot(m, xm)
                bw = (bg_t * w_t[h:h + 1, :]).astype(BF16)
                sd = _dot(bw, xm)
                y_pair = yd if y_pair is None else y_pair + yd
                s_pair = sd if s_pair is None else s_pair + sd
            h_in = ht_ref[:, sl]
            y_off = _dot(cg, h_in.astype(BF16)) * ea_x[:, sl]
            ht_ref[:, sl] = cd_x[:, sl] * h_in + s_pair
            y = y_pair + y_off + dskip_ref[:, sl] * xh
            yg = y[0:lb, :] * _silu(z_ref[0, :, sl])
            ybuf_ref[:, sl] = yg
            ssq = ssq + jnp.sum(yg * yg, axis=-1, keepdims=True)

    r = lax.rsqrt(ssq * (1.0 / SSD_WIDTH) + EPS)
    y_ref[0] = (ybuf_ref[...] * r * gout_ref[...]).astype(BF16)

    @pl.when(c == nc - 1)
    def _():
        hfin_ref[0] = ht_ref[...].T


def _ssd_mixer(xbc, dtraw, z, cprev8, h0, conv_w, conv_b, dtb, alog, dskip, gout, expand, *, t, lb):
    b, length, _ = xbc.shape
    nc = length // lb
    assert nc * lb == length and (lb == t or nc == 1)

    def seq(width):
        return pl.BlockSpec((1, lb, width), lambda bi, ci: (bi, ci, 0))

    def per_b(shape):
        return pl.BlockSpec((1,) + shape, lambda bi, ci: (bi, 0, 0))

    def const(shape):
        return pl.BlockSpec(shape, lambda bi, ci: (0,) * len(shape))

    hp = SSD_WIDTH
    return pl.pallas_call(
        functools.partial(_ssd_kernel, t=t, lb=lb),
        grid=(b, nc),
        in_specs=[seq(SSD_CONV_DIM), seq(DT_PAD), seq(SSD_WIDTH), per_b((8, SSD_CONV_DIM)),
                  per_b((hp, SSD_STATE)), const((SSD_CONV, SSD_CONV_DIM)), const((1, SSD_CONV_DIM)),
                  const((1, DT_PAD)), const((1, DT_PAD)), const((1, SSD_WIDTH)), const((1, SSD_WIDTH)),
                  const((LANES, SSD_WIDTH))],
        out_specs=[seq(SSD_WIDTH), per_b((hp, SSD_STATE))],
        out_shape=[jax.ShapeDtypeStruct((b, length, SSD_WIDTH), BF16),
                   jax.ShapeDtypeStruct((b, hp, SSD_STATE), F32)],
        scratch_shapes=[pltpu.VMEM((t + 8, SSD_CONV_DIM), F32),
                        pltpu.VMEM((t, SSD_CONV_DIM), F32),
                        pltpu.VMEM((SSD_STATE, hp), F32),
                        pltpu.VMEM((lb, SSD_WIDTH), F32)],
        compiler_params=pltpu.CompilerParams(dimension_semantics=("arbitrary", "arbitrary"),
                                             vmem_limit_bytes=VMEM_LIMIT),
        name="ssd_mixer",
    )(xbc, dtraw, z, cprev8, h0, conv_w, conv_b, dtb, alog, dskip, gout, expand)


ATT_SB = 2 * CHUNK
ATT_WIN = ATT_SB + PAST


def _toeplitz_rows(base_row, rows, offset, width):
    ext = base_row.shape[1]
    rolled = pltpu.roll(jnp.broadcast_to(base_row, (rows, ext)), 0, 1, stride=1, stride_axis=0)
    return rolled[:, offset:offset + width]


def _band_subblock(q, kw, vt, bias_t, first):
    return _band_softmax_pv(_band_scores(q, kw, bias_t, first), vt)


def _band_scores(q, kw, bias_t, first):
    zero = jnp.zeros_like(q)
    q2 = jnp.concatenate([jnp.where(first, q, zero), jnp.where(first, zero, q)], axis=0)
    return _dot_nt(kw, q2) + bias_t


def _band_softmax_pv(s, vt):
    sb = s.shape[1] // 2
    half = ATT_HEAD_DIM
    m = jnp.max(s, axis=0, keepdims=True)
    p = jnp.exp2(s - m)
    inv_l = 1.0 / jnp.sum(p, axis=0, keepdims=True)
    o_t = _dot(vt, p.astype(BF16))
    o_pair_t = jnp.concatenate([o_t[0:half, 0:sb] * inv_l[:, 0:sb],
                                o_t[half:2 * half, sb:2 * sb] * inv_l[:, sb:2 * sb]], axis=0)
    return o_pair_t.T.astype(BF16)


def _band_prompt_kernel(q_ref, k_ref, vt_ref, base_ref, o_ref, bias_ref, sc_ref):
    length = q_ref.shape[0]
    sb, w = ATT_SB, ATT_WIN
    kj = lax.broadcasted_iota(jnp.int32, (w, sb), 0) // CHUNK
    qi = lax.broadcasted_iota(jnp.int32, (w, sb), 1) // CHUNK
    in_band = (kj >= qi) & (kj <= qi + N_LEFT_CHUNKS)
    for hh in range(2):
        toe = _toeplitz_rows(base_ref[0, hh:hh + 1, :], w, w, sb)
        bias_ref[:, hh * sb:(hh + 1) * sb] = jnp.where(in_band, toe * LOG2E, NEG)

    lane = lax.broadcasted_iota(jnp.int32, (sb, LANES), 1)
    first = lane < ATT_HEAD_DIM
    n_sub = length // sb
    n_head = min(PAST // sb, n_sub)
    for j in range(n_head):
        n = (j + 1) * sb
        o_ref[j * sb:(j + 1) * sb, :] = _band_subblock(
            q_ref[j * sb:(j + 1) * sb, :], k_ref[0:n, :], vt_ref[:, 0:n], bias_ref[w - n:, :], first)

    n_main = n_sub - n_head
    if n_main == 0:
        return
    assert n_main % 2 == 0

    def scores_into(slot, j):
        r0 = pl.multiple_of(j * sb, sb)
        k0 = pl.multiple_of(j * sb - PAST, sb)
        sc_ref[slot] = _band_scores(q_ref[pl.ds(r0, sb), :], k_ref[pl.ds(k0, w), :], bias_ref[...], first)

    def finish(slot, j):
        k0 = pl.multiple_of(j * sb - PAST, sb)
        return _band_softmax_pv(sc_ref[slot], vt_ref[:, pl.ds(k0, w)])

    scores_into(0, n_head)

    def body(i, carry):
        ja = n_head + 2 * i
        scores_into(1, ja + 1)
        o_a = finish(0, ja)
        scores_into(0, jnp.minimum(ja + 2, n_sub - 1))
        o_b = finish(1, ja + 1)
        o_ref[pl.ds(pl.multiple_of(ja * sb, sb), sb), :] = o_a
        o_ref[pl.ds(pl.multiple_of((ja + 1) * sb, sb), sb), :] = o_b
        return carry

    lax.fori_loop(0, n_main // 2, body, 0)


def _band_prompt(q, k, vt, base):
    length = q.shape[0]
    assert length % ATT_SB == 0
    ext = ATT_SB + ATT_WIN
    col = pl.BlockSpec((length, LANES), lambda c: (0, c))
    return pl.pallas_call(
        _band_prompt_kernel,
        grid=(ATT_WIDTH // LANES,),
        in_specs=[col, col, pl.BlockSpec((LANES, length), lambda c: (c, 0)),
                  pl.BlockSpec((1, 2, ext), lambda c: (c, 0, 0))],
        out_specs=col,
        out_shape=jax.ShapeDtypeStruct((length, ATT_WIDTH), BF16),
        scratch_shapes=[pltpu.VMEM((ATT_WIN, 2 * ATT_SB), F32),
                        pltpu.VMEM((2, ATT_WIN, 2 * ATT_SB), F32)],
        compiler_params=pltpu.CompilerParams(dimension_semantics=("arbitrary",),
                                             vmem_limit_bytes=VMEM_LIMIT),
        name="band_prompt",
    )(q, k, vt, base.reshape(ATT_HEADS // 2, 2, ext))


def _band_sample_kernel(q_ref, kn_ref, vn_ref, kc_ref, vc_ref, base_ref, o_ref, bias_ref):
    lq = q_ref.shape[0]
    lc = kc_ref.shape[1]

    @pl.when(pl.program_id(0) == 0)
    def _():
        for h in range(ATT_HEADS):
            bias_ref[h] = _toeplitz_rows(base_ref[h:h + 1, :], lq, LANES, lc + LANES) * LOG2E

    lane = lax.broadcasted_iota(jnp.int32, (lq, LANES), 1)
    first = lane < ATT_HEAD_DIM
    for j in range(ATT_WIDTH // LANES):
        sl = slice(j * LANES, (j + 1) * LANES)
        q = q_ref[:, sl]
        kc = kc_ref[0, :, sl].astype(BF16)
        vc = vc_ref[0, :, sl].astype(BF16)
        kn = kn_ref[:, sl]
        vn = vn_ref[:, sl]
        outs = []
        for hh in range(2):
            h = 2 * j + hh
            qm = jnp.where(first if hh == 0 else jnp.logical_not(first), q, jnp.zeros_like(q))
            sc = _dot_nt(qm, kc) + bias_ref[h, :, 0:lc]
            sn = _dot_nt(qm, kn) + bias_ref[h, :, lc:lc + lq]
            m = jnp.maximum(jnp.max(sc, axis=-1, keepdims=True), jnp.max(sn, axis=-1, keepdims=True))
            pc = jnp.exp2(sc - m)
            pn = jnp.exp2(sn - m)
            l = jnp.sum(pc, axis=-1, keepdims=True) + jnp.sum(pn, axis=-1, keepdims=True)
            o = _dot(pc.astype(BF16), vc) + _dot(pn.astype(BF16), vn)
            outs.append(o * (1.0 / l))
        o_ref[:, sl] = jnp.where(first, outs[0], outs[1]).astype(BF16)


def _band_sample(q, kn, vn, kc, vc, base, *, lq):
    rows = q.shape[0]
    nb = rows // lq
    lc = kc.shape[1]
    assert lq <= LANES and lc % LANES == 0 and base.shape == (ATT_HEADS, lc + 2 * LANES)
    blk = pl.BlockSpec((lq, ATT_WIDTH), lambda b: (b, 0))
    cache = pl.BlockSpec((1, lc, ATT_WIDTH), lambda b: (b, 0, 0))
    return pl.pallas_call(
        _band_sample_kernel,
        grid=(nb,),
        in_specs=[blk, blk, blk, cache, cache, pl.BlockSpec(base.shape, lambda b: (0, 0))],
        out_specs=blk,
        out_shape=jax.ShapeDtypeStruct((rows, ATT_WIDTH), BF16),
        scratch_shapes=[pltpu.VMEM((ATT_HEADS, lq, lc + LANES), F32)],
        compiler_params=pltpu.CompilerParams(dimension_semantics=("arbitrary",),
                                             vmem_limit_bytes=VMEM_LIMIT),
        name="band_sample",
    )(q, kn, vn, kc, vc, base)


def _mem_head_norm(y, g_ref, hd, scale):
    blk = y[:, hd * MEM_HEAD_DIM:(hd + 1) * MEM_HEAD_DIM]
    r = lax.rsqrt(jnp.mean(blk * blk, axis=-1, keepdims=True) + EPS)
    return blk * r * (g_ref[...] * scale)


def _memkv_kernel(mem_ref, gsrc_ref, wk_ref, wv_ref, gk_ref, k_ref, v_ref):
    m = _rms(mem_ref[...], gsrc_ref[...]).astype(BF16)
    k = _dot(m, wk_ref[...])
    for hd in range(MEM_HEADS):
        k_ref[:, hd * MEM_HEAD_DIM:(hd + 1) * MEM_HEAD_DIM] = _mem_head_norm(k, gk_ref, hd, 1.0)
    v_ref[...] = _dot(m, wv_ref[...])


def _memory_kv(mem2d, g_src, w_mk, w_mv, g_mk):
    rows = mem2d.shape[0]

    def full(shape):
        return pl.BlockSpec(shape, lambda i: (0,) * len(shape))

    return pl.pallas_call(
        _memkv_kernel,
        grid=(1,),
        in_specs=[full((rows, D_MODEL)), full((1, D_MODEL)), full((D_MODEL, D_MODEL)),
                  full((D_MODEL, D_MODEL)), full((1, MEM_HEAD_DIM))],
        out_specs=[full((rows, D_MODEL)), full((rows, D_MODEL))],
        out_shape=[jax.ShapeDtypeStruct((rows, D_MODEL), F32)] * 2,
        compiler_params=pltpu.CompilerParams(dimension_semantics=("arbitrary",),
                                             vmem_limit_bytes=VMEM_LIMIT),
        name="memory_kv",
    )(mem2d, g_src, w_mk, w_mv, g_mk)


def _outproj_mem_kernel(x_ref, ys_ref, ya_ref, wo_ref, gmx_ref, wq_ref, gmq_ref, mk_ref, mv_ref,
                        wmo_ref, h_ref, obuf_ref, *, nb, rb):
    h = x_ref[...] + _dot(ys_ref[...], wo_ref[0:SSD_WIDTH, :]) + _dot(ya_ref[...], wo_ref[SSD_WIDTH:, :])
    hn = _rms(h, gmx_ref[...]).astype(BF16)
    q = _dot(hn, wq_ref[...])
    for hd in range(MEM_HEADS):
        sl = slice(hd * MEM_HEAD_DIM, (hd + 1) * MEM_HEAD_DIM)
        qn = _mem_head_norm(q, gmq_ref, hd, MEM_HEAD_DIM ** -0.5).astype(BF16)
        for b in range(nb):
            rows = slice(b * rb, (b + 1) * rb)
            s = _dot_nt(qn[rows, :], mk_ref[b, :, sl])
            m = jnp.max(s, axis=-1, keepdims=True)
            p = jnp.exp(s - m)
            l = jnp.sum(p, axis=-1, keepdims=True)
            o = _dot(p.astype(BF16), mv_ref[b, :, sl]) * (1.0 / l)
            obuf_ref[rows, sl] = o.astype(BF16)
    h_ref[...] = h + _dot(obuf_ref[...], wmo_ref[...])


def _outproj_mem(x2d, ys, ya, w_out, g_mem_x, w_mq, g_mq, mk, mv, w_mo, *, tr, seq_len):
    rows = x2d.shape[0]
    n = rows // tr
    nb = max(tr // seq_len, 1)
    tiles_per_stream = max(seq_len // tr, 1)
    rb = tr // nb
    assert n * tr == rows and nb * rb == tr and mk.shape[0] * seq_len == rows

    def row(width):
        return pl.BlockSpec((tr, width), lambda i: (i, 0))

    def const(shape):
        return pl.BlockSpec(shape, lambda i: (0,) * len(shape))

    mem = pl.BlockSpec((nb, N_MEM, D_MODEL), lambda i: (i // tiles_per_stream, 0, 0))
    return pl.pallas_call(
        functools.partial(_outproj_mem_kernel, nb=nb, rb=rb),
        grid=(n,),
        in_specs=[row(D_MODEL), row(SSD_WIDTH), row(ATT_WIDTH), const((SSD_WIDTH + ATT_WIDTH, D_MODEL)),
                  const((1, D_MODEL)), const((D_MODEL, D_MODEL)), const((1, MEM_HEAD_DIM)), mem, mem,
                  const((D_MODEL, D_MODEL))],
        out_specs=row(D_MODEL),
        out_shape=jax.ShapeDtypeStruct((rows, D_MODEL), F32),
        scratch_shapes=[pltpu.VMEM((tr, D_MODEL), BF16)],
        compiler_params=pltpu.CompilerParams(dimension_semantics=("arbitrary",),
                                             vmem_limit_bytes=VMEM_LIMIT),
        name="outproj_mem",
    )(x2d, ys, ya, w_out, g_mem_x, w_mq, g_mq, mk, mv, w_mo)


FF_SLAB = 1024


def _ffn_kernel(h_ref, g_ref, w1_ref, w2_ref, y_ref):
    h = h_ref[...]
    hn = _rms(h, g_ref[...]).astype(BF16)
    acc = h
    for s in range(D_FF // FF_SLAB):
        u = jnp.maximum(_dot(hn, w1_ref[:, s * FF_SLAB:(s + 1) * FF_SLAB]), 0.0)
        acc = acc + _dot((u * u).astype(BF16), w2_ref[s * FF_SLAB:(s + 1) * FF_SLAB, :])
    y_ref[...] = acc


def _ffn(h2d, g_ffn, w1, w2, *, tr):
    rows = h2d.shape[0]
    n = rows // tr
    assert n * tr == rows

    def const(shape):
        return pl.BlockSpec(shape, lambda i: (0,) * len(shape))

    row = pl.BlockSpec((tr, D_MODEL), lambda i: (i, 0))
    return pl.pallas_call(
        _ffn_kernel,
        grid=(n,),
        in_specs=[row, const((1, D_MODEL)), const((D_MODEL, D_FF)), const((D_FF, D_MODEL))],
        out_specs=row,
        out_shape=jax.ShapeDtypeStruct((rows, D_MODEL), F32),
        compiler_params=pltpu.CompilerParams(dimension_semantics=("arbitrary",),
                                             vmem_limit_bytes=VMEM_LIMIT),
        name="ffn",
    )(h2d, g_ffn, w1, w2)


def _toeplitz_base(table, offset, width):
    heads, size = table.shape
    n_far = offset - REL_CLIP
    assert n_far >= 0
    parts = [jnp.broadcast_to(table[:, size - 1:], (heads, n_far)), table[:, ::-1]]
    rest = width - n_far - size
    if rest > 0:
        parts.append(jnp.broadcast_to(table[:, :1], (heads, rest)))
    return jnp.concatenate(parts, axis=1)[:, :width].astype(F32)


def _toeplitz_base_t(table, shift, width):
    heads, size = table.shape
    n_low = shift - REL_CLIP
    assert n_low >= 0
    parts = [jnp.broadcast_to(table[:, :1], (heads, n_low)), table]
    rest = width - n_low - size
    if rest > 0:
        parts.append(jnp.broadcast_to(table[:, size - 1:], (heads, rest)))
    return jnp.concatenate(parts, axis=1)[:, :width].astype(F32)


def _prep_weights(g_mix, w_in, conv_w, conv_b, ssd_A_log, ssd_dt_bias, ssd_D, ssd_g_out, att_g_q, att_g_k,
                  w_out, g_mem_x, g_mem_src, w_mq, w_mk, w_mv, g_mq, g_mk, w_mo, g_ffn, w_ff1, w_ff2):
    o_dt = SSD_WIDTH + SSD_CONV_DIM
    w_cat = jnp.concatenate(
        [w_in[:, :o_dt], jnp.pad(w_in[:, o_dt:o_dt + SSD_HEADS], ((0, 0), (0, DT_PAD - SSD_HEADS))),
         w_in[:, o_dt + SSD_HEADS:]], axis=1).astype(BF16)
    pad_h = lambda v: jnp.pad(v, (0, DT_PAD - SSD_HEADS)).reshape(1, DT_PAD)
    row = lambda v: v.reshape(1, -1)
    expand = (jnp.arange(LANES)[:, None] == jnp.arange(SSD_WIDTH)[None, :] // SSD_HEAD_DIM).astype(BF16)
    return dict(
        g_mix=row(g_mix), w_cat=w_cat, w_vt=w_in[:, IN_WIDTH - ATT_WIDTH:].T.astype(BF16),
        gq_t=row(jnp.tile(att_g_q, ATT_HEADS)), gk_t=row(jnp.tile(att_g_k, ATT_HEADS)),
        conv_w=conv_w, conv_b=row(conv_b), dtb=pad_h(ssd_dt_bias), alog=pad_h(ssd_A_log),
        dskip=row(jnp.repeat(ssd_D, SSD_HEAD_DIM)), gout=row(ssd_g_out), expand=expand,
        w_out=w_out.astype(BF16), g_mem_x=row(g_mem_x), g_mem_src=row(g_mem_src),
        w_mq=w_mq.astype(BF16), w_mk=w_mk.astype(BF16), w_mv=w_mv.astype(BF16),
        g_mq=row(g_mq), g_mk=row(g_mk), w_mo=w_mo.astype(BF16), g_ffn=row(g_ffn),
        w_ff1=w_ff1.astype(BF16), w_ff2=w_ff2.astype(BF16))


def _layer(x, conv_prev, h0, k_cache, v_cache, mem_k, mem_v, p, rel, *, tr, t_scan):
    b, length, _ = x.shape
    rows = b * length
    x2d = x.reshape(rows, D_MODEL)
    prompt = k_cache is None
    tail_rows = PAST if prompt else rows
    z, xbc, dtraw, q, k, v, k_tail, v_tail = _in_proj(
        x2d, p["g_mix"], p["w_cat"], p["w_vt"], p["gq_t"], p["gk_t"], tr=tr, tail_rows=tail_rows,
        v_feature_major=prompt)

    cprev8 = jnp.pad(conv_prev, ((0, 0), (8 - (SSD_CONV - 1), 0), (0, 0)))
    lb = t_scan if prompt else length
    y_ssd, h_fin = _ssd_mixer(
        xbc.reshape(b, length, SSD_CONV_DIM), dtraw.reshape(b, length, DT_PAD),
        z.reshape(b, length, SSD_WIDTH), cprev8, h0.reshape(b, SSD_WIDTH, SSD_STATE),
        p["conv_w"], p["conv_b"], p["dtb"], p["alog"], p["dskip"], p["gout"], p["expand"],
        t=t_scan, lb=lb)
    conv_new = xbc.reshape(b, length, SSD_CONV_DIM)[:, length - (SSD_CONV - 1):]

    if prompt:
        y_att = _band_prompt(q, k, v, _toeplitz_base_t(rel, ATT_SB, ATT_SB + ATT_WIN))
        k_rows = k_tail.reshape(b, PAST, ATT_HEADS, ATT_HEAD_DIM)
        v_rows = v_tail.reshape(b, PAST, ATT_HEADS, ATT_HEAD_DIM)
    else:
        lc = k_cache.shape[1]
        y_att = _band_sample(q, k, v, k_cache.reshape(b, lc, ATT_WIDTH), v_cache.reshape(b, lc, ATT_WIDTH),
                             _toeplitz_base(rel, lc + LANES, lc + 2 * LANES), lq=length)
        k_rows = k_tail.reshape(b, length, ATT_HEADS, ATT_HEAD_DIM)
        v_rows = v_tail.reshape(b, length, ATT_HEADS, ATT_HEAD_DIM)

    h = _outproj_mem(x2d, y_ssd.reshape(rows, SSD_WIDTH), y_att, p["w_out"], p["g_mem_x"], p["w_mq"],
                     p["g_mq"], mem_k.astype(BF16), mem_v.astype(BF16), p["w_mo"], tr=tr, seq_len=length)
    y = _ffn(h, p["g_ffn"], p["w_ff1"], p["w_ff2"], tr=tr)
    return (y.reshape(b, length, D_MODEL), h_fin.reshape(b, SSD_HEADS, SSD_HEAD_DIM, SSD_STATE),
            conv_new, k_rows, v_rows)


def kernel(x_prompt, x_sample, mem_prompt, state_ssd, state_conv, cache_attn_k, cache_attn_v, cache_mem_k,
           cache_mem_v, g_mix, w_in, conv_w, conv_b, ssd_A_log, ssd_dt_bias, ssd_D, ssd_g_out, att_g_q,
           att_g_k, att_rel_bias, w_out, g_mem_x, g_mem_src, w_mq, w_mk, w_mv, g_mq, g_mk, w_mo, g_ffn,
           w_ff1, w_ff2):
    depth = g_mix.shape[0]
    b_p, seq, _ = x_prompt.shape
    b_s, dec_seq, _ = x_sample.shape
    yp, ys = x_prompt, x_sample
    outs = [[] for _ in range(10)]
    for l in range(depth):
        p = _prep_weights(g_mix[l], w_in[l], conv_w[l], conv_b[l], ssd_A_log[l], ssd_dt_bias[l], ssd_D[l],
                          ssd_g_out[l], att_g_q[l], att_g_k[l], w_out[l], g_mem_x[l], g_mem_src[l],
                          w_mq[l], w_mk[l], w_mv[l], g_mq[l], g_mk[l], w_mo[l], g_ffn[l], w_ff1[l], w_ff2[l])
        rel = att_rel_bias[l]
        mk, mv = _memory_kv(mem_prompt.reshape(b_p * N_MEM, D_MODEL), p["g_mem_src"], p["w_mk"], p["w_mv"],
                            p["g_mk"])
        mk = mk.reshape(b_p, N_MEM, D_MODEL)
        mv = mv.reshape(b_p, N_MEM, D_MODEL)
        conv0 = jnp.zeros((b_p, SSD_CONV - 1, SSD_CONV_DIM), F32)
        h00 = jnp.zeros((b_p, SSD_HEADS, SSD_HEAD_DIM, SSD_STATE), F32)
        yp, hp, cp, kp, vp = _layer(yp, conv0, h00, None, None, mk, mv, p, rel,
                                    tr=min(256, seq), t_scan=min(256, seq))
        ys, hs, cs, ks_, vs_ = _layer(ys, state_conv[l], state_ssd[l], cache_attn_k[l], cache_attn_v[l],
                                      cache_mem_k[l].reshape(b_s, N_MEM, D_MODEL),
                                      cache_mem_v[l].reshape(b_s, N_MEM, D_MODEL), p, rel,
                                      tr=b_s * dec_seq, t_scan=128)
        for lst, val in zip(outs, (hp, cp, kp, vp,
                                   mk.reshape(b_p, N_MEM, MEM_HEADS, MEM_HEAD_DIM),
                                   mv.reshape(b_p, N_MEM, MEM_HEADS, MEM_HEAD_DIM),
                                   hs, cs, ks_, vs_)):
            lst.append(val)
    return (yp, ys) + tuple(jnp.stack(o) for o in outs)
```

```python
import functools

import jax
import jax.numpy as jnp
from jax import lax
from jax.experimental import pallas as pl
from jax.experimental.pallas import tpu as pltpu

F32 = jnp.float32
BF16 = jnp.bfloat16

D_MODEL = 1024
CHUNK = 64
SSD_HEADS = 16
SSD_HEAD_DIM = 64
SSD_WIDTH = SSD_HEADS * SSD_HEAD_DIM
SSD_GROUPS = 2
SSD_STATE = 128
SSD_CONV = 4
SSD_CONV_DIM = SSD_WIDTH + 2 * SSD_GROUPS * SSD_STATE
ATT_HEADS = 16
ATT_HEAD_DIM = 64
ATT_WIDTH = ATT_HEADS * ATT_HEAD_DIM
N_LEFT_CHUNKS = 8
PAST = N_LEFT_CHUNKS * CHUNK
REL_CLIP = 128
N_MEM = 256
MEM_HEADS = 4
MEM_HEAD_DIM = D_MODEL // MEM_HEADS
D_FF = 4 * D_MODEL
EPS = 1e-6

LANES = 128
DT_PAD = LANES
IN_WIDTH = SSD_WIDTH + SSD_CONV_DIM + SSD_HEADS + 3 * ATT_WIDTH
IN_PAD_WIDTH = SSD_WIDTH + SSD_CONV_DIM + DT_PAD + 3 * ATT_WIDTH
NEG = -1e30
LOG2E = 1.4426950408889634
VMEM_LIMIT = 56 * 1024 * 1024


def _rms(x, g):
    return x * lax.rsqrt(jnp.mean(x * x, axis=-1, keepdims=True) + EPS) * g


def _silu(x):
    return x * jax.nn.sigmoid(x)


def _split2(x):
    hi = x.astype(BF16)
    lo = (x - hi.astype(F32)).astype(BF16)
    return hi, lo


def _split3(x):
    hi = x.astype(BF16)
    r = x - hi.astype(F32)
    mid = r.astype(BF16)
    lo = (r - mid.astype(F32)).astype(BF16)
    return hi, mid, lo


def _dot(a, b):
    return jnp.dot(a, b, preferred_element_type=F32)


def _dot_nt(a, b):
    return lax.dot_general(a, b, (((1,), (1,)), ((), ())), preferred_element_type=F32)


def _inproj_kernel(x_ref, gmix_ref, wzx_ref, wdt_ref, wqkv_ref, wvt_ref, gq_ref, gk_ref,
                   z_ref, xbc_ref, dt_ref, q_ref, k_ref, v_ref, kt_ref, vt_ref, *, n_tail, v_feature_major):
    i = pl.program_id(0)
    n = pl.num_programs(0)
    tr = x_ref.shape[0]
    xn = _rms(x_ref[...], gmix_ref[...]).astype(BF16)

    def proj(loc, width):
        w_ref, lo = loc
        return _dot(xn, w_ref[:, lo:lo + width])

    o_xbc = (wzx_ref, SSD_WIDTH)
    o_dt = (wdt_ref, 0)
    o_q = (wqkv_ref, 0)
    o_k = (wqkv_ref, ATT_WIDTH)
    o_v = (wqkv_ref, 2 * ATT_WIDTH)
    lane = lax.broadcasted_iota(jnp.int32, (tr, LANES), 1)
    first = lane < ATT_HEAD_DIM

    def head_norm(y, g_ref, c, scale):
        blk = y[:, c * LANES:(c + 1) * LANES]
        sq = blk * blk
        s0 = jnp.sum(jnp.where(first, sq, 0.0), axis=-1, keepdims=True)
        s1 = jnp.sum(jnp.where(first, 0.0, sq), axis=-1, keepdims=True)
        r = jnp.where(first, lax.rsqrt(s0 * (1.0 / ATT_HEAD_DIM) + EPS),
                      lax.rsqrt(s1 * (1.0 / ATT_HEAD_DIM) + EPS))
        return blk * r * (g_ref[:, c * LANES:(c + 1) * LANES] * scale)

    q = proj(o_q, ATT_WIDTH)
    k = proj(o_k, ATT_WIDTH)
    for c in range(ATT_WIDTH // LANES):
        q_ref[:, c * LANES:(c + 1) * LANES] = head_norm(
            q, gq_ref, c, ATT_HEAD_DIM ** -0.5 * LOG2E).astype(BF16)
    if v_feature_major:
        v_ref[...] = _dot_nt(wvt_ref[...], xn).astype(BF16)
    else:
        v = proj(o_v, ATT_WIDTH)
        v_ref[...] = v.astype(BF16)
        vt_ref[...] = v
    for c in range(ATT_WIDTH // LANES):
        kn = head_norm(k, gk_ref, c, 1.0)
        k_ref[:, c * LANES:(c + 1) * LANES] = kn.astype(BF16)
        kt_ref[:, c * LANES:(c + 1) * LANES] = kn
    z_ref[...] = proj((wzx_ref, 0), SSD_WIDTH)
    xbc_ref[...] = proj(o_xbc, SSD_CONV_DIM)
    dt_ref[...] = proj(o_dt, DT_PAD)
    if v_feature_major:
        @pl.when(i >= n - n_tail)
        def _():
            vt_ref[...] = proj(o_v, ATT_WIDTH)


def _in_proj(x2d, g_mix, w_zx, w_dt, w_qkv, w_vt, gq_t, gk_t, *, tr, tail_rows, v_feature_major):
    rows = x2d.shape[0]
    n = rows // tr
    n_tail = tail_rows // tr
    assert n * tr == rows and n_tail * tr == tail_rows

    def row(width):
        return pl.BlockSpec((tr, width), lambda i: (i, 0))

    def const(shape):
        return pl.BlockSpec(shape, lambda i: (0,) * len(shape))

    tail = pl.BlockSpec((tr, ATT_WIDTH), lambda i: (jnp.maximum(i - (n - n_tail), 0), 0))
    if v_feature_major:
        v_spec = pl.BlockSpec((ATT_WIDTH, tr), lambda i: (0, i))
        v_shape = jax.ShapeDtypeStruct((ATT_WIDTH, rows), BF16)
    else:
        v_spec = row(ATT_WIDTH)
        v_shape = jax.ShapeDtypeStruct((rows, ATT_WIDTH), BF16)
    return pl.pallas_call(
        functools.partial(_inproj_kernel, n_tail=n_tail, v_feature_major=v_feature_major),
        grid=(n,),
        in_specs=[row(D_MODEL), const((1, D_MODEL)), const((D_MODEL, SSD_WIDTH + SSD_CONV_DIM)),
                  const((D_MODEL, DT_PAD)), const((D_MODEL, 3 * ATT_WIDTH)),
                  const((ATT_WIDTH, D_MODEL)), const((1, ATT_WIDTH)), const((1, ATT_WIDTH))],
        out_specs=[row(SSD_WIDTH), row(SSD_CONV_DIM), row(DT_PAD), row(ATT_WIDTH), row(ATT_WIDTH),
                   v_spec, tail, tail],
        out_shape=[jax.ShapeDtypeStruct((rows, SSD_WIDTH), F32),
                   jax.ShapeDtypeStruct((rows, SSD_CONV_DIM), F32),
                   jax.ShapeDtypeStruct((rows, DT_PAD), F32),
                   jax.ShapeDtypeStruct((rows, ATT_WIDTH), BF16),
                   jax.ShapeDtypeStruct((rows, ATT_WIDTH), BF16),
                   v_shape,
                   jax.ShapeDtypeStruct((tail_rows, ATT_WIDTH), F32),
                   jax.ShapeDtypeStruct((tail_rows, ATT_WIDTH), F32)],
        compiler_params=pltpu.CompilerParams(dimension_semantics=("arbitrary",),
                                             vmem_limit_bytes=VMEM_LIMIT),
        name="in_proj",
    )(x2d, g_mix, w_zx, w_dt, w_qkv, w_vt, gq_t, gk_t)


def _ssd_kernel(xbc_ref, dtraw_ref, z_ref, cprev_ref, h0_ref, convw_ref, convb_ref, dtb_ref,
                alog_ref, dskip_ref, gout_ref, expand_ref,
                y_ref, hfin_ref,
                ext_ref, xc_ref, ht_ref, ybuf_ref, *, t, lb):
    c = pl.program_id(1)
    nc = pl.num_programs(1)
    halo = 8

    @pl.when(c == 0)
    def _():
        ext_ref[0:halo, :] = cprev_ref[0]
        ht_ref[...] = h0_ref[0].T

    @pl.when(c > 0)
    def _():
        ext_ref[0:halo, :] = ext_ref[t:t + halo, :]

    ext_ref[halo:halo + lb, :] = xbc_ref[0]
    if lb < t:
        ext_ref[halo + lb:halo + t, :] = jnp.zeros((t - lb, SSD_CONV_DIM), F32)

    for cb in range(SSD_CONV_DIM // LANES):
        sl = slice(cb * LANES, (cb + 1) * LANES)
        acc = convb_ref[:, sl] + convw_ref[SSD_CONV - 1:SSD_CONV, sl] * ext_ref[halo:halo + t, sl]
        for j in range(1, SSD_CONV):
            acc = acc + convw_ref[SSD_CONV - 1 - j:SSD_CONV - j, sl] * ext_ref[halo - j:halo - j + t, sl]
        xc_ref[:, sl] = _silu(acc)

    lane = lax.broadcasted_iota(jnp.int32, (t, LANES), 1)
    rowi = lax.broadcasted_iota(jnp.int32, (t, LANES), 0)
    dtraw = dtraw_ref[0]
    if lb < t:
        dtraw = jnp.concatenate([dtraw, jnp.zeros((t - lb, DT_PAD), F32)], axis=0)
    dt = jax.nn.softplus(dtraw + dtb_ref[...])
    dt = jnp.where((lane < SSD_HEADS) & (rowi < lb), dt, 0.0)
    a_neg = -jnp.exp(alog_ref[...]) * LOG2E
    a = dt * a_neg

    rr = lax.broadcasted_iota(jnp.int32, (t, t), 0)
    cc = lax.broadcasted_iota(jnp.int32, (t, t), 1)
    causal = rr >= cc
    tril = jnp.where(causal, 1.0, 0.0).astype(BF16)
    a1, a2, a3 = _split3(a)
    a_cum = _dot(tril, a1) + _dot(tril, a2) + _dot(tril, a3)
    a_last = a_cum[t - 1:t, :]
    ea = jnp.exp2(a_cum)
    cd = jnp.exp2(a_last)
    a_t = a_cum.T
    w_t = jnp.exp2(a_t[:, t - 1:t] - a_t)

    stacked = jnp.concatenate([ea, dt, jnp.broadcast_to(cd, (8, LANES))], axis=0)
    s_hi, s_lo = _split2(stacked)
    expanded = _dot(s_hi, expand_ref[...]) + _dot(s_lo, expand_ref[...])
    ea_x = expanded[0:t, :]
    dt_x = expanded[t:2 * t, :]
    cd_x = expanded[2 * t:2 * t + 1, :]

    first = lane < SSD_HEAD_DIM
    heads_per_group = SSD_HEADS // SSD_GROUPS
    o_b = SSD_WIDTH
    o_c = SSD_WIDTH + SSD_GROUPS * SSD_STATE
    ssq = jnp.zeros((lb, 1), F32)
    for g in range(SSD_GROUPS):
        bg = xc_ref[:, o_b + g * SSD_STATE:o_b + (g + 1) * SSD_STATE]
        cg = xc_ref[:, o_c + g * SSD_STATE:o_c + (g + 1) * SSD_STATE].astype(BF16)
        cb_mat = jnp.where(causal, _dot_nt(cg, bg.astype(BF16)), 0.0)
        bg_t = bg.T
        for jp in range(heads_per_group // 2):
            j = g * (heads_per_group // 2) + jp
            sl = slice(j * LANES, (j + 1) * LANES)
            xh = xc_ref[:, sl]
            xdt = xh * dt_x[:, sl]
            y_pair = None
            s_pair = None
            for hh in range(2):
                h = 2 * j + hh
                xm = jnp.where(first if hh == 0 else jnp.logical_not(first), xdt, 0.0).astype(BF16)
                seg = a_cum[:, h:h + 1] - a_t[h:h + 1, :]
                m = (cb_mat * jnp.exp2(jnp.minimum(seg, 0.0))).astype(BF16)
                yd = _dot(m, xm)
                bw = (bg_t * w_t[h:h + 1, :]).astype(BF16)
                sd = _dot(bw, xm)
                y_pair = yd if y_pair is None else y_pair + yd
                s_pair = sd if s_pair is None else s_pair + sd
            h_in = ht_ref[:, sl]
            y_off = _dot(cg, h_in.astype(BF16)) * ea_x[:, sl]
            ht_ref[:, sl] = cd_x[:, sl] * h_in + s_pair
            y = y_pair + y_off + dskip_ref[:, sl] * xh
            yg = y[0:lb, :] * _silu(z_ref[0, :, sl])
            ybuf_ref[:, sl] = yg
            ssq = ssq + jnp.sum(yg * yg, axis=-1, keepdims=True)

    r = lax.rsqrt(ssq * (1.0 / SSD_WIDTH) + EPS)
    y_ref[0] = (ybuf_ref[...] * r * gout_ref[...]).astype(BF16)

    @pl.when(c == nc - 1)
    def _():
        hfin_ref[0] = ht_ref[...].T


def _ssd_mixer(xbc, dtraw, z, cprev8, h0, conv_w, conv_b, dtb, alog, dskip, gout, expand, *, t, lb):
    b, length, _ = xbc.shape
    nc = length // lb
    assert nc * lb == length and (lb == t or nc == 1)

    def seq(width):
        return pl.BlockSpec((1, lb, width), lambda bi, ci: (bi, ci, 0))

    def per_b(shape):
        return pl.BlockSpec((1,) + shape, lambda bi, ci: (bi, 0, 0))

    def const(shape):
        return pl.BlockSpec(shape, lambda bi, ci: (0,) * len(shape))

    hp = SSD_WIDTH
    return pl.pallas_call(
        functools.partial(_ssd_kernel, t=t, lb=lb),
        grid=(b, nc),
        in_specs=[seq(SSD_CONV_DIM), seq(DT_PAD), seq(SSD_WIDTH), per_b((8, SSD_CONV_DIM)),
                  per_b((hp, SSD_STATE)), const((SSD_CONV, SSD_CONV_DIM)), const((1, SSD_CONV_DIM)),
                  const((1, DT_PAD)), const((1, DT_PAD)), const((1, SSD_WIDTH)), const((1, SSD_WIDTH)),
                  const((LANES, SSD_WIDTH))],
        out_specs=[seq(SSD_WIDTH), per_b((hp, SSD_STATE))],
        out_shape=[jax.ShapeDtypeStruct((b, length, SSD_WIDTH), BF16),
                   jax.ShapeDtypeStruct((b, hp, SSD_STATE), F32)],
        scratch_shapes=[pltpu.VMEM((t + 8, SSD_CONV_DIM), F32),
                        pltpu.VMEM((t, SSD_CONV_DIM), F32),
                        pltpu.VMEM((SSD_STATE, hp), F32),
                        pltpu.VMEM((lb, SSD_WIDTH), F32)],
        compiler_params=pltpu.CompilerParams(dimension_semantics=("arbitrary", "arbitrary"),
                                             vmem_limit_bytes=VMEM_LIMIT),
        name="ssd_mixer",
    )(xbc, dtraw, z, cprev8, h0, conv_w, conv_b, dtb, alog, dskip, gout, expand)


ATT_SB = 2 * CHUNK
ATT_WIN = ATT_SB + PAST


def _toeplitz_rows(base_row, rows, offset, width):
    ext = base_row.shape[1]
    rolled = pltpu.roll(jnp.broadcast_to(base_row, (rows, ext)), 0, 1, stride=1, stride_axis=0)
    return rolled[:, offset:offset + width]


def _band_subblock(q, kw, vt, bias_t, first):
    return _band_softmax_pv(_band_scores(q, kw, bias_t, first), vt)


def _band_scores(q, kw, bias_t, first):
    zero = jnp.zeros_like(q)
    q2 = jnp.concatenate([jnp.where(first, q, zero), jnp.where(first, zero, q)], axis=0)
    return _dot_nt(kw, q2) + bias_t


def _band_softmax_pv(s, vt):
    sb = s.shape[1] // 2
    half = ATT_HEAD_DIM
    m = jnp.max(s, axis=0, keepdims=True)
    p = jnp.exp2(s - m)
    inv_l = 1.0 / jnp.sum(p, axis=0, keepdims=True)
    o_t = _dot(vt, p.astype(BF16))
    o_pair_t = jnp.concatenate([o_t[0:half, 0:sb] * inv_l[:, 0:sb],
                                o_t[half:2 * half, sb:2 * sb] * inv_l[:, sb:2 * sb]], axis=0)
    return o_pair_t.T.astype(BF16)


def _band_prompt_kernel(q_ref, k_ref, vt_ref, base_ref, o_ref, bias_ref, sc_ref):
    length = q_ref.shape[0]
    sb, w = ATT_SB, ATT_WIN
    kj = lax.broadcasted_iota(jnp.int32, (w, sb), 0) // CHUNK
    qi = lax.broadcasted_iota(jnp.int32, (w, sb), 1) // CHUNK
    in_band = (kj >= qi) & (kj <= qi + N_LEFT_CHUNKS)
    for hh in range(2):
        toe = _toeplitz_rows(base_ref[0, hh:hh + 1, :], w, w, sb)
        bias_ref[:, hh * sb:(hh + 1) * sb] = jnp.where(in_band, toe * LOG2E, NEG)

    lane = lax.broadcasted_iota(jnp.int32, (sb, LANES), 1)
    first = lane < ATT_HEAD_DIM
    n_sub = length // sb
    n_head = min(PAST // sb, n_sub)
    for j in range(n_head):
        n = (j + 1) * sb
        o_ref[j * sb:(j + 1) * sb, :] = _band_subblock(
            q_ref[j * sb:(j + 1) * sb, :], k_ref[0:n, :], vt_ref[:, 0:n], bias_ref[w - n:, :], first)

    n_main = n_sub - n_head
    if n_main == 0:
        return
    assert n_main % 2 == 0

    def scores_into(slot, j):
        r0 = pl.multiple_of(j * sb, sb)
        k0 = pl.multiple_of(j * sb - PAST, sb)
        sc_ref[slot] = _band_scores(q_ref[pl.ds(r0, sb), :], k_ref[pl.ds(k0, w), :], bias_ref[...], first)

    def finish(slot, j):
        k0 = pl.multiple_of(j * sb - PAST, sb)
        return _band_softmax_pv(sc_ref[slot], vt_ref[:, pl.ds(k0, w)])

    scores_into(0, n_head)

    def body(i, carry):
        ja = n_head + 2 * i
        scores_into(1, ja + 1)
        o_a = finish(0, ja)
        scores_into(0, jnp.minimum(ja + 2, n_sub - 1))
        o_b = finish(1, ja + 1)
        o_ref[pl.ds(pl.multiple_of(ja * sb, sb), sb), :] = o_a
        o_ref[pl.ds(pl.multiple_of((ja + 1) * sb, sb), sb), :] = o_b
        return carry

    lax.fori_loop(0, n_main // 2, body, 0)


def _band_prompt(q, k, vt, base):
    length = q.shape[0]
    assert length % ATT_SB == 0
    ext = ATT_SB + ATT_WIN
    col = pl.BlockSpec((length, LANES), lambda c: (0, c))
    return pl.pallas_call(
        _band_prompt_kernel,
        grid=(ATT_WIDTH // LANES,),
        in_specs=[col, col, pl.BlockSpec((LANES, length), lambda c: (c, 0)),
                  pl.BlockSpec((1, 2, ext), lambda c: (c, 0, 0))],
        out_specs=col,
        out_shape=jax.ShapeDtypeStruct((length, ATT_WIDTH), BF16),
        scratch_shapes=[pltpu.VMEM((ATT_WIN, 2 * ATT_SB), F32),
                        pltpu.VMEM((2, ATT_WIN, 2 * ATT_SB), F32)],
        compiler_params=pltpu.CompilerParams(dimension_semantics=("arbitrary",),
                                             vmem_limit_bytes=VMEM_LIMIT),
        name="band_prompt",
    )(q, k, vt, base.reshape(ATT_HEADS // 2, 2, ext))


def _band_sample_kernel(q_ref, kn_ref, vn_ref, kc_ref, vc_ref, base_ref, o_ref, bias_ref):
    lq = q_ref.shape[0]
    lc = kc_ref.shape[1]

    @pl.when(pl.program_id(0) == 0)
    def _():
        for h in range(ATT_HEADS):
            bias_ref[h] = _toeplitz_rows(base_ref[h:h + 1, :], lq, LANES, lc + LANES) * LOG2E

    lane = lax.broadcasted_iota(jnp.int32, (lq, LANES), 1)
    first = lane < ATT_HEAD_DIM
    for j in range(ATT_WIDTH // LANES):
        sl = slice(j * LANES, (j + 1) * LANES)
        q = q_ref[:, sl]
        kc = kc_ref[0, :, sl]
        vc = vc_ref[0, :, sl]
        kn = kn_ref[:, sl]
        vn = vn_ref[:, sl]
        outs = []
        for hh in range(2):
            h = 2 * j + hh
            qm = jnp.where(first if hh == 0 else jnp.logical_not(first), q, jnp.zeros_like(q))
            sc = _dot_nt(qm, kc) + bias_ref[h, :, 0:lc]
            sn = _dot_nt(qm, kn) + bias_ref[h, :, lc:lc + lq]
            m = jnp.maximum(jnp.max(sc, axis=-1, keepdims=True), jnp.max(sn, axis=-1, keepdims=True))
            pc = jnp.exp2(sc - m)
            pn = jnp.exp2(sn - m)
            l = jnp.sum(pc, axis=-1, keepdims=True) + jnp.sum(pn, axis=-1, keepdims=True)
            o = _dot(pc.astype(BF16), vc) + _dot(pn.astype(BF16), vn)
            outs.append(o * (1.0 / l))
        o_ref[:, sl] = jnp.where(first, outs[0], outs[1]).astype(BF16)


def _band_sample(q, kn, vn, kc, vc, base, *, lq):
    rows = q.shape[0]
    nb = rows // lq
    lc = kc.shape[1]
    assert lq <= LANES and lc % LANES == 0 and base.shape == (ATT_HEADS, lc + 2 * LANES)
    blk = pl.BlockSpec((lq, ATT_WIDTH), lambda b: (b, 0))
    cache = pl.BlockSpec((1, lc, ATT_WIDTH), lambda b: (b, 0, 0))
    return pl.pallas_call(
        _band_sample_kernel,
        grid=(nb,),
        in_specs=[blk, blk, blk, cache, cache, pl.BlockSpec(base.shape, lambda b: (0, 0))],
        out_specs=blk,
        out_shape=jax.ShapeDtypeStruct((rows, ATT_WIDTH), BF16),
        scratch_shapes=[pltpu.VMEM((ATT_HEADS, lq, lc + LANES), F32)],
        compiler_params=pltpu.CompilerParams(dimension_semantics=("arbitrary",),
                                             vmem_limit_bytes=VMEM_LIMIT),
        name="band_sample",
    )(q, kn, vn, kc, vc, base)


def _mem_head_norm(y, g_ref, hd, scale):
    blk = y[:, hd * MEM_HEAD_DIM:(hd + 1) * MEM_HEAD_DIM]
    r = lax.rsqrt(jnp.mean(blk * blk, axis=-1, keepdims=True) + EPS)
    return blk * r * (g_ref[...] * scale)


def _memkv_kernel(mem_ref, gsrc_ref, wk_ref, wv_ref, gk_ref, k_ref, v_ref):
    m = _rms(mem_ref[...], gsrc_ref[...]).astype(BF16)
    k = _dot(m, wk_ref[...])
    for hd in range(MEM_HEADS):
        k_ref[:, hd * MEM_HEAD_DIM:(hd + 1) * MEM_HEAD_DIM] = _mem_head_norm(k, gk_ref, hd, 1.0)
    v_ref[...] = _dot(m, wv_ref[...])


def _memory_kv(mem2d, g_src, w_mk, w_mv, g_mk):
    rows = mem2d.shape[0]

    def full(shape):
        return pl.BlockSpec(shape, lambda i: (0,) * len(shape))

    return pl.pallas_call(
        _memkv_kernel,
        grid=(1,),
        in_specs=[full((rows, D_MODEL)), full((1, D_MODEL)), full((D_MODEL, D_MODEL)),
                  full((D_MODEL, D_MODEL)), full((1, MEM_HEAD_DIM))],
        out_specs=[full((rows, D_MODEL)), full((rows, D_MODEL))],
        out_shape=[jax.ShapeDtypeStruct((rows, D_MODEL), F32)] * 2,
        compiler_params=pltpu.CompilerParams(dimension_semantics=("arbitrary",),
                                             vmem_limit_bytes=VMEM_LIMIT),
        name="memory_kv",
    )(mem2d, g_src, w_mk, w_mv, g_mk)


def _outproj_mem_kernel(x_ref, ys_ref, ya_ref, wo_ref, gmx_ref, wq_ref, gmq_ref, mk_ref, mv_ref,
                        wmo_ref, h_ref, obuf_ref, *, nb, rb):
    h = x_ref[...] + _dot(ys_ref[...], wo_ref[0:SSD_WIDTH, :]) + _dot(ya_ref[...], wo_ref[SSD_WIDTH:, :])
    hn = _rms(h, gmx_ref[...]).astype(BF16)
    q = _dot(hn, wq_ref[...])
    for hd in range(MEM_HEADS):
        sl = slice(hd * MEM_HEAD_DIM, (hd + 1) * MEM_HEAD_DIM)
        qn = _mem_head_norm(q, gmq_ref, hd, MEM_HEAD_DIM ** -0.5).astype(BF16)
        for b in range(nb):
            rows = slice(b * rb, (b + 1) * rb)
            s = _dot_nt(qn[rows, :], mk_ref[b, :, sl])
            m = jnp.max(s, axis=-1, keepdims=True)
            p = jnp.exp(s - m)
            l = jnp.sum(p, axis=-1, keepdims=True)
            o = _dot(p.astype(BF16), mv_ref[b, :, sl]) * (1.0 / l)
            obuf_ref[rows, sl] = o.astype(BF16)
    h_ref[...] = h + _dot(obuf_ref[...], wmo_ref[...])


def _outproj_mem(x2d, ys, ya, w_out, g_mem_x, w_mq, g_mq, mk, mv, w_mo, *, tr, seq_len):
    rows = x2d.shape[0]
    n = rows // tr
    nb = max(tr // seq_len, 1)
    tiles_per_stream = max(seq_len // tr, 1)
    rb = tr // nb
    assert n * tr == rows and nb * rb == tr and mk.shape[0] * seq_len == rows

    def row(width):
        return pl.BlockSpec((tr, width), lambda i: (i, 0))

    def const(shape):
        return pl.BlockSpec(shape, lambda i: (0,) * len(shape))

    mem = pl.BlockSpec((nb, N_MEM, D_MODEL), lambda i: (i // tiles_per_stream, 0, 0))
    return pl.pallas_call(
        functools.partial(_outproj_mem_kernel, nb=nb, rb=rb),
        grid=(n,),
        in_specs=[row(D_MODEL), row(SSD_WIDTH), row(ATT_WIDTH), const((SSD_WIDTH + ATT_WIDTH, D_MODEL)),
                  const((1, D_MODEL)), const((D_MODEL, D_MODEL)), const((1, MEM_HEAD_DIM)), mem, mem,
                  const((D_MODEL, D_MODEL))],
        out_specs=row(D_MODEL),
        out_shape=jax.ShapeDtypeStruct((rows, D_MODEL), F32),
        scratch_shapes=[pltpu.VMEM((tr, D_MODEL), BF16)],
        compiler_params=pltpu.CompilerParams(dimension_semantics=("arbitrary",),
                                             vmem_limit_bytes=VMEM_LIMIT),
        name="outproj_mem",
    )(x2d, ys, ya, w_out, g_mem_x, w_mq, g_mq, mk, mv, w_mo)


FF_SLAB = 1024


def _ffn_kernel(h_ref, g_ref, w1_ref, w2_ref, y_ref):
    h = h_ref[...]
    hn = _rms(h, g_ref[...]).astype(BF16)
    acc = h
    for s in range(D_FF // FF_SLAB):
        u = jnp.maximum(_dot(hn, w1_ref[:, s * FF_SLAB:(s + 1) * FF_SLAB]), 0.0)
        acc = acc + _dot((u * u).astype(BF16), w2_ref[s * FF_SLAB:(s + 1) * FF_SLAB, :])
    y_ref[...] = acc


def _ffn(h2d, g_ffn, w1, w2, *, tr):
    rows = h2d.shape[0]
    n = rows // tr
    assert n * tr == rows

    def const(shape):
        return pl.BlockSpec(shape, lambda i: (0,) * len(shape))

    row = pl.BlockSpec((tr, D_MODEL), lambda i: (i, 0))
    return pl.pallas_call(
        _ffn_kernel,
        grid=(n,),
        in_specs=[row, const((1, D_MODEL)), const((D_MODEL, D_FF)), const((D_FF, D_MODEL))],
        out_specs=row,
        out_shape=jax.ShapeDtypeStruct((rows, D_MODEL), F32),
        compiler_params=pltpu.CompilerParams(dimension_semantics=("arbitrary",),
                                             vmem_limit_bytes=VMEM_LIMIT),
        name="ffn",
    )(h2d, g_ffn, w1, w2)


def _toeplitz_base(table, offset, width):
    heads, size = table.shape
    n_far = offset - REL_CLIP
    assert n_far >= 0
    parts = [jnp.broadcast_to(table[:, size - 1:], (heads, n_far)), table[:, ::-1]]
    rest = width - n_far - size
    if rest > 0:
        parts.append(jnp.broadcast_to(table[:, :1], (heads, rest)))
    return jnp.concatenate(parts, axis=1)[:, :width].astype(F32)


def _toeplitz_base_t(table, shift, width):
    heads, size = table.shape
    n_low = shift - REL_CLIP
    assert n_low >= 0
    parts = [jnp.broadcast_to(table[:, :1], (heads, n_low)), table]
    rest = width - n_low - size
    if rest > 0:
        parts.append(jnp.broadcast_to(table[:, size - 1:], (heads, rest)))
    return jnp.concatenate(parts, axis=1)[:, :width].astype(F32)


def _prep_weights(g_mix, w_in, conv_w, conv_b, ssd_A_log, ssd_dt_bias, ssd_D, ssd_g_out, att_g_q, att_g_k,
                  w_out, g_mem_x, g_mem_src, w_mq, w_mk, w_mv, g_mq, g_mk, w_mo, g_ffn, w_ff1, w_ff2):
    o_dt = SSD_WIDTH + SSD_CONV_DIM
    w_zx = w_in[:, :o_dt].astype(BF16)
    w_dt = jnp.pad(w_in[:, o_dt:o_dt + SSD_HEADS], ((0, 0), (0, DT_PAD - SSD_HEADS))).astype(BF16)
    w_qkv = w_in[:, o_dt + SSD_HEADS:].astype(BF16)
    pad_h = lambda v: jnp.pad(v, (0, DT_PAD - SSD_HEADS)).reshape(1, DT_PAD)
    row = lambda v: v.reshape(1, -1)
    expand = (jnp.arange(LANES)[:, None] == jnp.arange(SSD_WIDTH)[None, :] // SSD_HEAD_DIM).astype(BF16)
    return dict(
        g_mix=row(g_mix), w_zx=w_zx, w_dt=w_dt, w_qkv=w_qkv,
        w_vt=w_in[:, IN_WIDTH - ATT_WIDTH:].T.astype(BF16),
        gq_t=row(jnp.tile(att_g_q, ATT_HEADS)), gk_t=row(jnp.tile(att_g_k, ATT_HEADS)),
        conv_w=conv_w, conv_b=row(conv_b), dtb=pad_h(ssd_dt_bias), alog=pad_h(ssd_A_log),
        dskip=row(jnp.repeat(ssd_D, SSD_HEAD_DIM)), gout=row(ssd_g_out), expand=expand,
        w_out=w_out.astype(BF16), g_mem_x=row(g_mem_x), g_mem_src=row(g_mem_src),
        w_mq=w_mq.astype(BF16), w_mk=w_mk.astype(BF16), w_mv=w_mv.astype(BF16),
        g_mq=row(g_mq), g_mk=row(g_mk), w_mo=w_mo.astype(BF16), g_ffn=row(g_ffn),
        w_ff1=w_ff1.astype(BF16), w_ff2=w_ff2.astype(BF16))


def _layer(x, conv_prev, h0, k_cache, v_cache, mem_k, mem_v, p, rel, *, tr, t_scan):
    b, length, _ = x.shape
    rows = b * length
    x2d = x.reshape(rows, D_MODEL)
    prompt = k_cache is None
    tail_rows = PAST if prompt else rows
    z, xbc, dtraw, q, k, v, k_tail, v_tail = _in_proj(
        x2d, p["g_mix"], p["w_zx"], p["w_dt"], p["w_qkv"], p["w_vt"], p["gq_t"], p["gk_t"], tr=tr,
        tail_rows=tail_rows,
        v_feature_major=prompt)

    cprev8 = jnp.pad(conv_prev, ((0, 0), (8 - (SSD_CONV - 1), 0), (0, 0)))
    lb = t_scan if prompt else length
    y_ssd, h_fin = _ssd_mixer(
        xbc.reshape(b, length, SSD_CONV_DIM), dtraw.reshape(b, length, DT_PAD),
        z.reshape(b, length, SSD_WIDTH), cprev8, h0.reshape(b, SSD_WIDTH, SSD_STATE),
        p["conv_w"], p["conv_b"], p["dtb"], p["alog"], p["dskip"], p["gout"], p["expand"],
        t=t_scan, lb=lb)
    conv_new = xbc.reshape(b, length, SSD_CONV_DIM)[:, length - (SSD_CONV - 1):]

    if prompt:
        y_att = _band_prompt(q, k, v, _toeplitz_base_t(rel, ATT_SB, ATT_SB + ATT_WIN))
        k_rows = k_tail.reshape(b, PAST, ATT_HEADS, ATT_HEAD_DIM)
        v_rows = v_tail.reshape(b, PAST, ATT_HEADS, ATT_HEAD_DIM)
    else:
        lc = k_cache.shape[1]
        y_att = _band_sample(q, k, v, k_cache.reshape(b, lc, ATT_WIDTH).astype(BF16),
                             v_cache.reshape(b, lc, ATT_WIDTH).astype(BF16),
                             _toeplitz_base(rel, lc + LANES, lc + 2 * LANES), lq=length)
        k_rows = k_tail.reshape(b, length, ATT_HEADS, ATT_HEAD_DIM)
        v_rows = v_tail.reshape(b, length, ATT_HEADS, ATT_HEAD_DIM)

    h = _outproj_mem(x2d, y_ssd.reshape(rows, SSD_WIDTH), y_att, p["w_out"], p["g_mem_x"], p["w_mq"],
                     p["g_mq"], mem_k.astype(BF16), mem_v.astype(BF16), p["w_mo"], tr=tr, seq_len=length)
    y = _ffn(h, p["g_ffn"], p["w_ff1"], p["w_ff2"], tr=tr)
    return (y.reshape(b, length, D_MODEL), h_fin.reshape(b, SSD_HEADS, SSD_HEAD_DIM, SSD_STATE),
            conv_new, k_rows, v_rows)


def kernel(x_prompt, x_sample, mem_prompt, state_ssd, state_conv, cache_attn_k, cache_attn_v, cache_mem_k,
           cache_mem_v, g_mix, w_in, conv_w, conv_b, ssd_A_log, ssd_dt_bias, ssd_D, ssd_g_out, att_g_q,
           att_g_k, att_rel_bias, w_out, g_mem_x, g_mem_src, w_mq, w_mk, w_mv, g_mq, g_mk, w_mo, g_ffn,
           w_ff1, w_ff2):
    depth = g_mix.shape[0]
    b_p, seq, _ = x_prompt.shape
    b_s, dec_seq, _ = x_sample.shape
    yp, ys = x_prompt, x_sample
    outs = [[] for _ in range(10)]
    for l in range(depth):
        p = _prep_weights(g_mix[l], w_in[l], conv_w[l], conv_b[l], ssd_A_log[l], ssd_dt_bias[l], ssd_D[l],
                          ssd_g_out[l], att_g_q[l], att_g_k[l], w_out[l], g_mem_x[l], g_mem_src[l],
                          w_mq[l], w_mk[l], w_mv[l], g_mq[l], g_mk[l], w_mo[l], g_ffn[l], w_ff1[l], w_ff2[l])
        rel = att_rel_bias[l]
        mk, mv = _memory_kv(mem_prompt.reshape(b_p * N_MEM, D_MODEL), p["g_mem_src"], p["w_mk"], p["w_mv"],
                            p["g_mk"])
        mk = mk.reshape(b_p, N_MEM, D_MODEL)
        mv = mv.reshape(b_p, N_MEM, D_MODEL)
        conv0 = jnp.zeros((b_p, SSD_CONV - 1, SSD_CONV_DIM), F32)
        h00 = jnp.zeros((b_p, SSD_HEADS, SSD_HEAD_DIM, SSD_STATE), F32)
        yp, hp, cp, kp, vp = _layer(yp, conv0, h00, None, None, mk, mv, p, rel,
                                    tr=min(256, seq), t_scan=min(256, seq))
        ys, hs, cs, ks_, vs_ = _layer(ys, state_conv[l], state_ssd[l], cache_attn_k[l], cache_attn_v[l],
                                      cache_mem_k[l].reshape(b_s, N_MEM, D_MODEL),
                                      cache_mem_v[l].reshape(b_s, N_MEM, D_MODEL), p, rel,
                                      tr=b_s * dec_seq, t_scan=128)
        for lst, val in zip(outs, (hp, cp, kp, vp,
                                   mk.reshape(b_p, N_MEM, MEM_HEADS, MEM_HEAD_DIM),
                                   mv.reshape(b_p, N_MEM, MEM_HEADS, MEM_HEAD_DIM),
                                   hs, cs, ks_, vs_)):
            lst.append(val)
    return (yp, ys) + tuple(jnp.stack(o) for o in outs)
```

```python
import functools

import jax
import jax.numpy as jnp
from jax import lax
from jax.experimental import pallas as pl
from jax.experimental.pallas import tpu as pltpu

F32 = jnp.float32
BF16 = jnp.bfloat16

D_MODEL = 1024
CHUNK = 64
SSD_HEADS = 16
SSD_HEAD_DIM = 64
SSD_WIDTH = SSD_HEADS * SSD_HEAD_DIM
SSD_GROUPS = 2
SSD_STATE = 128
SSD_CONV = 4
SSD_CONV_DIM = SSD_WIDTH + 2 * SSD_GROUPS * SSD_STATE
ATT_HEADS = 16
ATT_HEAD_DIM = 64
ATT_WIDTH = ATT_HEADS * ATT_HEAD_DIM
N_LEFT_CHUNKS = 8
PAST = N_LEFT_CHUNKS * CHUNK
REL_CLIP = 128
N_MEM = 256
MEM_HEADS = 4
MEM_HEAD_DIM = D_MODEL // MEM_HEADS
D_FF = 4 * D_MODEL
EPS = 1e-6

LANES = 128
DT_PAD = LANES
IN_WIDTH = SSD_WIDTH + SSD_CONV_DIM + SSD_HEADS + 3 * ATT_WIDTH
IN_PAD_WIDTH = SSD_WIDTH + SSD_CONV_DIM + DT_PAD + 3 * ATT_WIDTH
NEG = -1e30
LOG2E = 1.4426950408889634
VMEM_LIMIT = 56 * 1024 * 1024


def _rms(x, g):
    return x * lax.rsqrt(jnp.mean(x * x, axis=-1, keepdims=True) + EPS) * g


def _silu(x):
    h = 0.5 * x
    return h + h * jnp.tanh(h)


def _split2(x):
    hi = x.astype(BF16)
    lo = (x - hi.astype(F32)).astype(BF16)
    return hi, lo


def _split3(x):
    hi = x.astype(BF16)
    r = x - hi.astype(F32)
    mid = r.astype(BF16)
    lo = (r - mid.astype(F32)).astype(BF16)
    return hi, mid, lo


def _dot(a, b):
    return jnp.dot(a, b, preferred_element_type=F32)


def _dot_nt(a, b):
    return lax.dot_general(a, b, (((1,), (1,)), ((), ())), preferred_element_type=F32)


def _inproj_kernel(x_ref, gmix_ref, wzx_ref, wdt_ref, wqkv_ref, wvt_ref, gq_ref, gk_ref, cprev_ref,
                   convw_ref, convb_ref,
                   gz_ref, xc_ref, xtail_ref, dt_ref, q_ref, k_ref, v_ref, kt_ref, vt_ref, ext_ref,
                   *, n_tail, v_feature_major, nb, tiles_per_stream):
    i = pl.program_id(0)
    n = pl.num_programs(0)
    tr = x_ref.shape[0]
    rb = tr // nb
    halo = 8
    xn = _rms(x_ref[...], gmix_ref[...]).astype(BF16)

    def proj(loc, width):
        w_ref, lo = loc
        return _dot(xn, w_ref[:, lo:lo + width])

    o_xbc = (wzx_ref, SSD_WIDTH)
    o_dt = (wdt_ref, 0)
    o_q = (wqkv_ref, 0)
    o_k = (wqkv_ref, ATT_WIDTH)
    o_v = (wqkv_ref, 2 * ATT_WIDTH)
    lane = lax.broadcasted_iota(jnp.int32, (tr, LANES), 1)
    first = lane < ATT_HEAD_DIM

    def head_norm(y, g_ref, c, scale):
        blk = y[:, c * LANES:(c + 1) * LANES]
        sq = blk * blk
        s0 = jnp.sum(jnp.where(first, sq, 0.0), axis=-1, keepdims=True)
        s1 = jnp.sum(jnp.where(first, 0.0, sq), axis=-1, keepdims=True)
        r = jnp.where(first, lax.rsqrt(s0 * (1.0 / ATT_HEAD_DIM) + EPS),
                      lax.rsqrt(s1 * (1.0 / ATT_HEAD_DIM) + EPS))
        return blk * r * (g_ref[:, c * LANES:(c + 1) * LANES] * scale)

    xbc = proj(o_xbc, SSD_CONV_DIM)
    z = proj((wzx_ref, 0), SSD_WIDTH)

    for s in range(nb):
        base = s * (rb + halo)
        if tiles_per_stream == 1:
            carried = cprev_ref[s]
        else:
            carried = jnp.where(i % tiles_per_stream == 0, cprev_ref[s], ext_ref[rb:rb + halo, :])
        ext_ref[base:base + halo, :] = carried
        ext_ref[base + halo:base + halo + rb, :] = xbc[s * rb:(s + 1) * rb, :]
        xtail_ref[s] = xbc[(s + 1) * rb - halo:(s + 1) * rb, :]
        for cb in range(SSD_CONV_DIM // LANES):
            sl = slice(cb * LANES, (cb + 1) * LANES)
            xe = ext_ref[base:base + halo + rb, sl]
            acc = convb_ref[:, sl] + convw_ref[SSD_CONV - 1:SSD_CONV, sl] * xe[halo:, :]
            for j in range(1, SSD_CONV):
                tap = pltpu.roll(xe, j, 0)[halo:, :]
                acc = acc + convw_ref[SSD_CONV - 1 - j:SSD_CONV - j, sl] * tap
            xc_ref[s * rb:(s + 1) * rb, sl] = _silu(acc)
    gz_ref[...] = _silu(z)

    q = proj(o_q, ATT_WIDTH)
    k = proj(o_k, ATT_WIDTH)
    for c in range(ATT_WIDTH // LANES):
        q_ref[:, c * LANES:(c + 1) * LANES] = head_norm(
            q, gq_ref, c, ATT_HEAD_DIM ** -0.5 * LOG2E).astype(BF16)
    if v_feature_major:
        v_ref[...] = _dot_nt(wvt_ref[...], xn).astype(BF16)
    else:
        v = proj(o_v, ATT_WIDTH)
        v_ref[...] = v.astype(BF16)
        vt_ref[...] = v
    for c in range(ATT_WIDTH // LANES):
        kn = head_norm(k, gk_ref, c, 1.0)
        k_ref[:, c * LANES:(c + 1) * LANES] = kn.astype(BF16)
        kt_ref[:, c * LANES:(c + 1) * LANES] = kn
    dt_ref[...] = proj(o_dt, DT_PAD)
    if v_feature_major:
        @pl.when(i >= n - n_tail)
        def _():
            vt_ref[...] = proj(o_v, ATT_WIDTH)


def _in_proj(x2d, g_mix, w_zx, w_dt, w_qkv, w_vt, gq_t, gk_t, cprev8, conv_w, conv_b, *, tr, seq_len,
             tail_rows, v_feature_major):
    rows = x2d.shape[0]
    n = rows // tr
    n_tail = tail_rows // tr
    nb = max(tr // seq_len, 1)
    tiles_per_stream = max(seq_len // tr, 1)
    n_streams = rows // seq_len
    assert n * tr == rows and n_tail * tr == tail_rows and cprev8.shape == (n_streams, 8, SSD_CONV_DIM)
    assert (tr // nb) % 8 == 0

    def row(width):
        return pl.BlockSpec((tr, width), lambda i: (i, 0))

    def const(shape):
        return pl.BlockSpec(shape, lambda i: (0,) * len(shape))

    per_stream = pl.BlockSpec((nb, 8, SSD_CONV_DIM), lambda i: (i // tiles_per_stream, 0, 0))
    tail = pl.BlockSpec((tr, ATT_WIDTH), lambda i: (jnp.maximum(i - (n - n_tail), 0), 0))
    if v_feature_major:
        v_spec = pl.BlockSpec((ATT_WIDTH, tr), lambda i: (0, i))
        v_shape = jax.ShapeDtypeStruct((ATT_WIDTH, rows), BF16)
    else:
        v_spec = row(ATT_WIDTH)
        v_shape = jax.ShapeDtypeStruct((rows, ATT_WIDTH), BF16)
    return pl.pallas_call(
        functools.partial(_inproj_kernel, n_tail=n_tail, v_feature_major=v_feature_major, nb=nb,
                          tiles_per_stream=tiles_per_stream),
        grid=(n,),
        in_specs=[row(D_MODEL), const((1, D_MODEL)), const((D_MODEL, SSD_WIDTH + SSD_CONV_DIM)),
                  const((D_MODEL, DT_PAD)), const((D_MODEL, 3 * ATT_WIDTH)),
                  const((ATT_WIDTH, D_MODEL)), const((1, ATT_WIDTH)), const((1, ATT_WIDTH)),
                  per_stream, const((SSD_CONV, SSD_CONV_DIM)), const((1, SSD_CONV_DIM))],
        out_specs=[row(SSD_WIDTH), row(SSD_CONV_DIM), per_stream, row(DT_PAD), row(ATT_WIDTH),
                   row(ATT_WIDTH), v_spec, tail, tail],
        out_shape=[jax.ShapeDtypeStruct((rows, SSD_WIDTH), F32),
                   jax.ShapeDtypeStruct((rows, SSD_CONV_DIM), F32),
                   jax.ShapeDtypeStruct((n_streams, 8, SSD_CONV_DIM), F32),
                   jax.ShapeDtypeStruct((rows, DT_PAD), F32),
                   jax.ShapeDtypeStruct((rows, ATT_WIDTH), BF16),
                   jax.ShapeDtypeStruct((rows, ATT_WIDTH), BF16),
                   v_shape,
                   jax.ShapeDtypeStruct((tail_rows, ATT_WIDTH), F32),
                   jax.ShapeDtypeStruct((tail_rows, ATT_WIDTH), F32)],
        scratch_shapes=[pltpu.VMEM((nb * (tr // nb + 8), SSD_CONV_DIM), F32)],
        compiler_params=pltpu.CompilerParams(dimension_semantics=("arbitrary",),
                                             vmem_limit_bytes=VMEM_LIMIT),
        name="in_proj",
    )(x2d, g_mix, w_zx, w_dt, w_qkv, w_vt, gq_t, gk_t, cprev8, conv_w, conv_b)


def _ssd_kernel(xc_ref, dtraw_ref, gz_ref, h0_ref, dtb_ref, alog_ref, dskip_ref, gout_ref, expand_ref,
                y_ref, hfin_ref,
                ht_ref, ybuf_ref, *, t, lb):
    c = pl.program_id(1)
    nc = pl.num_programs(1)

    @pl.when(c == 0)
    def _():
        ht_ref[...] = h0_ref[0].T

    def rows_of(ref, sl):
        v = ref[0, :, sl]
        if lb < t:
            v = jnp.concatenate([v, jnp.zeros((t - lb, v.shape[1]), v.dtype)], axis=0)
        return v

    lane = lax.broadcasted_iota(jnp.int32, (t, LANES), 1)
    rowi = lax.broadcasted_iota(jnp.int32, (t, LANES), 0)
    dt = jax.nn.softplus(rows_of(dtraw_ref, slice(None)) + dtb_ref[...])
    dt = jnp.where((lane < SSD_HEADS) & (rowi < lb), dt, 0.0)
    a_neg = -jnp.exp(alog_ref[...]) * LOG2E
    a = dt * a_neg

    rr = lax.broadcasted_iota(jnp.int32, (t, t), 0)
    cc = lax.broadcasted_iota(jnp.int32, (t, t), 1)
    causal = rr >= cc
    tril = jnp.where(causal, 1.0, 0.0).astype(BF16)
    a1, a2, a3 = _split3(a)
    a_cum = _dot(tril, a1) + _dot(tril, a2) + _dot(tril, a3)
    a_last = a_cum[t - 1:t, :]
    ea = jnp.exp2(a_cum)
    cd = jnp.exp2(a_last)
    a_t = a_cum.T
    w_t = jnp.exp2(a_t[:, t - 1:t] - a_t)

    stacked = jnp.concatenate([ea, dt, jnp.broadcast_to(cd, (8, LANES))], axis=0)
    s_hi, s_lo = _split2(stacked)
    expanded = _dot(s_hi, expand_ref[...]) + _dot(s_lo, expand_ref[...])
    ea_x = expanded[0:t, :]
    dt_x = expanded[t:2 * t, :]
    cd_x = expanded[2 * t:2 * t + 1, :]

    first = lane < SSD_HEAD_DIM
    heads_per_group = SSD_HEADS // SSD_GROUPS
    o_b = SSD_WIDTH
    o_c = SSD_WIDTH + SSD_GROUPS * SSD_STATE
    ssq = jnp.zeros((lb, 1), F32)
    for g in range(SSD_GROUPS):
        bg = rows_of(xc_ref, slice(o_b + g * SSD_STATE, o_b + (g + 1) * SSD_STATE))
        cg = rows_of(xc_ref, slice(o_c + g * SSD_STATE, o_c + (g + 1) * SSD_STATE)).astype(BF16)
        cb_mat = jnp.where(causal, _dot_nt(cg, bg.astype(BF16)), 0.0)
        bg_t = bg.T
        for jp in range(heads_per_group // 2):
            j = g * (heads_per_group // 2) + jp
            sl = slice(j * LANES, (j + 1) * LANES)
            xh = rows_of(xc_ref, sl)
            xdt = xh * dt_x[:, sl]
            y_pair = None
            s_pair = None
            for hh in range(2):
                h = 2 * j + hh
                xm = jnp.where(first if hh == 0 else jnp.logical_not(first), xdt, 0.0).astype(BF16)
                seg = a_cum[:, h:h + 1] - a_t[h:h + 1, :]
                m = (cb_mat * jnp.exp2(jnp.minimum(seg, 0.0))).astype(BF16)
                yd = _dot(m, xm)
                bw = (bg_t * w_t[h:h + 1, :]).astype(BF16)
                sd = _dot(bw, xm)
                y_pair = yd if y_pair is None else y_pair + yd
                s_pair = sd if s_pair is None else s_pair + sd
            h_in = ht_ref[:, sl]
            y_off = _dot(cg, h_in.astype(BF16)) * ea_x[:, sl]
            ht_ref[:, sl] = cd_x[:, sl] * h_in + s_pair
            y = y_pair + y_off + dskip_ref[:, sl] * xh
            yg = y[0:lb, :] * gz_ref[0, :, sl]
            ybuf_ref[:, sl] = yg
            ssq = ssq + jnp.sum(yg * yg, axis=-1, keepdims=True)

    r = lax.rsqrt(ssq * (1.0 / SSD_WIDTH) + EPS)
    y_ref[0] = (ybuf_ref[...] * r * gout_ref[...]).astype(BF16)

    @pl.when(c == nc - 1)
    def _():
        hfin_ref[0] = ht_ref[...].T


def _ssd_mixer(xc, dtraw, gz, h0, dtb, alog, dskip, gout, expand, *, t, lb):
    b, length, _ = xc.shape
    nc = length // lb
    assert nc * lb == length and (lb == t or nc == 1)

    def seq(width):
        return pl.BlockSpec((1, lb, width), lambda bi, ci: (bi, ci, 0))

    def per_b(shape):
        return pl.BlockSpec((1,) + shape, lambda bi, ci: (bi, 0, 0))

    def const(shape):
        return pl.BlockSpec(shape, lambda bi, ci: (0,) * len(shape))

    hp = SSD_WIDTH
    return pl.pallas_call(
        functools.partial(_ssd_kernel, t=t, lb=lb),
        grid=(b, nc),
        in_specs=[seq(SSD_CONV_DIM), seq(DT_PAD), seq(SSD_WIDTH), per_b((hp, SSD_STATE)),
                  const((1, DT_PAD)), const((1, DT_PAD)), const((1, SSD_WIDTH)), const((1, SSD_WIDTH)),
                  const((LANES, SSD_WIDTH))],
        out_specs=[seq(SSD_WIDTH), per_b((hp, SSD_STATE))],
        out_shape=[jax.ShapeDtypeStruct((b, length, SSD_WIDTH), BF16),
                   jax.ShapeDtypeStruct((b, hp, SSD_STATE), F32)],
        scratch_shapes=[pltpu.VMEM((SSD_STATE, hp), F32),
                        pltpu.VMEM((lb, SSD_WIDTH), F32)],
        compiler_params=pltpu.CompilerParams(dimension_semantics=("arbitrary", "arbitrary"),
                                             vmem_limit_bytes=VMEM_LIMIT),
        name="ssd_mixer",
    )(xc, dtraw, gz, h0, dtb, alog, dskip, gout, expand)


ATT_SB = 2 * CHUNK
ATT_WIN = ATT_SB + PAST


def _toeplitz_rows(base_row, rows, offset, width):
    ext = base_row.shape[1]
    rolled = pltpu.roll(jnp.broadcast_to(base_row, (rows, ext)), 0, 1, stride=1, stride_axis=0)
    return rolled[:, offset:offset + width]


def _band_subblock(q, kw, vt, bias_t, first):
    return _band_softmax_pv(_band_scores(q, kw, bias_t, first), vt)


def _band_scores(q, kw, bias_t, first):
    zero = jnp.zeros_like(q)
    q2 = jnp.concatenate([jnp.where(first, q, zero), jnp.where(first, zero, q)], axis=0)
    return _dot_nt(kw, q2) + bias_t


def _band_softmax(s):
    m = jnp.max(s, axis=0, keepdims=True)
    p = jnp.exp2(s - m)
    return p.astype(BF16), 1.0 / jnp.sum(p, axis=0, keepdims=True)


def _band_pv(p, inv_l, vt):
    sb = p.shape[1] // 2
    half = ATT_HEAD_DIM
    o_t = _dot(vt, p)
    o_pair_t = jnp.concatenate([o_t[0:half, 0:sb] * inv_l[:, 0:sb],
                                o_t[half:2 * half, sb:2 * sb] * inv_l[:, sb:2 * sb]], axis=0)
    return o_pair_t.T.astype(BF16)


def _band_softmax_pv(s, vt):
    p, inv_l = _band_softmax(s)
    return _band_pv(p, inv_l, vt)


def _band_prompt_kernel(q_ref, k_ref, vt_ref, base_ref, o_ref, bias_ref, sc_ref, pb_ref, il_ref):
    length = q_ref.shape[0]
    sb, w = ATT_SB, ATT_WIN
    kj = lax.broadcasted_iota(jnp.int32, (w, sb), 0) // CHUNK
    qi = lax.broadcasted_iota(jnp.int32, (w, sb), 1) // CHUNK
    in_band = (kj >= qi) & (kj <= qi + N_LEFT_CHUNKS)
    for hh in range(2):
        toe = _toeplitz_rows(base_ref[0, hh:hh + 1, :], w, w, sb)
        bias_ref[:, hh * sb:(hh + 1) * sb] = jnp.where(in_band, toe * LOG2E, NEG)

    lane = lax.broadcasted_iota(jnp.int32, (sb, LANES), 1)
    first = lane < ATT_HEAD_DIM
    n_sub = length // sb
    n_head = min(PAST // sb, n_sub)
    for j in range(n_head):
        n = (j + 1) * sb
        o_ref[j * sb:(j + 1) * sb, :] = _band_subblock(
            q_ref[j * sb:(j + 1) * sb, :], k_ref[0:n, :], vt_ref[:, 0:n], bias_ref[w - n:, :], first)

    n_main = n_sub - n_head
    if n_main == 0:
        return
    assert n_main % 2 == 0

    def scores_into(slot, j):
        r0 = pl.multiple_of(j * sb, sb)
        k0 = pl.multiple_of(j * sb - PAST, sb)
        sc_ref[slot] = _band_scores(q_ref[pl.ds(r0, sb), :], k_ref[pl.ds(k0, w), :], bias_ref[...], first)

    def softmax_into(slot):
        p, inv_l = _band_softmax(sc_ref[slot])
        pb_ref[slot] = p
        il_ref[slot] = jnp.broadcast_to(inv_l, (8, 2 * sb))

    def values_out(slot, j):
        k0 = pl.multiple_of(j * sb - PAST, sb)
        return _band_pv(pb_ref[slot], il_ref[slot, 0:1, :], vt_ref[:, pl.ds(k0, w)])

    last = n_sub - 1
    scores_into(0, n_head)
    scores_into(1, n_head + 1)
    softmax_into(0)

    per_trip = 4 if n_main % 4 == 0 else 2

    def body(i, carry):
        outs = []
        for u in range(per_trip):
            j = n_head + per_trip * i + u
            slot = u % 2
            scores_into(slot, jnp.minimum(j + 2, last))
            softmax_into(1 - slot)
            outs.append((j, values_out(slot, j)))
        for j, o in outs:
            o_ref[pl.ds(pl.multiple_of(j * sb, sb), sb), :] = o
        return carry

    lax.fori_loop(0, n_main // per_trip, body, 0)


def _band_prompt(q, k, vt, base):
    length = q.shape[0]
    assert length % ATT_SB == 0
    ext = ATT_SB + ATT_WIN
    col = pl.BlockSpec((length, LANES), lambda c: (0, c))
    return pl.pallas_call(
        _band_prompt_kernel,
        grid=(ATT_WIDTH // LANES,),
        in_specs=[col, col, pl.BlockSpec((LANES, length), lambda c: (c, 0)),
                  pl.BlockSpec((1, 2, ext), lambda c: (c, 0, 0))],
        out_specs=col,
        out_shape=jax.ShapeDtypeStruct((length, ATT_WIDTH), BF16),
        scratch_shapes=[pltpu.VMEM((ATT_WIN, 2 * ATT_SB), F32),
                        pltpu.VMEM((2, ATT_WIN, 2 * ATT_SB), F32),
                        pltpu.VMEM((2, ATT_WIN, 2 * ATT_SB), BF16),
                        pltpu.VMEM((2, 8, 2 * ATT_SB), F32)],
        compiler_params=pltpu.CompilerParams(dimension_semantics=("arbitrary",),
                                             vmem_limit_bytes=VMEM_LIMIT),
        name="band_prompt",
    )(q, k, vt, base.reshape(ATT_HEADS // 2, 2, ext))


def _band_sample_kernel(q_ref, kn_ref, vn_ref, kc_ref, vc_ref, base_ref, o_ref, bias_ref):
    lq = q_ref.shape[0]
    lc = kc_ref.shape[1]

    @pl.when(pl.program_id(0) == 0)
    def _():
        for h in range(ATT_HEADS):
            bias_ref[h] = _toeplitz_rows(base_ref[h:h + 1, :], lq, LANES, lc + LANES) * LOG2E

    lane = lax.broadcasted_iota(jnp.int32, (lq, LANES), 1)
    first = lane < ATT_HEAD_DIM
    for j in range(ATT_WIDTH // LANES):
        sl = slice(j * LANES, (j + 1) * LANES)
        q = q_ref[:, sl]
        kc = kc_ref[0, :, sl]
        vc = vc_ref[0, :, sl]
        kn = kn_ref[:, sl]
        vn = vn_ref[:, sl]
        outs = []
        for hh in range(2):
            h = 2 * j + hh
            qm = jnp.where(first if hh == 0 else jnp.logical_not(first), q, jnp.zeros_like(q))
            sc = _dot_nt(qm, kc) + bias_ref[h, :, 0:lc]
            sn = _dot_nt(qm, kn) + bias_ref[h, :, lc:lc + lq]
            m = jnp.maximum(jnp.max(sc, axis=-1, keepdims=True), jnp.max(sn, axis=-1, keepdims=True))
            pc = jnp.exp2(sc - m)
            pn = jnp.exp2(sn - m)
            l = jnp.sum(pc, axis=-1, keepdims=True) + jnp.sum(pn, axis=-1, keepdims=True)
            o = _dot(pc.astype(BF16), vc) + _dot(pn.astype(BF16), vn)
            outs.append(o * (1.0 / l))
        o_ref[:, sl] = jnp.where(first, outs[0], outs[1]).astype(BF16)


def _band_sample(q, kn, vn, kc, vc, base, *, lq):
    rows = q.shape[0]
    nb = rows // lq
    lc = kc.shape[1]
    assert lq <= LANES and lc % LANES == 0 and base.shape == (ATT_HEADS, lc + 2 * LANES)
    blk = pl.BlockSpec((lq, ATT_WIDTH), lambda b: (b, 0))
    cache = pl.BlockSpec((1, lc, ATT_WIDTH), lambda b: (b, 0, 0))
    return pl.pallas_call(
        _band_sample_kernel,
        grid=(nb,),
        in_specs=[blk, blk, blk, cache, cache, pl.BlockSpec(base.shape, lambda b: (0, 0))],
        out_specs=blk,
        out_shape=jax.ShapeDtypeStruct((rows, ATT_WIDTH), BF16),
        scratch_shapes=[pltpu.VMEM((ATT_HEADS, lq, lc + LANES), F32)],
        compiler_params=pltpu.CompilerParams(dimension_semantics=("arbitrary",),
                                             vmem_limit_bytes=VMEM_LIMIT),
        name="band_sample",
    )(q, kn, vn, kc, vc, base)


def _mem_head_norm(y, g_ref, hd, scale):
    blk = y[:, hd * MEM_HEAD_DIM:(hd + 1) * MEM_HEAD_DIM]
    r = lax.rsqrt(jnp.mean(blk * blk, axis=-1, keepdims=True) + EPS)
    return blk * r * (g_ref[...] * scale)


def _memkv_kernel(mem_ref, gsrc_ref, wk_ref, wv_ref, gk_ref, k_ref, v_ref):
    m = _rms(mem_ref[...], gsrc_ref[...]).astype(BF16)
    k = _dot(m, wk_ref[...])
    for hd in range(MEM_HEADS):
        k_ref[:, hd * MEM_HEAD_DIM:(hd + 1) * MEM_HEAD_DIM] = _mem_head_norm(k, gk_ref, hd, 1.0)
    v_ref[...] = _dot(m, wv_ref[...])


def _memory_kv(mem2d, g_src, w_mk, w_mv, g_mk):
    rows = mem2d.shape[0]

    def full(shape):
        return pl.BlockSpec(shape, lambda i: (0,) * len(shape))

    return pl.pallas_call(
        _memkv_kernel,
        grid=(1,),
        in_specs=[full((rows, D_MODEL)), full((1, D_MODEL)), full((D_MODEL, D_MODEL)),
                  full((D_MODEL, D_MODEL)), full((1, MEM_HEAD_DIM))],
        out_specs=[full((rows, D_MODEL)), full((rows, D_MODEL))],
        out_shape=[jax.ShapeDtypeStruct((rows, D_MODEL), F32)] * 2,
        compiler_params=pltpu.CompilerParams(dimension_semantics=("arbitrary",),
                                             vmem_limit_bytes=VMEM_LIMIT),
        name="memory_kv",
    )(mem2d, g_src, w_mk, w_mv, g_mk)


def _outproj_mem_kernel(x_ref, ys_ref, ya_ref, wo_ref, gmx_ref, wq_ref, gmq_ref, mk_ref, mv_ref,
                        wmo_ref, h_ref, obuf_ref, *, nb, rb):
    h = x_ref[...] + _dot(ys_ref[...], wo_ref[0:SSD_WIDTH, :]) + _dot(ya_ref[...], wo_ref[SSD_WIDTH:, :])
    hn = _rms(h, gmx_ref[...]).astype(BF16)
    q = _dot(hn, wq_ref[...])
    for hd in range(MEM_HEADS):
        sl = slice(hd * MEM_HEAD_DIM, (hd + 1) * MEM_HEAD_DIM)
        qn = _mem_head_norm(q, gmq_ref, hd, MEM_HEAD_DIM ** -0.5).astype(BF16)
        for b in range(nb):
            rows = slice(b * rb, (b + 1) * rb)
            s = _dot_nt(qn[rows, :], mk_ref[b, :, sl])
            m = jnp.max(s, axis=-1, keepdims=True)
            p = jnp.exp(s - m)
            l = jnp.sum(p, axis=-1, keepdims=True)
            o = _dot(p.astype(BF16), mv_ref[b, :, sl]) * (1.0 / l)
            obuf_ref[rows, sl] = o.astype(BF16)
    h_ref[...] = h + _dot(obuf_ref[...], wmo_ref[...])


def _outproj_mem(x2d, ys, ya, w_out, g_mem_x, w_mq, g_mq, mk, mv, w_mo, *, tr, seq_len):
    rows = x2d.shape[0]
    n = rows // tr
    nb = max(tr // seq_len, 1)
    tiles_per_stream = max(seq_len // tr, 1)
    rb = tr // nb
    assert n * tr == rows and nb * rb == tr and mk.shape[0] * seq_len == rows

    def row(width):
        return pl.BlockSpec((tr, width), lambda i: (i, 0))

    def const(shape):
        return pl.BlockSpec(shape, lambda i: (0,) * len(shape))

    mem = pl.BlockSpec((nb, N_MEM, D_MODEL), lambda i: (i // tiles_per_stream, 0, 0))
    return pl.pallas_call(
        functools.partial(_outproj_mem_kernel, nb=nb, rb=rb),
        grid=(n,),
        in_specs=[row(D_MODEL), row(SSD_WIDTH), row(ATT_WIDTH), const((SSD_WIDTH + ATT_WIDTH, D_MODEL)),
                  const((1, D_MODEL)), const((D_MODEL, D_MODEL)), const((1, MEM_HEAD_DIM)), mem, mem,
                  const((D_MODEL, D_MODEL))],
        out_specs=row(D_MODEL),
        out_shape=jax.ShapeDtypeStruct((rows, D_MODEL), F32),
        scratch_shapes=[pltpu.VMEM((tr, D_MODEL), BF16)],
        compiler_params=pltpu.CompilerParams(dimension_semantics=("arbitrary",),
                                             vmem_limit_bytes=VMEM_LIMIT),
        name="outproj_mem",
    )(x2d, ys, ya, w_out, g_mem_x, w_mq, g_mq, mk, mv, w_mo)


FF_SLAB = 1024


def _ffn_kernel(h_ref, g_ref, w1_ref, w2_ref, y_ref):
    h = h_ref[...]
    hn = _rms(h, g_ref[...]).astype(BF16)
    acc = h
    for s in range(D_FF // FF_SLAB):
        u = jnp.maximum(_dot(hn, w1_ref[:, s * FF_SLAB:(s + 1) * FF_SLAB]), 0.0)
        acc = acc + _dot((u * u).astype(BF16), w2_ref[s * FF_SLAB:(s + 1) * FF_SLAB, :])
    y_ref[...] = acc


def _ffn(h2d, g_ffn, w1, w2, *, tr):
    rows = h2d.shape[0]
    n = rows // tr
    assert n * tr == rows

    def const(shape):
        return pl.BlockSpec(shape, lambda i: (0,) * len(shape))

    row = pl.BlockSpec((tr, D_MODEL), lambda i: (i, 0))
    return pl.pallas_call(
        _ffn_kernel,
        grid=(n,),
        in_specs=[row, const((1, D_MODEL)), const((D_MODEL, D_FF)), const((D_FF, D_MODEL))],
        out_specs=row,
        out_shape=jax.ShapeDtypeStruct((rows, D_MODEL), F32),
        compiler_params=pltpu.CompilerParams(dimension_semantics=("arbitrary",),
                                             vmem_limit_bytes=VMEM_LIMIT),
        name="ffn",
    )(h2d, g_ffn, w1, w2)


def _toeplitz_base(table, offset, width):
    heads, size = table.shape
    n_far = offset - REL_CLIP
    assert n_far >= 0
    parts = [jnp.broadcast_to(table[:, size - 1:], (heads, n_far)), table[:, ::-1]]
    rest = width - n_far - size
    if rest > 0:
        parts.append(jnp.broadcast_to(table[:, :1], (heads, rest)))
    return jnp.concatenate(parts, axis=1)[:, :width].astype(F32)


def _toeplitz_base_t(table, shift, width):
    heads, size = table.shape
    n_low = shift - REL_CLIP
    assert n_low >= 0
    parts = [jnp.broadcast_to(table[:, :1], (heads, n_low)), table]
    rest = width - n_low - size
    if rest > 0:
        parts.append(jnp.broadcast_to(table[:, size - 1:], (heads, rest)))
    return jnp.concatenate(parts, axis=1)[:, :width].astype(F32)


def _prep_weights(g_mix, w_in, conv_w, conv_b, ssd_A_log, ssd_dt_bias, ssd_D, ssd_g_out, att_g_q, att_g_k,
                  w_out, g_mem_x, g_mem_src, w_mq, w_mk, w_mv, g_mq, g_mk, w_mo, g_ffn, w_ff1, w_ff2):
    o_dt = SSD_WIDTH + SSD_CONV_DIM
    w_zx = w_in[:, :o_dt].astype(BF16)
    w_dt = jnp.pad(w_in[:, o_dt:o_dt + SSD_HEADS], ((0, 0), (0, DT_PAD - SSD_HEADS))).astype(BF16)
    w_qkv = w_in[:, o_dt + SSD_HEADS:].astype(BF16)
    pad_h = lambda v: jnp.pad(v, (0, DT_PAD - SSD_HEADS)).reshape(1, DT_PAD)
    row = lambda v: v.reshape(1, -1)
    expand = (jnp.arange(LANES)[:, None] == jnp.arange(SSD_WIDTH)[None, :] // SSD_HEAD_DIM).astype(BF16)
    return dict(
        g_mix=row(g_mix), w_zx=w_zx, w_dt=w_dt, w_qkv=w_qkv,
        w_vt=w_in[:, IN_WIDTH - ATT_WIDTH:].T.astype(BF16),
        gq_t=row(jnp.tile(att_g_q, ATT_HEADS)), gk_t=row(jnp.tile(att_g_k, ATT_HEADS)),
        conv_w=conv_w, conv_b=row(conv_b), dtb=pad_h(ssd_dt_bias), alog=pad_h(ssd_A_log),
        dskip=row(jnp.repeat(ssd_D, SSD_HEAD_DIM)), gout=row(ssd_g_out), expand=expand,
        w_out=w_out.astype(BF16), g_mem_x=row(g_mem_x), g_mem_src=row(g_mem_src),
        w_mq=w_mq.astype(BF16), w_mk=w_mk.astype(BF16), w_mv=w_mv.astype(BF16),
        g_mq=row(g_mq), g_mk=row(g_mk), w_mo=w_mo.astype(BF16), g_ffn=row(g_ffn),
        w_ff1=w_ff1.astype(BF16), w_ff2=w_ff2.astype(BF16))


def _layer(x, conv_prev, h0, k_cache, v_cache, mem_k, mem_v, p, rel, *, tr, t_scan):
    b, length, _ = x.shape
    rows = b * length
    x2d = x.reshape(rows, D_MODEL)
    prompt = k_cache is None
    tail_rows = PAST if prompt else rows
    cprev8 = jnp.pad(conv_prev, ((0, 0), (8 - (SSD_CONV - 1), 0), (0, 0)))
    gz, xc, xtail, dtraw, q, k, v, k_tail, v_tail = _in_proj(
        x2d, p["g_mix"], p["w_zx"], p["w_dt"], p["w_qkv"], p["w_vt"], p["gq_t"], p["gk_t"], cprev8,
        p["conv_w"], p["conv_b"], tr=tr, seq_len=length, tail_rows=tail_rows, v_feature_major=prompt)
    conv_new = xtail[:, 8 - (SSD_CONV - 1):]

    lb = t_scan if prompt else length
    y_ssd, h_fin = _ssd_mixer(
        xc.reshape(b, length, SSD_CONV_DIM), dtraw.reshape(b, length, DT_PAD),
        gz.reshape(b, length, SSD_WIDTH), h0.reshape(b, SSD_WIDTH, SSD_STATE),
        p["dtb"], p["alog"], p["dskip"], p["gout"], p["expand"], t=t_scan, lb=lb)

    if prompt:
        y_att = _band_prompt(q, k, v, _toeplitz_base_t(rel, ATT_SB, ATT_SB + ATT_WIN))
        k_rows = k_tail.reshape(b, PAST, ATT_HEADS, ATT_HEAD_DIM)
        v_rows = v_tail.reshape(b, PAST, ATT_HEADS, ATT_HEAD_DIM)
    else:
        lc = k_cache.shape[1]
        y_att = _band_sample(q, k, v, k_cache.reshape(b, lc, ATT_WIDTH).astype(BF16),
                             v_cache.reshape(b, lc, ATT_WIDTH).astype(BF16),
                             _toeplitz_base(rel, lc + LANES, lc + 2 * LANES), lq=length)
        k_rows = k_tail.reshape(b, length, ATT_HEADS, ATT_HEAD_DIM)
        v_rows = v_tail.reshape(b, length, ATT_HEADS, ATT_HEAD_DIM)

    h = _outproj_mem(x2d, y_ssd.reshape(rows, SSD_WIDTH), y_att, p["w_out"], p["g_mem_x"], p["w_mq"],
                     p["g_mq"], mem_k.astype(BF16), mem_v.astype(BF16), p["w_mo"], tr=tr, seq_len=length)
    y = _ffn(h, p["g_ffn"], p["w_ff1"], p["w_ff2"], tr=tr)
    return (y.reshape(b, length, D_MODEL), h_fin.reshape(b, SSD_HEADS, SSD_HEAD_DIM, SSD_STATE),
            conv_new, k_rows, v_rows)


def kernel(x_prompt, x_sample, mem_prompt, state_ssd, state_conv, cache_attn_k, cache_attn_v, cache_mem_k,
           cache_mem_v, g_mix, w_in, conv_w, conv_b, ssd_A_log, ssd_dt_bias, ssd_D, ssd_g_out, att_g_q,
           att_g_k, att_rel_bias, w_out, g_mem_x, g_mem_src, w_mq, w_mk, w_mv, g_mq, g_mk, w_mo, g_ffn,
           w_ff1, w_ff2):
    depth = g_mix.shape[0]
    b_p, seq, _ = x_prompt.shape
    b_s, dec_seq, _ = x_sample.shape
    yp, ys = x_prompt, x_sample
    outs = [[] for _ in range(10)]
    for l in range(depth):
        p = _prep_weights(g_mix[l], w_in[l], conv_w[l], conv_b[l], ssd_A_log[l], ssd_dt_bias[l], ssd_D[l],
                          ssd_g_out[l], att_g_q[l], att_g_k[l], w_out[l], g_mem_x[l], g_mem_src[l],
                          w_mq[l], w_mk[l], w_mv[l], g_mq[l], g_mk[l], w_mo[l], g_ffn[l], w_ff1[l], w_ff2[l])
        rel = att_rel_bias[l]
        mk, mv = _memory_kv(mem_prompt.reshape(b_p * N_MEM, D_MODEL), p["g_mem_src"], p["w_mk"], p["w_mv"],
                            p["g_mk"])
        mk = mk.reshape(b_p, N_MEM, D_MODEL)
        mv = mv.reshape(b_p, N_MEM, D_MODEL)
        conv0 = jnp.zeros((b_p, SSD_CONV - 1, SSD_CONV_DIM), F32)
        h00 = jnp.zeros((b_p, SSD_HEADS, SSD_HEAD_DIM, SSD_STATE), F32)
        yp, hp, cp, kp, vp = _layer(yp, conv0, h00, None, None, mk, mv, p, rel,
                                    tr=min(256, seq), t_scan=min(256, seq))
        ys, hs, cs, ks_, vs_ = _layer(ys, state_conv[l], state_ssd[l], cache_attn_k[l], cache_attn_v[l],
                                      cache_mem_k[l].reshape(b_s, N_MEM, D_MODEL),
                                      cache_mem_v[l].reshape(b_s, N_MEM, D_MODEL), p, rel,
                                      tr=b_s * dec_seq, t_scan=128)
        for lst, val in zip(outs, (hp, cp, kp, vp,
                                   mk.reshape(b_p, N_MEM, MEM_HEADS, MEM_HEAD_DIM),
                                   mv.reshape(b_p, N_MEM, MEM_HEADS, MEM_HEAD_DIM),
                                   hs, cs, ks_, vs_)):
            lst.append(val)
    return (yp, ys) + tuple(jnp.stack(o) for o in outs)
```

```python
import functools

import jax
import jax.numpy as jnp
from jax import lax
from jax.experimental import pallas as pl
from jax.experimental.pallas import tpu as pltpu

F32 = jnp.float32
BF16 = jnp.bfloat16

D_MODEL = 1024
CHUNK = 64
SSD_HEADS = 16
SSD_HEAD_DIM = 64
SSD_WIDTH = SSD_HEADS * SSD_HEAD_DIM
SSD_GROUPS = 2
SSD_STATE = 128
SSD_CONV = 4
SSD_CONV_DIM = SSD_WIDTH + 2 * SSD_GROUPS * SSD_STATE
ATT_HEADS = 16
ATT_HEAD_DIM = 64
ATT_WIDTH = ATT_HEADS * ATT_HEAD_DIM
N_LEFT_CHUNKS = 8
PAST = N_LEFT_CHUNKS * CHUNK
REL_CLIP = 128
N_MEM = 256
MEM_HEADS = 4
MEM_HEAD_DIM = D_MODEL // MEM_HEADS
D_FF = 4 * D_MODEL
EPS = 1e-6

LANES = 128
DT_PAD = LANES
NEG = -1e30
LOG2E = 1.4426950408889634
CONV_ROWS = 64
VMEM_LIMIT = 56 * 1024 * 1024


def _rms(x, g):
    return x * lax.rsqrt(jnp.mean(x * x, axis=-1, keepdims=True) + EPS) * g


def _silu(x):
    h = 0.5 * x
    return h + h * jnp.tanh(h)


def _split2(x):
    hi = x.astype(BF16)
    lo = (x - hi.astype(F32)).astype(BF16)
    return hi, lo


def _split3(x):
    hi = x.astype(BF16)
    r = x - hi.astype(F32)
    mid = r.astype(BF16)
    lo = (r - mid.astype(F32)).astype(BF16)
    return hi, mid, lo


def _dot(a, b):
    return jnp.dot(a, b, preferred_element_type=F32)


def _dot_nt(a, b):
    return lax.dot_general(a, b, (((1,), (1,)), ((), ())), preferred_element_type=F32)


def _inproj_kernel(x_ref, gmix_ref, wzx_ref, wdt_ref, wqkv_ref, gq_ref, gk_ref, cprev_ref,
                   convw_ref, convb_ref,
                   gz_ref, xc_ref, xtail_ref, dt_ref, q_ref, k_ref, v_ref, kt_ref, vt_ref, ext_ref, wvt_ref,
                   *, n_tail, v_feature_major, nb, tiles_per_stream):
    i = pl.program_id(0)
    n = pl.num_programs(0)
    tr = x_ref.shape[0]
    rb = tr // nb
    halo = 8
    if v_feature_major:
        @pl.when(i == 0)
        def _():
            wvt_ref[...] = wqkv_ref[:, 2 * ATT_WIDTH:].T
    xn = _rms(x_ref[...], gmix_ref[...]).astype(BF16)

    def proj(loc, width):
        w_ref, lo = loc
        return _dot(xn, w_ref[:, lo:lo + width])

    o_xbc = (wzx_ref, SSD_WIDTH)
    o_dt = (wdt_ref, 0)
    o_q = (wqkv_ref, 0)
    o_k = (wqkv_ref, ATT_WIDTH)
    o_v = (wqkv_ref, 2 * ATT_WIDTH)
    lane = lax.broadcasted_iota(jnp.int32, (tr, LANES), 1)
    first = lane < ATT_HEAD_DIM

    def head_norm(blk, g_ref, c, scale):
        sq = blk * blk
        s0 = jnp.sum(jnp.where(first, sq, 0.0), axis=-1, keepdims=True)
        s1 = jnp.sum(jnp.where(first, 0.0, sq), axis=-1, keepdims=True)
        r = jnp.where(first, lax.rsqrt(s0 * (1.0 / ATT_HEAD_DIM) + EPS),
                      lax.rsqrt(s1 * (1.0 / ATT_HEAD_DIM) + EPS))
        return blk * r * (g_ref[:, c * LANES:(c + 1) * LANES] * scale)

    chunk = 2 * LANES
    work = []

    def run(count):
        for _ in range(min(count, len(work))):
            work.pop(0)()

    xbc = proj(o_xbc, SSD_CONV_DIM)
    for s in range(nb):
        base = s * (rb + halo)
        if tiles_per_stream == 1:
            carried = cprev_ref[s]
        else:
            carried = jnp.where(i % tiles_per_stream == 0, cprev_ref[s], ext_ref[rb:rb + halo, :])
        ext_ref[base:base + halo, :] = carried
        ext_ref[base + halo:base + halo + rb, :] = xbc[s * rb:(s + 1) * rb, :]
        xtail_ref[s] = xbc[(s + 1) * rb - halo:(s + 1) * rb, :]

        def conv_piece(s=s, base=base, cb=0, r0=0, nr=rb):
            sl = slice(cb * LANES, (cb + 1) * LANES)
            xe = ext_ref[base + r0:base + r0 + halo + nr, sl]
            acc = convb_ref[:, sl] + convw_ref[SSD_CONV - 1:SSD_CONV, sl] * xe[halo:, :]
            for j in range(1, SSD_CONV):
                tap = pltpu.roll(xe, j, 0)[halo:, :]
                acc = acc + convw_ref[SSD_CONV - 1 - j:SSD_CONV - j, sl] * tap
            xc_ref[s * rb + r0:s * rb + r0 + nr, sl] = _silu(acc)

        nr = min(rb, CONV_ROWS)
        for cb in range(SSD_CONV_DIM // LANES):
            for r0 in range(0, rb, nr):
                work.append(functools.partial(conv_piece, cb=cb, r0=r0, nr=nr))

    def gate(zc, c):
        gz_ref[:, c * LANES:(c + 1) * LANES] = _silu(zc)

    def norm_q(blk, c):
        q_ref[:, c * LANES:(c + 1) * LANES] = head_norm(
            blk, gq_ref, c, ATT_HEAD_DIM ** -0.5 * LOG2E).astype(BF16)

    def norm_k(blk, c):
        kn = head_norm(blk, gk_ref, c, 1.0)
        k_ref[:, c * LANES:(c + 1) * LANES] = kn.astype(BF16)
        kt_ref[:, c * LANES:(c + 1) * LANES] = kn

    def chunks(loc, width, consumer, per_chunk):
        w_ref, lo = loc
        for c0 in range(0, width, chunk):
            y = proj((w_ref, lo + c0), chunk)
            for t in range(chunk // LANES):
                work.append(functools.partial(consumer, y[:, t * LANES:(t + 1) * LANES], c0 // LANES + t))
            run(per_chunk)

    per_chunk = -(-len(work) // 12) + 2
    chunks((wzx_ref, 0), SSD_WIDTH, gate, per_chunk)
    chunks(o_q, ATT_WIDTH, norm_q, per_chunk)
    chunks(o_k, ATT_WIDTH, norm_k, per_chunk)
    for c0 in range(0, ATT_WIDTH, chunk):
        if v_feature_major:
            v_ref[c0:c0 + chunk, :] = _dot_nt(wvt_ref[c0:c0 + chunk, :], xn).astype(BF16)
        else:
            v = proj((wqkv_ref, 2 * ATT_WIDTH + c0), chunk)
            v_ref[:, c0:c0 + chunk] = v.astype(BF16)
            vt_ref[:, c0:c0 + chunk] = v
        run(3)
    run(len(work))
    dt_ref[...] = proj(o_dt, DT_PAD)
    if v_feature_major:
        @pl.when(i >= n - n_tail)
        def _():
            vt_ref[...] = proj(o_v, ATT_WIDTH)


def _in_proj(x2d, g_mix, w_zx, w_dt, w_qkv, gq_t, gk_t, cprev8, conv_w, conv_b, *, tr, seq_len,
             tail_rows, v_feature_major):
    rows = x2d.shape[0]
    n = rows // tr
    n_tail = tail_rows // tr
    nb = max(tr // seq_len, 1)
    tiles_per_stream = max(seq_len // tr, 1)
    n_streams = rows // seq_len
    assert n * tr == rows and n_tail * tr == tail_rows and cprev8.shape == (n_streams, 8, SSD_CONV_DIM)
    assert (tr // nb) % 8 == 0

    def row(width):
        return pl.BlockSpec((tr, width), lambda i: (i, 0))

    def const(shape):
        return pl.BlockSpec(shape, lambda i: (0,) * len(shape))

    per_stream = pl.BlockSpec((nb, 8, SSD_CONV_DIM), lambda i: (i // tiles_per_stream, 0, 0))
    tail = pl.BlockSpec((tr, ATT_WIDTH), lambda i: (jnp.maximum(i - (n - n_tail), 0), 0))
    if v_feature_major:
        v_spec = pl.BlockSpec((ATT_WIDTH, tr), lambda i: (0, i))
        v_shape = jax.ShapeDtypeStruct((ATT_WIDTH, rows), BF16)
    else:
        v_spec = row(ATT_WIDTH)
        v_shape = jax.ShapeDtypeStruct((rows, ATT_WIDTH), BF16)
    return pl.pallas_call(
        functools.partial(_inproj_kernel, n_tail=n_tail, v_feature_major=v_feature_major, nb=nb,
                          tiles_per_stream=tiles_per_stream),
        grid=(n,),
        in_specs=[row(D_MODEL), const((1, D_MODEL)), const((D_MODEL, SSD_WIDTH + SSD_CONV_DIM)),
                  const((D_MODEL, DT_PAD)), const((D_MODEL, 3 * ATT_WIDTH)),
                  const((1, ATT_WIDTH)), const((1, ATT_WIDTH)),
                  per_stream, const((SSD_CONV, SSD_CONV_DIM)), const((1, SSD_CONV_DIM))],
        out_specs=[row(SSD_WIDTH), row(SSD_CONV_DIM), per_stream, row(DT_PAD), row(ATT_WIDTH),
                   row(ATT_WIDTH), v_spec, tail, tail],
        out_shape=[jax.ShapeDtypeStruct((rows, SSD_WIDTH), F32),
                   jax.ShapeDtypeStruct((rows, SSD_CONV_DIM), F32),
                   jax.ShapeDtypeStruct((n_streams, 8, SSD_CONV_DIM), F32),
                   jax.ShapeDtypeStruct((rows, DT_PAD), F32),
                   jax.ShapeDtypeStruct((rows, ATT_WIDTH), BF16),
                   jax.ShapeDtypeStruct((rows, ATT_WIDTH), BF16),
                   v_shape,
                   jax.ShapeDtypeStruct((tail_rows, ATT_WIDTH), F32),
                   jax.ShapeDtypeStruct((tail_rows, ATT_WIDTH), F32)],
        scratch_shapes=[pltpu.VMEM((nb * (tr // nb + 8), SSD_CONV_DIM), F32),
                        pltpu.VMEM((ATT_WIDTH, D_MODEL) if v_feature_major else (16, LANES), BF16)],
        compiler_params=pltpu.CompilerParams(dimension_semantics=("arbitrary",),
                                             vmem_limit_bytes=VMEM_LIMIT),
        name="in_proj",
    )(x2d, g_mix, w_zx, w_dt, w_qkv, gq_t, gk_t, cprev8, conv_w, conv_b)


def _ssd_pairs(xc_ref, dtraw_ref, gz_ref, dtb_ref, alog_ref, dskip_ref, expand_ref, ht_ref, ybuf_ref,
               *, t, lb):
    def rows_of(ref, sl):
        v = ref[0, :, sl]
        if lb < t:
            v = jnp.concatenate([v, jnp.zeros((t - lb, v.shape[1]), v.dtype)], axis=0)
        return v

    lane = lax.broadcasted_iota(jnp.int32, (t, LANES), 1)
    rowi = lax.broadcasted_iota(jnp.int32, (t, LANES), 0)
    dt = jax.nn.softplus(rows_of(dtraw_ref, slice(None)) + dtb_ref[...])
    dt = jnp.where((lane < SSD_HEADS) & (rowi < lb), dt, 0.0)
    a_neg = -jnp.exp(alog_ref[...]) * LOG2E
    a = dt * a_neg

    rr = lax.broadcasted_iota(jnp.int32, (t, t), 0)
    cc = lax.broadcasted_iota(jnp.int32, (t, t), 1)
    causal = rr >= cc
    tril = jnp.where(causal, 1.0, 0.0).astype(BF16)
    a1, a2, a3 = _split3(a)
    a_cum = _dot(tril, a1) + _dot(tril, a2) + _dot(tril, a3)
    a_last = a_cum[t - 1:t, :]
    ea = jnp.exp2(a_cum)
    cd = jnp.exp2(a_last)
    a_t = a_cum.T
    w_t = jnp.exp2(a_t[:, t - 1:t] - a_t)

    stacked = jnp.concatenate([ea, dt, jnp.broadcast_to(cd, (8, LANES))], axis=0)
    s_hi, s_lo = _split2(stacked)
    expanded = _dot(s_hi, expand_ref[...]) + _dot(s_lo, expand_ref[...])
    ea_x = expanded[0:t, :]
    dt_x = expanded[t:2 * t, :]
    cd_x = expanded[2 * t:2 * t + 1, :]

    first = lane < SSD_HEAD_DIM
    heads_per_group = SSD_HEADS // SSD_GROUPS
    o_b = SSD_WIDTH
    o_c = SSD_WIDTH + SSD_GROUPS * SSD_STATE
    ssq = [jnp.zeros((lb, 1), F32)]
    per_group = {}

    def group_values(g):
        if g not in per_group:
            bg = rows_of(xc_ref, slice(o_b + g * SSD_STATE, o_b + (g + 1) * SSD_STATE))
            cg = rows_of(xc_ref, slice(o_c + g * SSD_STATE, o_c + (g + 1) * SSD_STATE)).astype(BF16)
            cb_mat = jnp.where(causal, _dot_nt(cg, bg.astype(BF16)), 0.0)
            per_group[g] = (cg, cb_mat, bg.T)
        return per_group[g]

    def pair(j):
        cg, cb_mat, bg_t = group_values(j // (heads_per_group // 2))
        sl = slice(j * LANES, (j + 1) * LANES)
        xh = rows_of(xc_ref, sl)
        xdt = xh * dt_x[:, sl]
        y_pair = None
        s_pair = None
        for hh in range(2):
            h = 2 * j + hh
            xm = jnp.where(first if hh == 0 else jnp.logical_not(first), xdt, 0.0).astype(BF16)
            seg = a_cum[:, h:h + 1] - a_t[h:h + 1, :]
            m = (cb_mat * jnp.exp2(jnp.minimum(seg, 0.0))).astype(BF16)
            yd = _dot(m, xm)
            bw = (bg_t * w_t[h:h + 1, :]).astype(BF16)
            sd = _dot(bw, xm)
            y_pair = yd if y_pair is None else y_pair + yd
            s_pair = sd if s_pair is None else s_pair + sd
        h_in = ht_ref[:, sl]
        y_off = _dot(cg, h_in.astype(BF16)) * ea_x[:, sl]
        ht_ref[:, sl] = cd_x[:, sl] * h_in + s_pair
        y = y_pair + y_off + dskip_ref[:, sl] * xh
        yg = y[0:lb, :] * gz_ref[0, :, sl]
        ybuf_ref[:, sl] = yg
        ssq[0] = ssq[0] + jnp.sum(yg * yg, axis=-1, keepdims=True)

    return [functools.partial(pair, j) for j in range(SSD_WIDTH // LANES)], ssq


def _ssd_kernel(xc_ref, dtraw_ref, gz_ref, h0_ref, dtb_ref, alog_ref, dskip_ref, gout_ref, expand_ref,
                y_ref, hfin_ref,
                ht_ref, ybuf_ref, *, t, lb):
    c = pl.program_id(1)
    nc = pl.num_programs(1)

    @pl.when(c == 0)
    def _():
        ht_ref[...] = h0_ref[0].T

    pairs, ssq = _ssd_pairs(xc_ref, dtraw_ref, gz_ref, dtb_ref, alog_ref, dskip_ref, expand_ref, ht_ref,
                            ybuf_ref, t=t, lb=lb)
    for do_pair in pairs:
        do_pair()
    r = lax.rsqrt(ssq[0] * (1.0 / SSD_WIDTH) + EPS)
    y_ref[0] = (ybuf_ref[...] * r * gout_ref[...]).astype(BF16)

    @pl.when(c == nc - 1)
    def _():
        hfin_ref[0] = ht_ref[...].T


def _ssd_mixer(xc, dtraw, gz, h0, dtb, alog, dskip, gout, expand, *, t, lb):
    b, length, _ = xc.shape
    nc = length // lb
    assert nc * lb == length and (lb == t or nc == 1)

    def seq(width):
        return pl.BlockSpec((1, lb, width), lambda bi, ci: (bi, ci, 0))

    def per_b(shape):
        return pl.BlockSpec((1,) + shape, lambda bi, ci: (bi, 0, 0))

    def const(shape):
        return pl.BlockSpec(shape, lambda bi, ci: (0,) * len(shape))

    hp = SSD_WIDTH
    return pl.pallas_call(
        functools.partial(_ssd_kernel, t=t, lb=lb),
        grid=(b, nc),
        in_specs=[seq(SSD_CONV_DIM), seq(DT_PAD), seq(SSD_WIDTH), per_b((hp, SSD_STATE)),
                  const((1, DT_PAD)), const((1, DT_PAD)), const((1, SSD_WIDTH)), const((1, SSD_WIDTH)),
                  const((LANES, SSD_WIDTH))],
        out_specs=[seq(SSD_WIDTH), per_b((hp, SSD_STATE))],
        out_shape=[jax.ShapeDtypeStruct((b, length, SSD_WIDTH), BF16),
                   jax.ShapeDtypeStruct((b, hp, SSD_STATE), F32)],
        scratch_shapes=[pltpu.VMEM((SSD_STATE, hp), F32),
                        pltpu.VMEM((lb, SSD_WIDTH), F32)],
        compiler_params=pltpu.CompilerParams(dimension_semantics=("arbitrary", "arbitrary"),
                                             vmem_limit_bytes=VMEM_LIMIT),
        name="ssd_mixer",
    )(xc, dtraw, gz, h0, dtb, alog, dskip, gout, expand)


ATT_SB = 2 * CHUNK
ATT_WIN = ATT_SB + PAST


def _toeplitz_rows(base_row, rows, offset, width):
    ext = base_row.shape[1]
    rolled = pltpu.roll(jnp.broadcast_to(base_row, (rows, ext)), 0, 1, stride=1, stride_axis=0)
    return rolled[:, offset:offset + width]


def _band_subblock(q, kw, vt, bias_t, first):
    return _band_softmax_pv(_band_scores(q, kw, bias_t, first), vt)


def _band_scores(q, kw, bias_t, first):
    zero = jnp.zeros_like(q)
    q2 = jnp.concatenate([jnp.where(first, q, zero), jnp.where(first, zero, q)], axis=0)
    return _dot_nt(kw, q2) + bias_t


def _band_softmax(s):
    m = jnp.max(s, axis=0, keepdims=True)
    p = jnp.exp2(s - m)
    return p.astype(BF16), 1.0 / jnp.sum(p, axis=0, keepdims=True)


def _band_pv(p, inv_l, vt):
    sb = p.shape[1] // 2
    half = ATT_HEAD_DIM
    o_t = _dot(vt, p)
    o_pair_t = jnp.concatenate([o_t[0:half, 0:sb] * inv_l[:, 0:sb],
                                o_t[half:2 * half, sb:2 * sb] * inv_l[:, sb:2 * sb]], axis=0)
    return o_pair_t.T.astype(BF16)


def _band_softmax_pv(s, vt):
    p, inv_l = _band_softmax(s)
    return _band_pv(p, inv_l, vt)


def _band_prompt_kernel(q_ref, k_ref, vt_ref, base_ref, o_ref, bias_ref, sc_ref, pb_ref, il_ref):
    length = q_ref.shape[0]
    sb, w = ATT_SB, ATT_WIN
    kj = lax.broadcasted_iota(jnp.int32, (w, sb), 0) // CHUNK
    qi = lax.broadcasted_iota(jnp.int32, (w, sb), 1) // CHUNK
    in_band = (kj >= qi) & (kj <= qi + N_LEFT_CHUNKS)
    for hh in range(2):
        toe = _toeplitz_rows(base_ref[0, hh:hh + 1, :], w, w, sb)
        bias_ref[:, hh * sb:(hh + 1) * sb] = jnp.where(in_band, toe * LOG2E, NEG)

    lane = lax.broadcasted_iota(jnp.int32, (sb, LANES), 1)
    first = lane < ATT_HEAD_DIM
    n_sub = length // sb
    n_head = min(PAST // sb, n_sub)
    for j in range(n_head):
        n = (j + 1) * sb
        o_ref[j * sb:(j + 1) * sb, :] = _band_subblock(
            q_ref[j * sb:(j + 1) * sb, :], k_ref[0:n, :], vt_ref[:, 0:n], bias_ref[w - n:, :], first)

    n_main = n_sub - n_head
    if n_main == 0:
        return
    assert n_main % 2 == 0

    def scores_into(slot, j):
        r0 = pl.multiple_of(j * sb, sb)
        k0 = pl.multiple_of(j * sb - PAST, sb)
        sc_ref[slot] = _band_scores(q_ref[pl.ds(r0, sb), :], k_ref[pl.ds(k0, w), :], bias_ref[...], first)

    def softmax_into(slot):
        p, inv_l = _band_softmax(sc_ref[slot])
        pb_ref[slot] = p
        il_ref[slot] = jnp.broadcast_to(inv_l, (8, 2 * sb))

    def values_out(slot, j):
        k0 = pl.multiple_of(j * sb - PAST, sb)
        return _band_pv(pb_ref[slot], il_ref[slot, 0:1, :], vt_ref[:, pl.ds(k0, w)])

    last = n_sub - 1
    scores_into(0, n_head)
    scores_into(1, n_head + 1)
    softmax_into(0)

    per_trip = 4 if n_main % 4 == 0 else 2

    def body(i, carry):
        outs = []
        for u in range(per_trip):
            j = n_head + per_trip * i + u
            slot = u % 2
            scores_into(slot, jnp.minimum(j + 2, last))
            softmax_into(1 - slot)
            outs.append((j, values_out(slot, j)))
        for j, o in outs:
            o_ref[pl.ds(pl.multiple_of(j * sb, sb), sb), :] = o
        return carry

    lax.fori_loop(0, n_main // per_trip, body, 0)


def _band_prompt(q, k, vt, base):
    length = q.shape[0]
    assert length % ATT_SB == 0
    ext = ATT_SB + ATT_WIN
    col = pl.BlockSpec((length, LANES), lambda c: (0, c))
    return pl.pallas_call(
        _band_prompt_kernel,
        grid=(ATT_WIDTH // LANES,),
        in_specs=[col, col, pl.BlockSpec((LANES, length), lambda c: (c, 0)),
                  pl.BlockSpec((1, 2, ext), lambda c: (c, 0, 0))],
        out_specs=col,
        out_shape=jax.ShapeDtypeStruct((length, ATT_WIDTH), BF16),
        scratch_shapes=[pltpu.VMEM((ATT_WIN, 2 * ATT_SB), F32),
                        pltpu.VMEM((2, ATT_WIN, 2 * ATT_SB), F32),
                        pltpu.VMEM((2, ATT_WIN, 2 * ATT_SB), BF16),
                        pltpu.VMEM((2, 8, 2 * ATT_SB), F32)],
        compiler_params=pltpu.CompilerParams(dimension_semantics=("arbitrary",),
                                             vmem_limit_bytes=VMEM_LIMIT),
        name="band_prompt",
    )(q, k, vt, base.reshape(ATT_HEADS // 2, 2, ext))


def _band_sample_kernel(q_ref, kn_ref, vn_ref, kc_ref, vc_ref, base_ref, o_ref, bias_ref):
    lq = q_ref.shape[0]
    lc = kc_ref.shape[1] // ATT_HEADS

    @pl.when(pl.program_id(0) == 0)
    def _():
        for h in range(ATT_HEADS):
            bias_ref[h] = _toeplitz_rows(base_ref[h:h + 1, :], lq, LANES, lc + LANES) * LOG2E

    for h in range(ATT_HEADS):
        sl = slice(h * ATT_HEAD_DIM, (h + 1) * ATT_HEAD_DIM)
        q = q_ref[:, sl]
        kc = kc_ref[0, pl.ds(h, lc, stride=ATT_HEADS), :].astype(BF16)
        vc = vc_ref[0, pl.ds(h, lc, stride=ATT_HEADS), :].astype(BF16)
        sc = _dot_nt(q, kc) + bias_ref[h, :, 0:lc]
        sn = _dot_nt(q, kn_ref[:, sl]) + bias_ref[h, :, lc:lc + lq]
        m = jnp.maximum(jnp.max(sc, axis=-1, keepdims=True), jnp.max(sn, axis=-1, keepdims=True))
        pc = jnp.exp2(sc - m)
        pn = jnp.exp2(sn - m)
        l = jnp.sum(pc, axis=-1, keepdims=True) + jnp.sum(pn, axis=-1, keepdims=True)
        o = _dot(pc.astype(BF16), vc) + _dot(pn.astype(BF16), vn_ref[:, sl])
        o_ref[:, sl] = (o * (1.0 / l)).astype(BF16)


def _band_sample(q, kn, vn, kc, vc, base, *, lq):
    rows = q.shape[0]
    nb = rows // lq
    lc = kc.shape[1] // ATT_HEADS
    assert lq <= LANES and lc % LANES == 0 and base.shape == (ATT_HEADS, lc + 2 * LANES)
    blk = pl.BlockSpec((lq, ATT_WIDTH), lambda b: (b, 0))
    cache = pl.BlockSpec((1, lc * ATT_HEADS, ATT_HEAD_DIM), lambda b: (b, 0, 0))
    return pl.pallas_call(
        _band_sample_kernel,
        grid=(nb,),
        in_specs=[blk, blk, blk, cache, cache, pl.BlockSpec(base.shape, lambda b: (0, 0))],
        out_specs=blk,
        out_shape=jax.ShapeDtypeStruct((rows, ATT_WIDTH), BF16),
        scratch_shapes=[pltpu.VMEM((ATT_HEADS, lq, lc + LANES), F32)],
        compiler_params=pltpu.CompilerParams(dimension_semantics=("arbitrary",),
                                             vmem_limit_bytes=VMEM_LIMIT),
        name="band_sample",
    )(q, kn, vn, kc, vc, base)


def _mem_head_norm(y, g_ref, hd, scale):
    blk = y[:, hd * MEM_HEAD_DIM:(hd + 1) * MEM_HEAD_DIM]
    r = lax.rsqrt(jnp.mean(blk * blk, axis=-1, keepdims=True) + EPS)
    return blk * r * (g_ref[...] * scale)


def _memkv_kernel(mem_ref, gsrc_ref, wk_ref, wv_ref, gk_ref, k_ref, v_ref):
    m = _rms(mem_ref[...], gsrc_ref[...]).astype(BF16)
    k = _dot(m, wk_ref[...])
    for hd in range(MEM_HEADS):
        k_ref[:, hd * MEM_HEAD_DIM:(hd + 1) * MEM_HEAD_DIM] = _mem_head_norm(k, gk_ref, hd, 1.0)
    v_ref[...] = _dot(m, wv_ref[...])


def _memory_kv(mem2d, g_src, w_mk, w_mv, g_mk):
    rows = mem2d.shape[0]

    def full(shape):
        return pl.BlockSpec(shape, lambda i: (0,) * len(shape))

    return pl.pallas_call(
        _memkv_kernel,
        grid=(1,),
        in_specs=[full((rows, D_MODEL)), full((1, D_MODEL)), full((D_MODEL, D_MODEL)),
                  full((D_MODEL, D_MODEL)), full((1, MEM_HEAD_DIM))],
        out_specs=[full((rows, D_MODEL)), full((rows, D_MODEL))],
        out_shape=[jax.ShapeDtypeStruct((rows, D_MODEL), F32)] * 2,
        compiler_params=pltpu.CompilerParams(dimension_semantics=("arbitrary",),
                                             vmem_limit_bytes=VMEM_LIMIT),
        name="memory_kv",
    )(mem2d, g_src, w_mk, w_mv, g_mk)


def _outproj_mem_kernel(x_ref, ys_ref, ya_ref, wo_ref, gmx_ref, wq_ref, gmq_ref, mk_ref, mv_ref,
                        wmo_ref, h_ref, obuf_ref, *, nb, rb):
    h = x_ref[...] + _dot(ys_ref[...], wo_ref[0:SSD_WIDTH, :]) + _dot(ya_ref[...], wo_ref[SSD_WIDTH:, :])
    hn = _rms(h, gmx_ref[...]).astype(BF16)
    q = _dot(hn, wq_ref[...])
    for hd in range(MEM_HEADS):
        sl = slice(hd * MEM_HEAD_DIM, (hd + 1) * MEM_HEAD_DIM)
        qn = _mem_head_norm(q, gmq_ref, hd, MEM_HEAD_DIM ** -0.5).astype(BF16)
        for b in range(nb):
            rows = slice(b * rb, (b + 1) * rb)
            s = _dot_nt(qn[rows, :], mk_ref[b, :, sl])
            m = jnp.max(s, axis=-1, keepdims=True)
            p = jnp.exp(s - m)
            l = jnp.sum(p, axis=-1, keepdims=True)
            o = _dot(p.astype(BF16), mv_ref[b, :, sl]) * (1.0 / l)
            obuf_ref[rows, sl] = o.astype(BF16)
    h_ref[...] = h + _dot(obuf_ref[...], wmo_ref[...])


def _outproj_mem(x2d, ys, ya, w_out, g_mem_x, w_mq, g_mq, mk, mv, w_mo, *, tr, seq_len):
    rows = x2d.shape[0]
    n = rows // tr
    nb = max(tr // seq_len, 1)
    tiles_per_stream = max(seq_len // tr, 1)
    rb = tr // nb
    assert n * tr == rows and nb * rb == tr and mk.shape[0] * seq_len == rows

    def row(width):
        return pl.BlockSpec((tr, width), lambda i: (i, 0))

    def const(shape):
        return pl.BlockSpec(shape, lambda i: (0,) * len(shape))

    mem = pl.BlockSpec((nb, N_MEM, D_MODEL), lambda i: (i // tiles_per_stream, 0, 0))
    return pl.pallas_call(
        functools.partial(_outproj_mem_kernel, nb=nb, rb=rb),
        grid=(n,),
        in_specs=[row(D_MODEL), row(SSD_WIDTH), row(ATT_WIDTH), const((SSD_WIDTH + ATT_WIDTH, D_MODEL)),
                  const((1, D_MODEL)), const((D_MODEL, D_MODEL)), const((1, MEM_HEAD_DIM)), mem, mem,
                  const((D_MODEL, D_MODEL))],
        out_specs=row(D_MODEL),
        out_shape=jax.ShapeDtypeStruct((rows, D_MODEL), F32),
        scratch_shapes=[pltpu.VMEM((tr, D_MODEL), BF16)],
        compiler_params=pltpu.CompilerParams(dimension_semantics=("arbitrary",),
                                             vmem_limit_bytes=VMEM_LIMIT),
        name="outproj_mem",
    )(x2d, ys, ya, w_out, g_mem_x, w_mq, g_mq, mk, mv, w_mo)


FF_SLAB = 1024


def _ffn_kernel(h_ref, g_ref, w1_ref, w2_ref, y_ref):
    h = h_ref[...]
    hn = _rms(h, g_ref[...]).astype(BF16)
    acc = h
    for s in range(D_FF // FF_SLAB):
        u = jnp.maximum(_dot(hn, w1_ref[:, s * FF_SLAB:(s + 1) * FF_SLAB]), 0.0)
        acc = acc + _dot((u * u).astype(BF16), w2_ref[s * FF_SLAB:(s + 1) * FF_SLAB, :])
    y_ref[...] = acc


def _ffn(h2d, g_ffn, w1, w2, *, tr):
    rows = h2d.shape[0]
    n = rows // tr
    assert n * tr == rows

    def const(shape):
        return pl.BlockSpec(shape, lambda i: (0,) * len(shape))

    row = pl.BlockSpec((tr, D_MODEL), lambda i: (i, 0))
    return pl.pallas_call(
        _ffn_kernel,
        grid=(n,),
        in_specs=[row, const((1, D_MODEL)), const((D_MODEL, D_FF)), const((D_FF, D_MODEL))],
        out_specs=row,
        out_shape=jax.ShapeDtypeStruct((rows, D_MODEL), F32),
        compiler_params=pltpu.CompilerParams(dimension_semantics=("arbitrary",),
                                             vmem_limit_bytes=VMEM_LIMIT),
        name="ffn",
    )(h2d, g_ffn, w1, w2)


def _toeplitz_base(table, offset, width):
    heads, size = table.shape
    n_far = offset - REL_CLIP
    assert n_far >= 0
    parts = [jnp.broadcast_to(table[:, size - 1:], (heads, n_far)), table[:, ::-1]]
    rest = width - n_far - size
    if rest > 0:
        parts.append(jnp.broadcast_to(table[:, :1], (heads, rest)))
    return jnp.concatenate(parts, axis=1)[:, :width].astype(F32)


def _toeplitz_base_t(table, shift, width):
    heads, size = table.shape
    n_low = shift - REL_CLIP
    assert n_low >= 0
    parts = [jnp.broadcast_to(table[:, :1], (heads, n_low)), table]
    rest = width - n_low - size
    if rest > 0:
        parts.append(jnp.broadcast_to(table[:, size - 1:], (heads, rest)))
    return jnp.concatenate(parts, axis=1)[:, :width].astype(F32)


def _prep_weights(g_mix, w_in, conv_w, conv_b, ssd_A_log, ssd_dt_bias, ssd_D, ssd_g_out, att_g_q, att_g_k,
                  w_out, g_mem_x, g_mem_src, w_mq, w_mk, w_mv, g_mq, g_mk, w_mo, g_ffn, w_ff1, w_ff2):
    o_dt = SSD_WIDTH + SSD_CONV_DIM
    w_zx = w_in[:, :o_dt].astype(BF16)
    w_dt = jnp.pad(w_in[:, o_dt:o_dt + SSD_HEADS], ((0, 0), (0, DT_PAD - SSD_HEADS))).astype(BF16)
    w_qkv = w_in[:, o_dt + SSD_HEADS:].astype(BF16)
    pad_h = lambda v: jnp.pad(v, (0, DT_PAD - SSD_HEADS)).reshape(1, DT_PAD)
    row = lambda v: v.reshape(1, -1)
    expand = (jnp.arange(LANES)[:, None] == jnp.arange(SSD_WIDTH)[None, :] // SSD_HEAD_DIM).astype(BF16)
    return dict(
        g_mix=row(g_mix), w_zx=w_zx, w_dt=w_dt, w_qkv=w_qkv,
        gq_t=row(jnp.tile(att_g_q, ATT_HEADS)), gk_t=row(jnp.tile(att_g_k, ATT_HEADS)),
        conv_w=conv_w, conv_b=row(conv_b), dtb=pad_h(ssd_dt_bias), alog=pad_h(ssd_A_log),
        dskip=row(jnp.repeat(ssd_D, SSD_HEAD_DIM)), gout=row(ssd_g_out), expand=expand,
        w_out=w_out.astype(BF16), g_mem_x=row(g_mem_x), g_mem_src=row(g_mem_src),
        w_mq=w_mq.astype(BF16), w_mk=w_mk.astype(BF16), w_mv=w_mv.astype(BF16),
        g_mq=row(g_mq), g_mk=row(g_mk), w_mo=w_mo.astype(BF16), g_ffn=row(g_ffn),
        w_ff1=w_ff1.astype(BF16), w_ff2=w_ff2.astype(BF16))


def _layer(x, conv_prev, h0, k_cache, v_cache, mem_k, mem_v, p, rel, *, tr, t_scan):
    b, length, _ = x.shape
    rows = b * length
    x2d = x.reshape(rows, D_MODEL)
    prompt = k_cache is None
    tail_rows = PAST if prompt else rows
    cprev8 = jnp.pad(conv_prev, ((0, 0), (8 - (SSD_CONV - 1), 0), (0, 0)))
    gz, xc, xtail, dtraw, q, k, v, k_tail, v_tail = _in_proj(
        x2d, p["g_mix"], p["w_zx"], p["w_dt"], p["w_qkv"], p["gq_t"], p["gk_t"], cprev8,
        p["conv_w"], p["conv_b"], tr=tr, seq_len=length, tail_rows=tail_rows, v_feature_major=prompt)
    conv_new = xtail[:, 8 - (SSD_CONV - 1):]

    scan_args = (xc.reshape(b, length, SSD_CONV_DIM), dtraw.reshape(b, length, DT_PAD),
                 gz.reshape(b, length, SSD_WIDTH), h0.reshape(b, SSD_WIDTH, SSD_STATE),
                 p["dtb"], p["alog"], p["dskip"], p["gout"], p["expand"])
    mix_args = (p["w_out"], p["g_mem_x"], p["w_mq"], p["g_mq"], mem_k.astype(BF16), mem_v.astype(BF16),
                p["w_mo"])
    y_ssd, h_fin = _ssd_mixer(*scan_args, t=t_scan, lb=t_scan if prompt else length)
    if prompt:
        y_att = _band_prompt(q, k, v, _toeplitz_base_t(rel, ATT_SB, ATT_SB + ATT_WIN))
        k_rows = k_tail.reshape(b, PAST, ATT_HEADS, ATT_HEAD_DIM)
        v_rows = v_tail.reshape(b, PAST, ATT_HEADS, ATT_HEAD_DIM)
    else:
        lc = k_cache.shape[1]
        y_att = _band_sample(q, k, v, k_cache.reshape(b, lc * ATT_HEADS, ATT_HEAD_DIM),
                             v_cache.reshape(b, lc * ATT_HEADS, ATT_HEAD_DIM),
                             _toeplitz_base(rel, lc + LANES, lc + 2 * LANES), lq=length)
        k_rows = k_tail.reshape(b, length, ATT_HEADS, ATT_HEAD_DIM)
        v_rows = v_tail.reshape(b, length, ATT_HEADS, ATT_HEAD_DIM)
    h = _outproj_mem(x2d, y_ssd.reshape(rows, SSD_WIDTH), y_att, *mix_args, tr=tr, seq_len=length)
    y = _ffn(h, p["g_ffn"], p["w_ff1"], p["w_ff2"], tr=tr)
    return (y.reshape(b, length, D_MODEL), h_fin.reshape(b, SSD_HEADS, SSD_HEAD_DIM, SSD_STATE),
            conv_new, k_rows, v_rows)


def kernel(x_prompt, x_sample, mem_prompt, state_ssd, state_conv, cache_attn_k, cache_attn_v, cache_mem_k,
           cache_mem_v, g_mix, w_in, conv_w, conv_b, ssd_A_log, ssd_dt_bias, ssd_D, ssd_g_out, att_g_q,
           att_g_k, att_rel_bias, w_out, g_mem_x, g_mem_src, w_mq, w_mk, w_mv, g_mq, g_mk, w_mo, g_ffn,
           w_ff1, w_ff2):
    depth = g_mix.shape[0]
    b_p, seq, _ = x_prompt.shape
    b_s, dec_seq, _ = x_sample.shape
    yp, ys = x_prompt, x_sample
    outs = [[] for _ in range(10)]
    for l in range(depth):
        p = _prep_weights(g_mix[l], w_in[l], conv_w[l], conv_b[l], ssd_A_log[l], ssd_dt_bias[l], ssd_D[l],
                          ssd_g_out[l], att_g_q[l], att_g_k[l], w_out[l], g_mem_x[l], g_mem_src[l],
                          w_mq[l], w_mk[l], w_mv[l], g_mq[l], g_mk[l], w_mo[l], g_ffn[l], w_ff1[l], w_ff2[l])
        rel = att_rel_bias[l]
        mk, mv = _memory_kv(mem_prompt.reshape(b_p * N_MEM, D_MODEL), p["g_mem_src"], p["w_mk"], p["w_mv"],
                            p["g_mk"])
        mk = mk.reshape(b_p, N_MEM, D_MODEL)
        mv = mv.reshape(b_p, N_MEM, D_MODEL)
        conv0 = jnp.zeros((b_p, SSD_CONV - 1, SSD_CONV_DIM), F32)
        h00 = jnp.zeros((b_p, SSD_HEADS, SSD_HEAD_DIM, SSD_STATE), F32)
        yp, hp, cp, kp, vp = _layer(yp, conv0, h00, None, None, mk, mv, p, rel,
                                    tr=min(256, seq), t_scan=min(256, seq))
        ys, hs, cs, ks_, vs_ = _layer(ys, state_conv[l], state_ssd[l], cache_attn_k[l], cache_attn_v[l],
                                      cache_mem_k[l].reshape(b_s, N_MEM, D_MODEL),
                                      cache_mem_v[l].reshape(b_s, N_MEM, D_MODEL), p, rel,
                                      tr=b_s * dec_seq, t_scan=128)
        for lst, val in zip(outs, (hp, cp, kp, vp,
                                   mk.reshape(b_p, N_MEM, MEM_HEADS, MEM_HEAD_DIM),
                                   mv.reshape(b_p, N_MEM, MEM_HEADS, MEM_HEAD_DIM),
                                   hs, cs, ks_, vs_)):
            lst.append(val)
    return (yp, ys) + tuple(jnp.stack(o) for o in outs)
```

```python
import functools

import jax
import jax.numpy as jnp
from jax import lax
from jax.experimental import pallas as pl
from jax.experimental.pallas import tpu as pltpu

F32 = jnp.float32
BF16 = jnp.bfloat16

D_MODEL = 1024
CHUNK = 64
SSD_HEADS = 16
SSD_HEAD_DIM = 64
SSD_WIDTH = SSD_HEADS * SSD_HEAD_DIM
SSD_GROUPS = 2
SSD_STATE = 128
SSD_CONV = 4
SSD_CONV_DIM = SSD_WIDTH + 2 * SSD_GROUPS * SSD_STATE
ATT_HEADS = 16
ATT_HEAD_DIM = 64
ATT_WIDTH = ATT_HEADS * ATT_HEAD_DIM
N_LEFT_CHUNKS = 8
PAST = N_LEFT_CHUNKS * CHUNK
REL_CLIP = 128
N_MEM = 256
MEM_HEADS = 4
MEM_HEAD_DIM = D_MODEL // MEM_HEADS
D_FF = 4 * D_MODEL
EPS = 1e-6

LANES = 128
DT_PAD = LANES
NEG = -1e30
LOG2E = 1.4426950408889634
CONV_ROWS = 64
MIX_DENSE_ROWS = 256
VMEM_LIMIT = 56 * 1024 * 1024


def _rms(x, g):
    return x * lax.rsqrt(jnp.mean(x * x, axis=-1, keepdims=True) + EPS) * g


def _silu(x):
    h = 0.5 * x
    return h + h * jnp.tanh(h)


def _split2(x):
    hi = x.astype(BF16)
    lo = (x - hi.astype(F32)).astype(BF16)
    return hi, lo


def _split3(x):
    hi = x.astype(BF16)
    r = x - hi.astype(F32)
    mid = r.astype(BF16)
    lo = (r - mid.astype(F32)).astype(BF16)
    return hi, mid, lo


def _dot(a, b):
    return jnp.dot(a, b, preferred_element_type=F32)


def _dot_nt(a, b):
    return lax.dot_general(a, b, (((1,), (1,)), ((), ())), preferred_element_type=F32)


W_PREP_ROWS = 128


def _split_win_kernel(w_ref, wzx_ref, wdt_ref, wqkv_ref):
    o_dt = SSD_WIDTH + SSD_CONV_DIM
    o_qkv = o_dt + SSD_HEADS
    wzx_ref[...] = w_ref[:, 0:o_dt].astype(BF16)
    lane = lax.broadcasted_iota(jnp.int32, (w_ref.shape[0], DT_PAD), 1)
    wdt_ref[...] = jnp.where(lane < SSD_HEADS, w_ref[:, o_dt:o_dt + DT_PAD], 0.0).astype(BF16)
    wqkv_ref[...] = w_ref[:, o_qkv:o_qkv + 3 * ATT_WIDTH].astype(BF16)


def _split_w_in(w_in):
    d, width = w_in.shape
    n = d // W_PREP_ROWS
    widths = (SSD_WIDTH + SSD_CONV_DIM, DT_PAD, 3 * ATT_WIDTH)
    return pl.pallas_call(
        _split_win_kernel,
        grid=(n,),
        in_specs=[pl.BlockSpec((W_PREP_ROWS, width), lambda i: (i, 0))],
        out_specs=[pl.BlockSpec((W_PREP_ROWS, w), lambda i: (i, 0)) for w in widths],
        out_shape=[jax.ShapeDtypeStruct((d, w), BF16) for w in widths],
        compiler_params=pltpu.CompilerParams(dimension_semantics=("arbitrary",),
                                             vmem_limit_bytes=VMEM_LIMIT),
        name="split_w_in",
    )(w_in)


def _inproj_kernel(x_ref, gmix_ref, wzx_ref, wdt_ref, wqkv_ref, gq_ref, gk_ref, cprev_ref,
                   convw_ref, convb_ref,
                   gz_ref, xc_ref, xtail_ref, dt_ref, q_ref, k_ref, v_ref, kt_ref, vt_ref, ext_ref, wvt_ref,
                   *, n_tail, v_feature_major, nb, tiles_per_stream):
    i = pl.program_id(0)
    n = pl.num_programs(0)
    tr = x_ref.shape[0]
    rb = tr // nb
    halo = 8
    if v_feature_major:
        @pl.when(i == 0)
        def _():
            wvt_ref[...] = wqkv_ref[:, 2 * ATT_WIDTH:].T
    xn = _rms(x_ref[...], gmix_ref[...]).astype(BF16)

    def proj(loc, width):
        w_ref, lo = loc
        return _dot(xn, w_ref[:, lo:lo + width])

    o_xbc = (wzx_ref, SSD_WIDTH)
    o_dt = (wdt_ref, 0)
    o_q = (wqkv_ref, 0)
    o_k = (wqkv_ref, ATT_WIDTH)
    o_v = (wqkv_ref, 2 * ATT_WIDTH)
    lane = lax.broadcasted_iota(jnp.int32, (tr, LANES), 1)
    first = lane < ATT_HEAD_DIM

    def head_norm(blk, g_ref, c, scale):
        sq = blk * blk
        s0 = jnp.sum(jnp.where(first, sq, 0.0), axis=-1, keepdims=True)
        s1 = jnp.sum(jnp.where(first, 0.0, sq), axis=-1, keepdims=True)
        r = jnp.where(first, lax.rsqrt(s0 * (1.0 / ATT_HEAD_DIM) + EPS),
                      lax.rsqrt(s1 * (1.0 / ATT_HEAD_DIM) + EPS))
        return blk * r * (g_ref[:, c * LANES:(c + 1) * LANES] * scale)

    chunk = 2 * LANES
    work = []

    def run(count):
        for _ in range(min(count, len(work))):
            work.pop(0)()

    xbc = proj(o_xbc, SSD_CONV_DIM)
    for s in range(nb):
        base = s * (rb + halo)
        if tiles_per_stream == 1:
            carried = cprev_ref[s]
        else:
            carried = jnp.where(i % tiles_per_stream == 0, cprev_ref[s], ext_ref[rb:rb + halo, :])
        ext_ref[base:base + halo, :] = carried
        ext_ref[base + halo:base + halo + rb, :] = xbc[s * rb:(s + 1) * rb, :]
        xtail_ref[s] = xbc[(s + 1) * rb - halo:(s + 1) * rb, :]

        def conv_piece(s=s, base=base, cb=0, r0=0, nr=rb):
            sl = slice(cb * LANES, (cb + 1) * LANES)
            xe = ext_ref[base + r0:base + r0 + halo + nr, sl]
            acc = convb_ref[:, sl] + convw_ref[SSD_CONV - 1:SSD_CONV, sl] * xe[halo:, :]
            for j in range(1, SSD_CONV):
                tap = pltpu.roll(xe, j, 0)[halo:, :]
                acc = acc + convw_ref[SSD_CONV - 1 - j:SSD_CONV - j, sl] * tap
            xc_ref[s * rb + r0:s * rb + r0 + nr, sl] = _silu(acc)

        nr = min(rb, CONV_ROWS)
        for cb in range(SSD_CONV_DIM // LANES):
            for r0 in range(0, rb, nr):
                work.append(functools.partial(conv_piece, cb=cb, r0=r0, nr=nr))

    def gate(zc, c):
        gz_ref[:, c * LANES:(c + 1) * LANES] = _silu(zc)

    def norm_q(blk, c):
        q_ref[:, c * LANES:(c + 1) * LANES] = head_norm(
            blk, gq_ref, c, ATT_HEAD_DIM ** -0.5 * LOG2E).astype(BF16)

    def norm_k(blk, c):
        kn = head_norm(blk, gk_ref, c, 1.0)
        k_ref[:, c * LANES:(c + 1) * LANES] = kn.astype(BF16)
        kt_ref[:, c * LANES:(c + 1) * LANES] = kn

    def chunks(loc, width, consumer, per_chunk):
        w_ref, lo = loc
        for c0 in range(0, width, chunk):
            y = proj((w_ref, lo + c0), chunk)
            for t in range(chunk // LANES):
                work.append(functools.partial(consumer, y[:, t * LANES:(t + 1) * LANES], c0 // LANES + t))
            run(per_chunk)

    per_chunk = -(-len(work) // 12) + 2
    chunks((wzx_ref, 0), SSD_WIDTH, gate, per_chunk)
    chunks(o_q, ATT_WIDTH, norm_q, per_chunk)
    chunks(o_k, ATT_WIDTH, norm_k, per_chunk)
    for c0 in range(0, ATT_WIDTH, chunk):
        if v_feature_major:
            v_ref[c0:c0 + chunk, :] = _dot_nt(wvt_ref[c0:c0 + chunk, :], xn).astype(BF16)
        else:
            v = proj((wqkv_ref, 2 * ATT_WIDTH + c0), chunk)
            v_ref[:, c0:c0 + chunk] = v.astype(BF16)
            vt_ref[:, c0:c0 + chunk] = v
        run(3)
    run(len(work))
    dt_ref[...] = proj(o_dt, DT_PAD)
    if v_feature_major:
        @pl.when(i >= n - n_tail)
        def _():
            vt_ref[...] = proj(o_v, ATT_WIDTH)


def _in_proj(x2d, g_mix, w_zx, w_dt, w_qkv, gq_t, gk_t, cprev8, conv_w, conv_b, *, tr, seq_len,
             tail_rows, v_feature_major):
    rows = x2d.shape[0]
    n = rows // tr
    n_tail = tail_rows // tr
    nb = max(tr // seq_len, 1)
    tiles_per_stream = max(seq_len // tr, 1)
    n_streams = rows // seq_len
    assert n * tr == rows and n_tail * tr == tail_rows and cprev8.shape == (n_streams, 8, SSD_CONV_DIM)
    assert (tr // nb) % 8 == 0

    def row(width):
        return pl.BlockSpec((tr, width), lambda i: (i, 0))

    def const(shape):
        return pl.BlockSpec(shape, lambda i: (0,) * len(shape))

    per_stream = pl.BlockSpec((nb, 8, SSD_CONV_DIM), lambda i: (i // tiles_per_stream, 0, 0))
    tail = pl.BlockSpec((tr, ATT_WIDTH), lambda i: (jnp.maximum(i - (n - n_tail), 0), 0))
    if v_feature_major:
        v_spec = pl.BlockSpec((ATT_WIDTH, tr), lambda i: (0, i))
        v_shape = jax.ShapeDtypeStruct((ATT_WIDTH, rows), BF16)
    else:
        v_spec = row(ATT_WIDTH)
        v_shape = jax.ShapeDtypeStruct((rows, ATT_WIDTH), BF16)
    return pl.pallas_call(
        functools.partial(_inproj_kernel, n_tail=n_tail, v_feature_major=v_feature_major, nb=nb,
                          tiles_per_stream=tiles_per_stream),
        grid=(n,),
        in_specs=[row(D_MODEL), const((1, D_MODEL)), const((D_MODEL, SSD_WIDTH + SSD_CONV_DIM)),
                  const((D_MODEL, DT_PAD)), const((D_MODEL, 3 * ATT_WIDTH)),
                  const((1, ATT_WIDTH)), const((1, ATT_WIDTH)),
                  per_stream, const((SSD_CONV, SSD_CONV_DIM)), const((1, SSD_CONV_DIM))],
        out_specs=[row(SSD_WIDTH), row(SSD_CONV_DIM), per_stream, row(DT_PAD), row(ATT_WIDTH),
                   row(ATT_WIDTH), v_spec, tail, tail],
        out_shape=[jax.ShapeDtypeStruct((rows, SSD_WIDTH), F32),
                   jax.ShapeDtypeStruct((rows, SSD_CONV_DIM), F32),
                   jax.ShapeDtypeStruct((n_streams, 8, SSD_CONV_DIM), F32),
                   jax.ShapeDtypeStruct((rows, DT_PAD), F32),
                   jax.ShapeDtypeStruct((rows, ATT_WIDTH), BF16),
                   jax.ShapeDtypeStruct((rows, ATT_WIDTH), BF16),
                   v_shape,
                   jax.ShapeDtypeStruct((tail_rows, ATT_WIDTH), F32),
                   jax.ShapeDtypeStruct((tail_rows, ATT_WIDTH), F32)],
        scratch_shapes=[pltpu.VMEM((nb * (tr // nb + 8), SSD_CONV_DIM), F32),
                        pltpu.VMEM((ATT_WIDTH, D_MODEL) if v_feature_major else (16, LANES), BF16)],
        compiler_params=pltpu.CompilerParams(dimension_semantics=("arbitrary",),
                                             vmem_limit_bytes=VMEM_LIMIT),
        name="in_proj",
    )(x2d, g_mix, w_zx, w_dt, w_qkv, gq_t, gk_t, cprev8, conv_w, conv_b)


def _ssd_pairs(xc_ref, dtraw_ref, gz_ref, dtb_ref, alog_ref, dskip_ref, expand_ref, ht_ref, ybuf_ref,
               *, t, lb):
    def rows_of(ref, sl):
        v = ref[0, :, sl]
        if lb < t:
            v = jnp.concatenate([v, jnp.zeros((t - lb, v.shape[1]), v.dtype)], axis=0)
        return v

    lane = lax.broadcasted_iota(jnp.int32, (t, LANES), 1)
    rowi = lax.broadcasted_iota(jnp.int32, (t, LANES), 0)
    dt = jax.nn.softplus(rows_of(dtraw_ref, slice(None)) + dtb_ref[...])
    dt = jnp.where((lane < SSD_HEADS) & (rowi < lb), dt, 0.0)
    a_neg = -jnp.exp(alog_ref[...]) * LOG2E
    a = dt * a_neg

    rr = lax.broadcasted_iota(jnp.int32, (t, t), 0)
    cc = lax.broadcasted_iota(jnp.int32, (t, t), 1)
    causal = rr >= cc
    tril = jnp.where(causal, 1.0, 0.0).astype(BF16)
    a1, a2, a3 = _split3(a)
    a_cum = _dot(tril, a1) + _dot(tril, a2) + _dot(tril, a3)
    a_last = a_cum[t - 1:t, :]
    ea = jnp.exp2(a_cum)
    cd = jnp.exp2(a_last)
    a_t = a_cum.T
    w_t = jnp.exp2(a_t[:, t - 1:t] - a_t)

    stacked = jnp.concatenate([ea, dt, jnp.broadcast_to(cd, (8, LANES))], axis=0)
    s_hi, s_lo = _split2(stacked)
    expanded = _dot(s_hi, expand_ref[...]) + _dot(s_lo, expand_ref[...])
    ea_x = expanded[0:t, :]
    dt_x = expanded[t:2 * t, :]
    cd_x = expanded[2 * t:2 * t + 1, :]

    first = lane < SSD_HEAD_DIM
    heads_per_group = SSD_HEADS // SSD_GROUPS
    o_b = SSD_WIDTH
    o_c = SSD_WIDTH + SSD_GROUPS * SSD_STATE
    ssq = [jnp.zeros((lb, 1), F32)]
    per_group = {}

    def group_values(g):
        if g not in per_group:
            bg = rows_of(xc_ref, slice(o_b + g * SSD_STATE, o_b + (g + 1) * SSD_STATE))
            cg = rows_of(xc_ref, slice(o_c + g * SSD_STATE, o_c + (g + 1) * SSD_STATE)).astype(BF16)
            cb_mat = jnp.where(causal, _dot_nt(cg, bg.astype(BF16)), 0.0)
            per_group[g] = (cg, cb_mat, bg.T)
        return per_group[g]

    def pair(j):
        cg, cb_mat, bg_t = group_values(j // (heads_per_group // 2))
        sl = slice(j * LANES, (j + 1) * LANES)
        xh = rows_of(xc_ref, sl)
        xdt = xh * dt_x[:, sl]
        y_pair = None
        s_pair = None
        for hh in range(2):
            h = 2 * j + hh
            xm = jnp.where(first if hh == 0 else jnp.logical_not(first), xdt, 0.0).astype(BF16)
            seg = a_cum[:, h:h + 1] - a_t[h:h + 1, :]
            m = (cb_mat * jnp.exp2(jnp.minimum(seg, 0.0))).astype(BF16)
            yd = _dot(m, xm)
            bw = (bg_t * w_t[h:h + 1, :]).astype(BF16)
            sd = _dot(bw, xm)
            y_pair = yd if y_pair is None else y_pair + yd
            s_pair = sd if s_pair is None else s_pair + sd
        h_in = ht_ref[:, sl]
        y_off = _dot(cg, h_in.astype(BF16)) * ea_x[:, sl]
        ht_ref[:, sl] = cd_x[:, sl] * h_in + s_pair
        y = y_pair + y_off + dskip_ref[:, sl] * xh
        yg = y[0:lb, :] * gz_ref[0, :, sl]
        ybuf_ref[:, sl] = yg
        ssq[0] = ssq[0] + jnp.sum(yg * yg, axis=-1, keepdims=True)

    return [functools.partial(pair, j) for j in range(SSD_WIDTH // LANES)], ssq


def _ssd_kernel(xc_ref, dtraw_ref, gz_ref, h0_ref, dtb_ref, alog_ref, dskip_ref, gout_ref, expand_ref,
                y_ref, hfin_ref,
                ht_ref, ybuf_ref, *, t, lb):
    c = pl.program_id(1)
    nc = pl.num_programs(1)

    @pl.when(c == 0)
    def _():
        ht_ref[...] = h0_ref[0].T

    pairs, ssq = _ssd_pairs(xc_ref, dtraw_ref, gz_ref, dtb_ref, alog_ref, dskip_ref, expand_ref, ht_ref,
                            ybuf_ref, t=t, lb=lb)
    for do_pair in pairs:
        do_pair()
    r = lax.rsqrt(ssq[0] * (1.0 / SSD_WIDTH) + EPS)
    y_ref[0] = (ybuf_ref[...] * r * gout_ref[...]).astype(BF16)

    @pl.when(c == nc - 1)
    def _():
        hfin_ref[0] = ht_ref[...].T


def _ssd_mixer(xc, dtraw, gz, h0, dtb, alog, dskip, gout, expand, *, t, lb):
    b, length, _ = xc.shape
    nc = length // lb
    assert nc * lb == length and (lb == t or nc == 1)

    def seq(width):
        return pl.BlockSpec((1, lb, width), lambda bi, ci: (bi, ci, 0))

    def per_b(shape):
        return pl.BlockSpec((1,) + shape, lambda bi, ci: (bi, 0, 0))

    def const(shape):
        return pl.BlockSpec(shape, lambda bi, ci: (0,) * len(shape))

    hp = SSD_WIDTH
    return pl.pallas_call(
        functools.partial(_ssd_kernel, t=t, lb=lb),
        grid=(b, nc),
        in_specs=[seq(SSD_CONV_DIM), seq(DT_PAD), seq(SSD_WIDTH), per_b((hp, SSD_STATE)),
                  const((1, DT_PAD)), const((1, DT_PAD)), const((1, SSD_WIDTH)), const((1, SSD_WIDTH)),
                  const((LANES, SSD_WIDTH))],
        out_specs=[seq(SSD_WIDTH), per_b((hp, SSD_STATE))],
        out_shape=[jax.ShapeDtypeStruct((b, length, SSD_WIDTH), BF16),
                   jax.ShapeDtypeStruct((b, hp, SSD_STATE), F32)],
        scratch_shapes=[pltpu.VMEM((SSD_STATE, hp), F32),
                        pltpu.VMEM((lb, SSD_WIDTH), F32)],
        compiler_params=pltpu.CompilerParams(dimension_semantics=("arbitrary", "arbitrary"),
                                             vmem_limit_bytes=VMEM_LIMIT),
        name="ssd_mixer",
    )(xc, dtraw, gz, h0, dtb, alog, dskip, gout, expand)


ATT_SB = 2 * CHUNK
ATT_WIN = ATT_SB + PAST


def _toeplitz_rows(base_row, rows, offset, width):
    ext = base_row.shape[1]
    rolled = pltpu.roll(jnp.broadcast_to(base_row, (rows, ext)), 0, 1, stride=1, stride_axis=0)
    return rolled[:, offset:offset + width]


def _band_subblock(q, kw, vt, bias_t, first):
    return _band_softmax_pv(_band_scores(q, kw, bias_t, first), vt)


def _band_scores(q, kw, bias_t, first):
    zero = jnp.zeros_like(q)
    q2 = jnp.concatenate([jnp.where(first, q, zero), jnp.where(first, zero, q)], axis=0)
    return _dot_nt(kw, q2) + bias_t


def _band_softmax(s):
    m = jnp.max(s, axis=0, keepdims=True)
    p = jnp.exp2(s - m)
    return p.astype(BF16), 1.0 / jnp.sum(p, axis=0, keepdims=True)


def _band_pv(p, inv_l, vt):
    sb = p.shape[1] // 2
    half = ATT_HEAD_DIM
    o_t = _dot(vt, p)
    o_pair_t = jnp.concatenate([o_t[0:half, 0:sb] * inv_l[:, 0:sb],
                                o_t[half:2 * half, sb:2 * sb] * inv_l[:, sb:2 * sb]], axis=0)
    return o_pair_t.T.astype(BF16)


def _band_softmax_pv(s, vt):
    p, inv_l = _band_softmax(s)
    return _band_pv(p, inv_l, vt)


def _band_prompt_kernel(q_ref, k_ref, vt_ref, base_ref, o_ref, bias_ref, sc_ref, pb_ref, il_ref):
    length = q_ref.shape[0]
    sb, w = ATT_SB, ATT_WIN
    kj = lax.broadcasted_iota(jnp.int32, (w, sb), 0) // CHUNK
    qi = lax.broadcasted_iota(jnp.int32, (w, sb), 1) // CHUNK
    in_band = (kj >= qi) & (kj <= qi + N_LEFT_CHUNKS)
    for hh in range(2):
        toe = _toeplitz_rows(base_ref[0, hh:hh + 1, :], w, w, sb)
        bias_ref[:, hh * sb:(hh + 1) * sb] = jnp.where(in_band, toe * LOG2E, NEG)

    lane = lax.broadcasted_iota(jnp.int32, (sb, LANES), 1)
    first = lane < ATT_HEAD_DIM
    n_sub = length // sb
    n_head = min(PAST // sb, n_sub)
    for j in range(n_head):
        n = (j + 1) * sb
        o_ref[j * sb:(j + 1) * sb, :] = _band_subblock(
            q_ref[j * sb:(j + 1) * sb, :], k_ref[0:n, :], vt_ref[:, 0:n], bias_ref[w - n:, :], first)

    n_main = n_sub - n_head
    if n_main == 0:
        return
    assert n_main % 2 == 0

    def scores_into(slot, j):
        r0 = pl.multiple_of(j * sb, sb)
        k0 = pl.multiple_of(j * sb - PAST, sb)
        sc_ref[slot] = _band_scores(q_ref[pl.ds(r0, sb), :], k_ref[pl.ds(k0, w), :], bias_ref[...], first)

    def softmax_into(slot):
        p, inv_l = _band_softmax(sc_ref[slot])
        pb_ref[slot] = p
        il_ref[slot] = jnp.broadcast_to(inv_l, (8, 2 * sb))

    def values_out(slot, j):
        k0 = pl.multiple_of(j * sb - PAST, sb)
        return _band_pv(pb_ref[slot], il_ref[slot, 0:1, :], vt_ref[:, pl.ds(k0, w)])

    last = n_sub - 1
    scores_into(0, n_head)
    scores_into(1, n_head + 1)
    softmax_into(0)

    per_trip = 4 if n_main % 4 == 0 else 2

    def body(i, carry):
        outs = []
        for u in range(per_trip):
            j = n_head + per_trip * i + u
            slot = u % 2
            scores_into(slot, jnp.minimum(j + 2, last))
            softmax_into(1 - slot)
            outs.append((j, values_out(slot, j)))
        for j, o in outs:
            o_ref[pl.ds(pl.multiple_of(j * sb, sb), sb), :] = o
        return carry

    lax.fori_loop(0, n_main // per_trip, body, 0)


def _band_prompt(q, k, vt, base):
    length = q.shape[0]
    assert length % ATT_SB == 0
    ext = ATT_SB + ATT_WIN
    col = pl.BlockSpec((length, LANES), lambda c: (0, c))
    return pl.pallas_call(
        _band_prompt_kernel,
        grid=(ATT_WIDTH // LANES,),
        in_specs=[col, col, pl.BlockSpec((LANES, length), lambda c: (c, 0)),
                  pl.BlockSpec((1, 2, ext), lambda c: (c, 0, 0))],
        out_specs=col,
        out_shape=jax.ShapeDtypeStruct((length, ATT_WIDTH), BF16),
        scratch_shapes=[pltpu.VMEM((ATT_WIN, 2 * ATT_SB), F32),
                        pltpu.VMEM((2, ATT_WIN, 2 * ATT_SB), F32),
                        pltpu.VMEM((2, ATT_WIN, 2 * ATT_SB), BF16),
                        pltpu.VMEM((2, 8, 2 * ATT_SB), F32)],
        compiler_params=pltpu.CompilerParams(dimension_semantics=("arbitrary",),
                                             vmem_limit_bytes=VMEM_LIMIT),
        name="band_prompt",
    )(q, k, vt, base.reshape(ATT_HEADS // 2, 2, ext))


def _band_sample_kernel(q_ref, kn_ref, vn_ref, kc_ref, vc_ref, base_ref, o_ref, bias_ref):
    lq = q_ref.shape[0]
    lc = kc_ref.shape[3]

    @pl.when(pl.program_id(0) == 0)
    def _():
        for h in range(ATT_HEADS):
            bias_ref[h] = _toeplitz_rows(base_ref[h:h + 1, :], lq, LANES, lc + LANES) * LOG2E

    for h in range(ATT_HEADS):
        sl = slice(h * ATT_HEAD_DIM, (h + 1) * ATT_HEAD_DIM)
        q = q_ref[:, sl]
        sc = _dot(q, kc_ref[0, h]) + bias_ref[h, :, 0:lc]
        sn = _dot_nt(q, kn_ref[:, sl]) + bias_ref[h, :, lc:lc + lq]
        m = jnp.maximum(jnp.max(sc, axis=-1, keepdims=True), jnp.max(sn, axis=-1, keepdims=True))
        pc = jnp.exp2(sc - m)
        pn = jnp.exp2(sn - m)
        l = jnp.sum(pc, axis=-1, keepdims=True) + jnp.sum(pn, axis=-1, keepdims=True)
        o = _dot_nt(pc.astype(BF16), vc_ref[0, h]) + _dot(pn.astype(BF16), vn_ref[:, sl])
        o_ref[:, sl] = (o * (1.0 / l)).astype(BF16)


def _band_sample(q, kn, vn, kc, vc, base, *, lq):
    rows = q.shape[0]
    nb = rows // lq
    lc = kc.shape[3]
    assert lq <= LANES and lc % LANES == 0 and base.shape == (ATT_HEADS, lc + 2 * LANES)
    blk = pl.BlockSpec((lq, ATT_WIDTH), lambda b: (b, 0))
    cache = pl.BlockSpec((1, ATT_HEADS, ATT_HEAD_DIM, lc), lambda b: (b, 0, 0, 0))
    return pl.pallas_call(
        _band_sample_kernel,
        grid=(nb,),
        in_specs=[blk, blk, blk, cache, cache, pl.BlockSpec(base.shape, lambda b: (0, 0))],
        out_specs=blk,
        out_shape=jax.ShapeDtypeStruct((rows, ATT_WIDTH), BF16),
        scratch_shapes=[pltpu.VMEM((ATT_HEADS, lq, lc + LANES), F32)],
        compiler_params=pltpu.CompilerParams(dimension_semantics=("arbitrary",),
                                             vmem_limit_bytes=VMEM_LIMIT),
        name="band_sample",
    )(q, kn, vn, kc, vc, base)


def _mem_head_norm(y, g_ref, hd, scale):
    blk = y[:, hd * MEM_HEAD_DIM:(hd + 1) * MEM_HEAD_DIM]
    r = lax.rsqrt(jnp.mean(blk * blk, axis=-1, keepdims=True) + EPS)
    return blk * r * (g_ref[...] * scale)


def _memkv_kernel(mem_ref, gsrc_ref, wk_ref, wv_ref, gk_ref, k_ref, v_ref):
    m = _rms(mem_ref[...], gsrc_ref[...]).astype(BF16)
    k = _dot(m, wk_ref[...])
    for hd in range(MEM_HEADS):
        k_ref[:, hd * MEM_HEAD_DIM:(hd + 1) * MEM_HEAD_DIM] = _mem_head_norm(k, gk_ref, hd, 1.0)
    v_ref[...] = _dot(m, wv_ref[...])


def _memory_kv(mem2d, g_src, w_mk, w_mv, g_mk):
    rows = mem2d.shape[0]

    def full(shape):
        return pl.BlockSpec(shape, lambda i: (0,) * len(shape))

    return pl.pallas_call(
        _memkv_kernel,
        grid=(1,),
        in_specs=[full((rows, D_MODEL)), full((1, D_MODEL)), full((D_MODEL, D_MODEL)),
                  full((D_MODEL, D_MODEL)), full((1, MEM_HEAD_DIM))],
        out_specs=[full((rows, D_MODEL)), full((rows, D_MODEL))],
        out_shape=[jax.ShapeDtypeStruct((rows, D_MODEL), F32)] * 2,
        compiler_params=pltpu.CompilerParams(dimension_semantics=("arbitrary",),
                                             vmem_limit_bytes=VMEM_LIMIT),
        name="memory_kv",
    )(mem2d, g_src, w_mk, w_mv, g_mk)


def _outproj_mem_kernel(x_ref, ys_ref, ya_ref, wo_ref, gmx_ref, wq_ref, gmq_ref, mk_ref, mv_ref,
                        wmo_ref, h_ref, obuf_ref, *, nb, rb, dense_rows):
    tr = x_ref.shape[0]
    groups = [slice(r0, r0 + dense_rows) for r0 in range(0, tr, dense_rows)]
    h = [x_ref[g, :] + _dot(ys_ref[g, :], wo_ref[0:SSD_WIDTH, :]) + _dot(ya_ref[g, :], wo_ref[SSD_WIDTH:, :])
         for g in groups]
    hn = [_rms(hg, gmx_ref[...]).astype(BF16) for hg in h]
    q = [_dot(hg, wq_ref[...]) for hg in hn]
    for hd in range(MEM_HEADS):
        sl = slice(hd * MEM_HEAD_DIM, (hd + 1) * MEM_HEAD_DIM)
        qn = [_mem_head_norm(qg, gmq_ref, hd, MEM_HEAD_DIM ** -0.5 * LOG2E).astype(BF16) for qg in q]
        for g, qg in zip(groups, qn):
            for r0 in range(g.start, g.stop, rb):
                b = r0 // rb if nb > 1 else 0
                s = _dot_nt(qg[r0 - g.start:r0 - g.start + min(rb, dense_rows), :], mk_ref[b, :, sl])
                p = jnp.exp2(s - jnp.max(s, axis=-1, keepdims=True))
                o = _dot(p.astype(BF16), mv_ref[b, :, sl]) * (1.0 / jnp.sum(p, axis=-1, keepdims=True))
                obuf_ref[r0:r0 + min(rb, dense_rows), sl] = o.astype(BF16)
    for g, hg in zip(groups, h):
        h_ref[g, :] = hg + _dot(obuf_ref[g, :], wmo_ref[...])


def _outproj_mem(x2d, ys, ya, w_out, g_mem_x, w_mq, g_mq, mk, mv, w_mo, *, tr, seq_len):
    rows = x2d.shape[0]
    n = rows // tr
    nb = max(tr // seq_len, 1)
    tiles_per_stream = max(seq_len // tr, 1)
    rb = tr // nb
    assert n * tr == rows and nb * rb == tr and mk.shape[0] * seq_len == rows

    def row(width):
        return pl.BlockSpec((tr, width), lambda i: (i, 0))

    def const(shape):
        return pl.BlockSpec(shape, lambda i: (0,) * len(shape))

    mem = pl.BlockSpec((nb, N_MEM, D_MODEL), lambda i: (i // tiles_per_stream, 0, 0))
    return pl.pallas_call(
        functools.partial(_outproj_mem_kernel, nb=nb, rb=rb, dense_rows=min(tr, MIX_DENSE_ROWS)),
        grid=(n,),
        in_specs=[row(D_MODEL), row(SSD_WIDTH), row(ATT_WIDTH), const((SSD_WIDTH + ATT_WIDTH, D_MODEL)),
                  const((1, D_MODEL)), const((D_MODEL, D_MODEL)), const((1, MEM_HEAD_DIM)), mem, mem,
                  const((D_MODEL, D_MODEL))],
        out_specs=row(D_MODEL),
        out_shape=jax.ShapeDtypeStruct((rows, D_MODEL), F32),
        scratch_shapes=[pltpu.VMEM((tr, D_MODEL), BF16)],
        compiler_params=pltpu.CompilerParams(dimension_semantics=("arbitrary",),
                                             vmem_limit_bytes=VMEM_LIMIT),
        name="outproj_mem",
    )(x2d, ys, ya, w_out, g_mem_x, w_mq, g_mq, mk, mv, w_mo)


FF_SLAB = 1024


def _ffn_kernel(h_ref, g_ref, w1_ref, w2_ref, y_ref):
    h = h_ref[...]
    hn = _rms(h, g_ref[...]).astype(BF16)
    acc = h
    for s in range(D_FF // FF_SLAB):
        u = jnp.maximum(_dot(hn, w1_ref[:, s * FF_SLAB:(s + 1) * FF_SLAB]), 0.0)
        acc = acc + _dot((u * u).astype(BF16), w2_ref[s * FF_SLAB:(s + 1) * FF_SLAB, :])
    y_ref[...] = acc


def _ffn(h2d, g_ffn, w1, w2, *, tr):
    rows = h2d.shape[0]
    n = rows // tr
    assert n * tr == rows

    def const(shape):
        return pl.BlockSpec(shape, lambda i: (0,) * len(shape))

    row = pl.BlockSpec((tr, D_MODEL), lambda i: (i, 0))
    return pl.pallas_call(
        _ffn_kernel,
        grid=(n,),
        in_specs=[row, const((1, D_MODEL)), const((D_MODEL, D_FF)), const((D_FF, D_MODEL))],
        out_specs=row,
        out_shape=jax.ShapeDtypeStruct((rows, D_MODEL), F32),
        compiler_params=pltpu.CompilerParams(dimension_semantics=("arbitrary",),
                                             vmem_limit_bytes=VMEM_LIMIT),
        name="ffn",
    )(h2d, g_ffn, w1, w2)


def _toeplitz_base(table, offset, width):
    heads, size = table.shape
    n_far = offset - REL_CLIP
    assert n_far >= 0
    parts = [jnp.broadcast_to(table[:, size - 1:], (heads, n_far)), table[:, ::-1]]
    rest = width - n_far - size
    if rest > 0:
        parts.append(jnp.broadcast_to(table[:, :1], (heads, rest)))
    return jnp.concatenate(parts, axis=1)[:, :width].astype(F32)


def _toeplitz_base_t(table, shift, width):
    heads, size = table.shape
    n_low = shift - REL_CLIP
    assert n_low >= 0
    parts = [jnp.broadcast_to(table[:, :1], (heads, n_low)), table]
    rest = width - n_low - size
    if rest > 0:
        parts.append(jnp.broadcast_to(table[:, size - 1:], (heads, rest)))
    return jnp.concatenate(parts, axis=1)[:, :width].astype(F32)


def _prep_weights(g_mix, w_in, conv_w, conv_b, ssd_A_log, ssd_dt_bias, ssd_D, ssd_g_out, att_g_q, att_g_k,
                  w_out, g_mem_x, g_mem_src, w_mq, w_mk, w_mv, g_mq, g_mk, w_mo, g_ffn, w_ff1, w_ff2):
    w_zx, w_dt, w_qkv = _split_w_in(w_in)
    pad_h = lambda v: jnp.pad(v, (0, DT_PAD - SSD_HEADS)).reshape(1, DT_PAD)
    row = lambda v: v.reshape(1, -1)
    expand = (jnp.arange(LANES)[:, None] == jnp.arange(SSD_WIDTH)[None, :] // SSD_HEAD_DIM).astype(BF16)
    return dict(
        g_mix=row(g_mix), w_zx=w_zx, w_dt=w_dt, w_qkv=w_qkv,
        gq_t=row(jnp.tile(att_g_q, ATT_HEADS)), gk_t=row(jnp.tile(att_g_k, ATT_HEADS)),
        conv_w=conv_w, conv_b=row(conv_b), dtb=pad_h(ssd_dt_bias), alog=pad_h(ssd_A_log),
        dskip=row(jnp.repeat(ssd_D, SSD_HEAD_DIM)), gout=row(ssd_g_out), expand=expand,
        w_out=w_out.astype(BF16), g_mem_x=row(g_mem_x), g_mem_src=row(g_mem_src),
        w_mq=w_mq.astype(BF16), w_mk=w_mk.astype(BF16), w_mv=w_mv.astype(BF16),
        g_mq=row(g_mq), g_mk=row(g_mk), w_mo=w_mo.astype(BF16), g_ffn=row(g_ffn),
        w_ff1=w_ff1.astype(BF16), w_ff2=w_ff2.astype(BF16))


def _layer(x, conv_prev, h0, k_cache, v_cache, mem_k, mem_v, p, rel, *, tr, t_scan):
    b, length, _ = x.shape
    rows = b * length
    x2d = x.reshape(rows, D_MODEL)
    prompt = k_cache is None
    tail_rows = PAST if prompt else rows
    cprev8 = jnp.pad(conv_prev, ((0, 0), (8 - (SSD_CONV - 1), 0), (0, 0)))
    gz, xc, xtail, dtraw, q, k, v, k_tail, v_tail = _in_proj(
        x2d, p["g_mix"], p["w_zx"], p["w_dt"], p["w_qkv"], p["gq_t"], p["gk_t"], cprev8,
        p["conv_w"], p["conv_b"], tr=tr, seq_len=length, tail_rows=tail_rows, v_feature_major=prompt)
    conv_new = xtail[:, 8 - (SSD_CONV - 1):]

    scan_args = (xc.reshape(b, length, SSD_CONV_DIM), dtraw.reshape(b, length, DT_PAD),
                 gz.reshape(b, length, SSD_WIDTH), h0.reshape(b, SSD_WIDTH, SSD_STATE),
                 p["dtb"], p["alog"], p["dskip"], p["gout"], p["expand"])
    mix_args = (p["w_out"], p["g_mem_x"], p["w_mq"], p["g_mq"], mem_k.astype(BF16), mem_v.astype(BF16),
                p["w_mo"])
    y_ssd, h_fin = _ssd_mixer(*scan_args, t=t_scan, lb=t_scan if prompt else length)
    if prompt:
        y_att = _band_prompt(q, k, v, _toeplitz_base_t(rel, ATT_SB, ATT_SB + ATT_WIN))
        k_rows = k_tail.reshape(b, PAST, ATT_HEADS, ATT_HEAD_DIM)
        v_rows = v_tail.reshape(b, PAST, ATT_HEADS, ATT_HEAD_DIM)
    else:
        lc = k_cache.shape[1]
        y_att = _band_sample(q, k, v, jnp.transpose(k_cache, (0, 2, 3, 1)).astype(BF16),
                             jnp.transpose(v_cache, (0, 2, 3, 1)).astype(BF16),
                             _toeplitz_base(rel, lc + LANES, lc + 2 * LANES), lq=length)
        k_rows = k_tail.reshape(b, length, ATT_HEADS, ATT_HEAD_DIM)
        v_rows = v_tail.reshape(b, length, ATT_HEADS, ATT_HEAD_DIM)
    tr_mix = 2 * tr if prompt and rows % (2 * tr) == 0 else tr
    h = _outproj_mem(x2d, y_ssd.reshape(rows, SSD_WIDTH), y_att, *mix_args, tr=tr_mix, seq_len=length)
    y = _ffn(h, p["g_ffn"], p["w_ff1"], p["w_ff2"], tr=tr)
    return (y.reshape(b, length, D_MODEL), h_fin.reshape(b, SSD_HEADS, SSD_HEAD_DIM, SSD_STATE),
            conv_new, k_rows, v_rows)


def kernel(x_prompt, x_sample, mem_prompt, state_ssd, state_conv, cache_attn_k, cache_attn_v, cache_mem_k,
           cache_mem_v, g_mix, w_in, conv_w, conv_b, ssd_A_log, ssd_dt_bias, ssd_D, ssd_g_out, att_g_q,
           att_g_k, att_rel_bias, w_out, g_mem_x, g_mem_src, w_mq, w_mk, w_mv, g_mq, g_mk, w_mo, g_ffn,
           w_ff1, w_ff2):
    depth = g_mix.shape[0]
    b_p, seq, _ = x_prompt.shape
    b_s, dec_seq, _ = x_sample.shape
    yp, ys = x_prompt, x_sample
    outs = [[] for _ in range(10)]
    for l in range(depth):
        p = _prep_weights(g_mix[l], w_in[l], conv_w[l], conv_b[l], ssd_A_log[l], ssd_dt_bias[l], ssd_D[l],
                          ssd_g_out[l], att_g_q[l], att_g_k[l], w_out[l], g_mem_x[l], g_mem_src[l],
                          w_mq[l], w_mk[l], w_mv[l], g_mq[l], g_mk[l], w_mo[l], g_ffn[l], w_ff1[l], w_ff2[l])
        rel = att_rel_bias[l]
        mk, mv = _memory_kv(mem_prompt.reshape(b_p * N_MEM, D_MODEL), p["g_mem_src"], p["w_mk"], p["w_mv"],
                            p["g_mk"])
        mk = mk.reshape(b_p, N_MEM, D_MODEL)
        mv = mv.reshape(b_p, N_MEM, D_MODEL)
        conv0 = jnp.zeros((b_p, SSD_CONV - 1, SSD_CONV_DIM), F32)
        h00 = jnp.zeros((b_p, SSD_HEADS, SSD_HEAD_DIM, SSD_STATE), F32)
        yp, hp, cp, kp, vp = _layer(yp, conv0, h00, None, None, mk, mv, p, rel,
                                    tr=min(256, seq), t_scan=min(256, seq))
        ys, hs, cs, ks_, vs_ = _layer(ys, state_conv[l], state_ssd[l], cache_attn_k[l], cache_attn_v[l],
                                      cache_mem_k[l].reshape(b_s, N_MEM, D_MODEL),
                                      cache_mem_v[l].reshape(b_s, N_MEM, D_MODEL), p, rel,
                                      tr=b_s * dec_seq, t_scan=128)
        for lst, val in zip(outs, (hp, cp, kp, vp,
                                   mk.reshape(b_p, N_MEM, MEM_HEADS, MEM_HEAD_DIM),
                                   mv.reshape(b_p, N_MEM, MEM_HEADS, MEM_HEAD_DIM),
                                   hs, cs, ks_, vs_)):
            lst.append(val)
    return (yp, ys) + tuple(jnp.stack(o) for o in outs)
```

```python
import functools

import jax
import jax.numpy as jnp
from jax import lax
from jax.experimental import pallas as pl
from jax.experimental.pallas import tpu as pltpu

F32 = jnp.float32
BF16 = jnp.bfloat16

D_MODEL = 1024
CHUNK = 64
SSD_HEADS = 16
SSD_HEAD_DIM = 64
SSD_WIDTH = SSD_HEADS * SSD_HEAD_DIM
SSD_GROUPS = 2
SSD_STATE = 128
SSD_CONV = 4
SSD_CONV_DIM = SSD_WIDTH + 2 * SSD_GROUPS * SSD_STATE
ATT_HEADS = 16
ATT_HEAD_DIM = 64
ATT_WIDTH = ATT_HEADS * ATT_HEAD_DIM
N_LEFT_CHUNKS = 8
PAST = N_LEFT_CHUNKS * CHUNK
REL_CLIP = 128
N_MEM = 256
MEM_HEADS = 4
MEM_HEAD_DIM = D_MODEL // MEM_HEADS
D_FF = 4 * D_MODEL
EPS = 1e-6

LANES = 128
DT_PAD = LANES
NEG = -1e30
LOG2E = 1.4426950408889634
CONV_ROWS = 64
MIX_DENSE_ROWS = 256
IN_GROUP_ROWS = 256
VMEM_LIMIT = 56 * 1024 * 1024


def _rms(x, g):
    return x * lax.rsqrt(jnp.mean(x * x, axis=-1, keepdims=True) + EPS) * g


def _silu(x):
    h = 0.5 * x
    return h + h * jnp.tanh(h)


def _split2(x):
    hi = x.astype(BF16)
    lo = (x - hi.astype(F32)).astype(BF16)
    return hi, lo


def _split3(x):
    hi = x.astype(BF16)
    r = x - hi.astype(F32)
    mid = r.astype(BF16)
    lo = (r - mid.astype(F32)).astype(BF16)
    return hi, mid, lo


def _resident(shape):
    return pl.BlockSpec(shape, lambda *_: (0,) * len(shape), pipeline_mode=pl.Buffered(1))


def _dot(a, b):
    return jnp.dot(a, b, preferred_element_type=F32)


def _dot_nt(a, b):
    return lax.dot_general(a, b, (((1,), (1,)), ((), ())), preferred_element_type=F32)


W_PREP_ROWS = 128


def _split_win_kernel(w_ref, wzx_ref, wdt_ref, wqkv_ref):
    o_dt = SSD_WIDTH + SSD_CONV_DIM
    o_qkv = o_dt + SSD_HEADS
    wzx_ref[...] = w_ref[:, 0:o_dt].astype(BF16)
    lane = lax.broadcasted_iota(jnp.int32, (w_ref.shape[0], DT_PAD), 1)
    wdt_ref[...] = jnp.where(lane < SSD_HEADS, w_ref[:, o_dt:o_dt + DT_PAD], 0.0).astype(BF16)
    wqkv_ref[...] = w_ref[:, o_qkv:o_qkv + 3 * ATT_WIDTH].astype(BF16)


def _split_w_in(w_in):
    d, width = w_in.shape
    n = d // W_PREP_ROWS
    widths = (SSD_WIDTH + SSD_CONV_DIM, DT_PAD, 3 * ATT_WIDTH)
    return pl.pallas_call(
        _split_win_kernel,
        grid=(n,),
        in_specs=[pl.BlockSpec((W_PREP_ROWS, width), lambda i: (i, 0))],
        out_specs=[pl.BlockSpec((W_PREP_ROWS, w), lambda i: (i, 0)) for w in widths],
        out_shape=[jax.ShapeDtypeStruct((d, w), BF16) for w in widths],
        compiler_params=pltpu.CompilerParams(dimension_semantics=("arbitrary",),
                                             vmem_limit_bytes=VMEM_LIMIT),
        name="split_w_in",
    )(w_in)


def _inproj_kernel(x_ref, gmix_ref, wzx_ref, wdt_ref, wqkv_ref, gq_ref, gk_ref, cprev_ref,
                   convw_ref, convb_ref,
                   gz_ref, xc_ref, xtail_ref, dt_ref, q_ref, k_ref, v_ref, kt_ref, vt_ref, ext_ref, wvt_ref,
                   *, n_tail, v_feature_major, nb, tiles_per_stream):
    i = pl.program_id(0)
    n = pl.num_programs(0)
    tr = x_ref.shape[0]
    rb = tr // nb
    halo = 8
    if v_feature_major:
        @pl.when(i == 0)
        def _():
            wvt_ref[...] = wqkv_ref[:, 2 * ATT_WIDTH:].T
    grp = min(tr, IN_GROUP_ROWS)
    groups = [slice(r0, r0 + grp) for r0 in range(0, tr, grp)]
    xn = [_rms(x_ref[g, :], gmix_ref[...]).astype(BF16) for g in groups]

    def proj(gi, loc, width):
        w_ref, lo = loc
        return _dot(xn[gi], w_ref[:, lo:lo + width])

    o_xbc = (wzx_ref, SSD_WIDTH)
    o_dt = (wdt_ref, 0)
    o_q = (wqkv_ref, 0)
    o_k = (wqkv_ref, ATT_WIDTH)
    o_v = (wqkv_ref, 2 * ATT_WIDTH)
    lane = lax.broadcasted_iota(jnp.int32, (grp, LANES), 1)
    first = lane < ATT_HEAD_DIM

    def head_norm(blk, g_ref, c, scale):
        sq = blk * blk
        s0 = jnp.sum(jnp.where(first, sq, 0.0), axis=-1, keepdims=True)
        s1 = jnp.sum(jnp.where(first, 0.0, sq), axis=-1, keepdims=True)
        r = jnp.where(first, lax.rsqrt(s0 * (1.0 / ATT_HEAD_DIM) + EPS),
                      lax.rsqrt(s1 * (1.0 / ATT_HEAD_DIM) + EPS))
        return blk * r * (g_ref[:, c * LANES:(c + 1) * LANES] * scale)

    chunk = 2 * LANES
    work = []

    def run(count):
        for _ in range(min(count, len(work))):
            work.pop(0)()

    xbc = [proj(gi, o_xbc, SSD_CONV_DIM) for gi in range(len(groups))]

    def xbc_rows(r0, r1):
        gi = r0 // grp
        return xbc[gi][r0 - gi * grp:r1 - gi * grp, :]

    for s in range(nb):
        base = s * (rb + halo)
        if tiles_per_stream == 1:
            carried = cprev_ref[s]
        else:
            carried = jnp.where(i % tiles_per_stream == 0, cprev_ref[s], ext_ref[rb:rb + halo, :])
        ext_ref[base:base + halo, :] = carried
        piece = min(rb, grp)
        for r0 in range(0, rb, piece):
            ext_ref[base + halo + r0:base + halo + r0 + piece, :] = xbc_rows(s * rb + r0, s * rb + r0 + piece)
        xtail_ref[s] = xbc_rows((s + 1) * rb - halo, (s + 1) * rb)

        def conv_piece(s=s, base=base, cb=0, r0=0, nr=rb):
            sl = slice(cb * LANES, (cb + 1) * LANES)
            xe = ext_ref[base + r0:base + r0 + halo + nr, sl]
            acc = convb_ref[:, sl] + convw_ref[SSD_CONV - 1:SSD_CONV, sl] * xe[halo:, :]
            for j in range(1, SSD_CONV):
                tap = pltpu.roll(xe, j, 0)[halo:, :]
                acc = acc + convw_ref[SSD_CONV - 1 - j:SSD_CONV - j, sl] * tap
            xc_ref[s * rb + r0:s * rb + r0 + nr, sl] = _silu(acc)

        nr = min(rb, CONV_ROWS)
        for cb in range(SSD_CONV_DIM // LANES):
            for r0 in range(0, rb, nr):
                work.append(functools.partial(conv_piece, cb=cb, r0=r0, nr=nr))

    def gate(zc, c, g):
        gz_ref[g, c * LANES:(c + 1) * LANES] = _silu(zc)

    def norm_q(blk, c, g):
        q_ref[g, c * LANES:(c + 1) * LANES] = head_norm(
            blk, gq_ref, c, ATT_HEAD_DIM ** -0.5 * LOG2E).astype(BF16)

    def norm_k(blk, c, g):
        kn = head_norm(blk, gk_ref, c, 1.0)
        k_ref[g, c * LANES:(c + 1) * LANES] = kn.astype(BF16)
        kt_ref[g, c * LANES:(c + 1) * LANES] = kn

    def chunks(loc, width, consumer, per_chunk):
        w_ref, lo = loc
        for c0 in range(0, width, chunk):
            for gi, g in enumerate(groups):
                y = proj(gi, (w_ref, lo + c0), chunk)
                for t in range(chunk // LANES):
                    work.append(functools.partial(consumer, y[:, t * LANES:(t + 1) * LANES], c0 // LANES + t, g))
                run(per_chunk)

    per_chunk = -(-len(work) // (12 * len(groups))) + 2
    chunks((wzx_ref, 0), SSD_WIDTH, gate, per_chunk)
    chunks(o_q, ATT_WIDTH, norm_q, per_chunk)
    chunks(o_k, ATT_WIDTH, norm_k, per_chunk)
    for c0 in range(0, ATT_WIDTH, chunk):
        for gi, g in enumerate(groups):
            if v_feature_major:
                v_ref[c0:c0 + chunk, g] = _dot_nt(wvt_ref[c0:c0 + chunk, :], xn[gi]).astype(BF16)
            else:
                v = proj(gi, (wqkv_ref, 2 * ATT_WIDTH + c0), chunk)
                v_ref[g, c0:c0 + chunk] = v.astype(BF16)
                vt_ref[g, c0:c0 + chunk] = v
            run(3)
    run(len(work))
    for gi, g in enumerate(groups):
        dt_ref[g, :] = proj(gi, o_dt, DT_PAD)
    if v_feature_major:
        @pl.when(i >= n - n_tail)
        def _():
            for gi, g in enumerate(groups):
                vt_ref[g, :] = proj(gi, o_v, ATT_WIDTH)


def _in_proj(x2d, g_mix, w_zx, w_dt, w_qkv, gq_t, gk_t, cprev8, conv_w, conv_b, *, tr, seq_len,
             tail_rows, v_feature_major):
    rows = x2d.shape[0]
    n = rows // tr
    n_tail = tail_rows // tr
    nb = max(tr // seq_len, 1)
    tiles_per_stream = max(seq_len // tr, 1)
    n_streams = rows // seq_len
    assert n * tr == rows and n_tail * tr == tail_rows and cprev8.shape == (n_streams, 8, SSD_CONV_DIM)
    assert (tr // nb) % 8 == 0

    def row(width):
        return pl.BlockSpec((tr, width), lambda i: (i, 0))

    def const(shape):
        return pl.BlockSpec(shape, lambda i: (0,) * len(shape))

    per_stream = pl.BlockSpec((nb, 8, SSD_CONV_DIM), lambda i: (i // tiles_per_stream, 0, 0))
    tail = pl.BlockSpec((tr, ATT_WIDTH), lambda i: (jnp.maximum(i - (n - n_tail), 0), 0))
    if v_feature_major:
        v_spec = pl.BlockSpec((ATT_WIDTH, tr), lambda i: (0, i))
        v_shape = jax.ShapeDtypeStruct((ATT_WIDTH, rows), BF16)
    else:
        v_spec = row(ATT_WIDTH)
        v_shape = jax.ShapeDtypeStruct((rows, ATT_WIDTH), BF16)
    return pl.pallas_call(
        functools.partial(_inproj_kernel, n_tail=n_tail, v_feature_major=v_feature_major, nb=nb,
                          tiles_per_stream=tiles_per_stream),
        grid=(n,),
        in_specs=[row(D_MODEL), const((1, D_MODEL)), _resident((D_MODEL, SSD_WIDTH + SSD_CONV_DIM)),
                  _resident((D_MODEL, DT_PAD)), _resident((D_MODEL, 3 * ATT_WIDTH)),
                  const((1, ATT_WIDTH)), const((1, ATT_WIDTH)),
                  per_stream, const((SSD_CONV, SSD_CONV_DIM)), const((1, SSD_CONV_DIM))],
        out_specs=[row(SSD_WIDTH), row(SSD_CONV_DIM), per_stream, row(DT_PAD), row(ATT_WIDTH),
                   row(ATT_WIDTH), v_spec, tail, tail],
        out_shape=[jax.ShapeDtypeStruct((rows, SSD_WIDTH), F32),
                   jax.ShapeDtypeStruct((rows, SSD_CONV_DIM), F32),
                   jax.ShapeDtypeStruct((n_streams, 8, SSD_CONV_DIM), F32),
                   jax.ShapeDtypeStruct((rows, DT_PAD), F32),
                   jax.ShapeDtypeStruct((rows, ATT_WIDTH), BF16),
                   jax.ShapeDtypeStruct((rows, ATT_WIDTH), BF16),
                   v_shape,
                   jax.ShapeDtypeStruct((tail_rows, ATT_WIDTH), F32),
                   jax.ShapeDtypeStruct((tail_rows, ATT_WIDTH), F32)],
        scratch_shapes=[pltpu.VMEM((nb * (tr // nb + 8), SSD_CONV_DIM), F32),
                        pltpu.VMEM((ATT_WIDTH, D_MODEL) if v_feature_major else (16, LANES), BF16)],
        compiler_params=pltpu.CompilerParams(dimension_semantics=("arbitrary",),
                                             vmem_limit_bytes=VMEM_LIMIT),
        name="in_proj",
    )(x2d, g_mix, w_zx, w_dt, w_qkv, gq_t, gk_t, cprev8, conv_w, conv_b)


def _ssd_pairs(xc_ref, dtraw_ref, gz_ref, dtb_ref, alog_ref, dskip_ref, expand_ref, ht_ref, ybuf_ref,
               *, t, lb):
    def rows_of(ref, sl):
        v = ref[0, :, sl]
        if lb < t:
            v = jnp.concatenate([v, jnp.zeros((t - lb, v.shape[1]), v.dtype)], axis=0)
        return v

    lane = lax.broadcasted_iota(jnp.int32, (t, LANES), 1)
    rowi = lax.broadcasted_iota(jnp.int32, (t, LANES), 0)
    dt = jax.nn.softplus(rows_of(dtraw_ref, slice(None)) + dtb_ref[...])
    dt = jnp.where((lane < SSD_HEADS) & (rowi < lb), dt, 0.0)
    a_neg = -jnp.exp(alog_ref[...]) * LOG2E
    a = dt * a_neg

    rr = lax.broadcasted_iota(jnp.int32, (t, t), 0)
    cc = lax.broadcasted_iota(jnp.int32, (t, t), 1)
    causal = rr >= cc
    tril = jnp.where(causal, 1.0, 0.0).astype(BF16)
    a1, a2, a3 = _split3(a)
    a_cum = _dot(tril, a1) + _dot(tril, a2) + _dot(tril, a3)
    a_last = a_cum[t - 1:t, :]
    ea = jnp.exp2(a_cum)
    cd = jnp.exp2(a_last)
    a_t = a_cum.T
    w_t = jnp.exp2(a_t[:, t - 1:t] - a_t)

    stacked = jnp.concatenate([ea, dt, jnp.broadcast_to(cd, (8, LANES))], axis=0)
    s_hi, s_lo = _split2(stacked)
    expanded = _dot(s_hi, expand_ref[...]) + _dot(s_lo, expand_ref[...])
    ea_x = expanded[0:t, :]
    dt_x = expanded[t:2 * t, :]
    cd_x = expanded[2 * t:2 * t + 1, :]

    first = lane < SSD_HEAD_DIM
    heads_per_group = SSD_HEADS // SSD_GROUPS
    o_b = SSD_WIDTH
    o_c = SSD_WIDTH + SSD_GROUPS * SSD_STATE
    ssq = [jnp.zeros((lb, 1), F32)]
    per_group = {}

    def group_values(g):
        if g not in per_group:
            bg = rows_of(xc_ref, slice(o_b + g * SSD_STATE, o_b + (g + 1) * SSD_STATE))
            cg = rows_of(xc_ref, slice(o_c + g * SSD_STATE, o_c + (g + 1) * SSD_STATE)).astype(BF16)
            cb_mat = jnp.where(causal, _dot_nt(cg, bg.astype(BF16)), 0.0)
            per_group[g] = (cg, cb_mat, bg.T)
        return per_group[g]

    def prepare(j):
        cg, cb_mat, bg_t = group_values(j // (heads_per_group // 2))
        sl = slice(j * LANES, (j + 1) * LANES)
        xh = rows_of(xc_ref, sl)
        xdt = xh * dt_x[:, sl]
        ops = []
        for hh in range(2):
            h = 2 * j + hh
            xm = jnp.where(first if hh == 0 else jnp.logical_not(first), xdt, 0.0).astype(BF16)
            seg = a_cum[:, h:h + 1] - a_t[h:h + 1, :]
            m = (cb_mat * jnp.exp2(jnp.minimum(seg, 0.0))).astype(BF16)
            bw = (bg_t * w_t[h:h + 1, :]).astype(BF16)
            ops.append((m, bw, xm))
        return cg, xh, ops

    def finish(j, prepared):
        cg, xh, ops = prepared
        sl = slice(j * LANES, (j + 1) * LANES)
        y_pair = _dot(ops[0][0], ops[0][2]) + _dot(ops[1][0], ops[1][2])
        s_pair = _dot(ops[0][1], ops[0][2]) + _dot(ops[1][1], ops[1][2])
        h_in = ht_ref[:, sl]
        y_off = _dot(cg, h_in.astype(BF16)) * ea_x[:, sl]
        ht_ref[:, sl] = cd_x[:, sl] * h_in + s_pair
        y = y_pair + y_off + dskip_ref[:, sl] * xh
        yg = y[0:lb, :] * gz_ref[0, :, sl]
        ybuf_ref[:, sl] = yg
        ssq[0] = ssq[0] + jnp.sum(yg * yg, axis=-1, keepdims=True)

    n_tiles = SSD_WIDTH // LANES
    ahead = prepare(0)
    for j in range(n_tiles):
        cur = ahead
        if j + 1 < n_tiles:
            ahead = prepare(j + 1)
        finish(j, cur)
    return ssq[0]


def _ssd_kernel(xc_ref, dtraw_ref, gz_ref, h0_ref, dtb_ref, alog_ref, dskip_ref, gout_ref, expand_ref,
                y_ref, hfin_ref,
                ht_ref, ybuf_ref, *, t, lb):
    c = pl.program_id(1)
    nc = pl.num_programs(1)

    @pl.when(c == 0)
    def _():
        ht_ref[...] = h0_ref[0].T

    ssq = _ssd_pairs(xc_ref, dtraw_ref, gz_ref, dtb_ref, alog_ref, dskip_ref, expand_ref, ht_ref,
                     ybuf_ref, t=t, lb=lb)
    r = lax.rsqrt(ssq * (1.0 / SSD_WIDTH) + EPS)
    y_ref[0] = (ybuf_ref[...] * r * gout_ref[...]).astype(BF16)

    @pl.when(c == nc - 1)
    def _():
        hfin_ref[0] = ht_ref[...].T


def _ssd_mixer(xc, dtraw, gz, h0, dtb, alog, dskip, gout, expand, *, t, lb):
    b, length, _ = xc.shape
    nc = length // lb
    assert nc * lb == length and (lb == t or nc == 1)

    def seq(width):
        return pl.BlockSpec((1, lb, width), lambda bi, ci: (bi, ci, 0))

    def per_b(shape):
        return pl.BlockSpec((1,) + shape, lambda bi, ci: (bi, 0, 0))

    def const(shape):
        return pl.BlockSpec(shape, lambda bi, ci: (0,) * len(shape))

    hp = SSD_WIDTH
    return pl.pallas_call(
        functools.partial(_ssd_kernel, t=t, lb=lb),
        grid=(b, nc),
        in_specs=[seq(SSD_CONV_DIM), seq(DT_PAD), seq(SSD_WIDTH), per_b((hp, SSD_STATE)),
                  const((1, DT_PAD)), const((1, DT_PAD)), const((1, SSD_WIDTH)), const((1, SSD_WIDTH)),
                  const((LANES, SSD_WIDTH))],
        out_specs=[seq(SSD_WIDTH), per_b((hp, SSD_STATE))],
        out_shape=[jax.ShapeDtypeStruct((b, length, SSD_WIDTH), BF16),
                   jax.ShapeDtypeStruct((b, hp, SSD_STATE), F32)],
        scratch_shapes=[pltpu.VMEM((SSD_STATE, hp), F32),
                        pltpu.VMEM((lb, SSD_WIDTH), F32)],
        compiler_params=pltpu.CompilerParams(dimension_semantics=("arbitrary", "arbitrary"),
                                             vmem_limit_bytes=VMEM_LIMIT),
        name="ssd_mixer",
    )(xc, dtraw, gz, h0, dtb, alog, dskip, gout, expand)


ATT_SB = 2 * CHUNK
ATT_WIN = ATT_SB + PAST


def _toeplitz_rows(base_row, rows, offset, width):
    ext = base_row.shape[1]
    rolled = pltpu.roll(jnp.broadcast_to(base_row, (rows, ext)), 0, 1, stride=1, stride_axis=0)
    return rolled[:, offset:offset + width]


def _band_subblock(q, kw, vt, bias_t, first):
    return _band_softmax_pv(_band_scores(q, kw, bias_t, first), vt)


def _band_scores(q, kw, bias_t, first):
    zero = jnp.zeros_like(q)
    q2 = jnp.concatenate([jnp.where(first, q, zero), jnp.where(first, zero, q)], axis=0)
    return _dot_nt(kw, q2) + bias_t


def _band_softmax(s):
    m = jnp.max(s, axis=0, keepdims=True)
    p = jnp.exp2(s - m)
    return p.astype(BF16), 1.0 / jnp.sum(p, axis=0, keepdims=True)


def _band_pv(p, inv_l, vt):
    sb = p.shape[1] // 2
    half = ATT_HEAD_DIM
    o_t = _dot(vt, p)
    o_pair_t = jnp.concatenate([o_t[0:half, 0:sb] * inv_l[:, 0:sb],
                                o_t[half:2 * half, sb:2 * sb] * inv_l[:, sb:2 * sb]], axis=0)
    return o_pair_t.T.astype(BF16)


def _band_softmax_pv(s, vt):
    p, inv_l = _band_softmax(s)
    return _band_pv(p, inv_l, vt)


def _band_prompt_kernel(q_ref, k_ref, vt_ref, base_ref, o_ref, bias_ref, sc_ref, pb_ref, il_ref):
    length = q_ref.shape[0]
    sb, w = ATT_SB, ATT_WIN
    kj = lax.broadcasted_iota(jnp.int32, (w, sb), 0) // CHUNK
    qi = lax.broadcasted_iota(jnp.int32, (w, sb), 1) // CHUNK
    in_band = (kj >= qi) & (kj <= qi + N_LEFT_CHUNKS)
    for hh in range(2):
        toe = _toeplitz_rows(base_ref[0, hh:hh + 1, :], w, w, sb)
        bias_ref[:, hh * sb:(hh + 1) * sb] = jnp.where(in_band, toe * LOG2E, NEG)

    lane = lax.broadcasted_iota(jnp.int32, (sb, LANES), 1)
    first = lane < ATT_HEAD_DIM
    n_sub = length // sb
    n_head = min(PAST // sb, n_sub)
    for j in range(n_head):
        n = (j + 1) * sb
        o_ref[j * sb:(j + 1) * sb, :] = _band_subblock(
            q_ref[j * sb:(j + 1) * sb, :], k_ref[0:n, :], vt_ref[:, 0:n], bias_ref[w - n:, :], first)

    n_main = n_sub - n_head
    if n_main == 0:
        return
    assert n_main % 2 == 0

    def scores_into(slot, j):
        r0 = pl.multiple_of(j * sb, sb)
        k0 = pl.multiple_of(j * sb - PAST, sb)
        sc_ref[slot] = _band_scores(q_ref[pl.ds(r0, sb), :], k_ref[pl.ds(k0, w), :], bias_ref[...], first)

    def softmax_into(slot):
        p, inv_l = _band_softmax(sc_ref[slot])
        pb_ref[slot] = p
        il_ref[slot] = jnp.broadcast_to(inv_l, (8, 2 * sb))

    def values_out(slot, j):
        k0 = pl.multiple_of(j * sb - PAST, sb)
        return _band_pv(pb_ref[slot], il_ref[slot, 0:1, :], vt_ref[:, pl.ds(k0, w)])

    last = n_sub - 1
    scores_into(0, n_head)
    scores_into(1, n_head + 1)
    softmax_into(0)

    per_trip = 4 if n_main % 4 == 0 else 2

    def body(i, carry):
        outs = []
        for u in range(per_trip):
            j = n_head + per_trip * i + u
            slot = u % 2
            scores_into(slot, jnp.minimum(j + 2, last))
            softmax_into(1 - slot)
            outs.append((j, values_out(slot, j)))
        for j, o in outs:
            o_ref[pl.ds(pl.multiple_of(j * sb, sb), sb), :] = o
        return carry

    lax.fori_loop(0, n_main // per_trip, body, 0)


def _band_prompt(q, k, vt, base):
    length = q.shape[0]
    assert length % ATT_SB == 0
    ext = ATT_SB + ATT_WIN
    col = pl.BlockSpec((length, LANES), lambda c: (0, c))
    return pl.pallas_call(
        _band_prompt_kernel,
        grid=(ATT_WIDTH // LANES,),
        in_specs=[col, col, pl.BlockSpec((LANES, length), lambda c: (c, 0)),
                  pl.BlockSpec((1, 2, ext), lambda c: (c, 0, 0))],
        out_specs=col,
        out_shape=jax.ShapeDtypeStruct((length, ATT_WIDTH), BF16),
        scratch_shapes=[pltpu.VMEM((ATT_WIN, 2 * ATT_SB), F32),
                        pltpu.VMEM((2, ATT_WIN, 2 * ATT_SB), F32),
                        pltpu.VMEM((2, ATT_WIN, 2 * ATT_SB), BF16),
                        pltpu.VMEM((2, 8, 2 * ATT_SB), F32)],
        compiler_params=pltpu.CompilerParams(dimension_semantics=("arbitrary",),
                                             vmem_limit_bytes=VMEM_LIMIT),
        name="band_prompt",
    )(q, k, vt, base.reshape(ATT_HEADS // 2, 2, ext))


def _band_sample_kernel(q_ref, kn_ref, vn_ref, kc_ref, vc_ref, base_ref, o_ref, bias_ref):
    lq = q_ref.shape[0]
    lc = kc_ref.shape[3]

    @pl.when(pl.program_id(0) == 0)
    def _():
        for h in range(ATT_HEADS):
            bias_ref[h] = _toeplitz_rows(base_ref[h:h + 1, :], lq, LANES, lc + LANES) * LOG2E

    for h in range(ATT_HEADS):
        sl = slice(h * ATT_HEAD_DIM, (h + 1) * ATT_HEAD_DIM)
        q = q_ref[:, sl]
        sc = _dot(q, kc_ref[0, h]) + bias_ref[h, :, 0:lc]
        sn = _dot_nt(q, kn_ref[:, sl]) + bias_ref[h, :, lc:lc + lq]
        m = jnp.maximum(jnp.max(sc, axis=-1, keepdims=True), jnp.max(sn, axis=-1, keepdims=True))
        pc = jnp.exp2(sc - m)
        pn = jnp.exp2(sn - m)
        l = jnp.sum(pc, axis=-1, keepdims=True) + jnp.sum(pn, axis=-1, keepdims=True)
        o = _dot_nt(pc.astype(BF16), vc_ref[0, h]) + _dot(pn.astype(BF16), vn_ref[:, sl])
        o_ref[:, sl] = (o * (1.0 / l)).astype(BF16)


def _band_sample(q, kn, vn, kc, vc, base, *, lq):
    rows = q.shape[0]
    nb = rows // lq
    lc = kc.shape[3]
    assert lq <= LANES and lc % LANES == 0 and base.shape == (ATT_HEADS, lc + 2 * LANES)
    blk = pl.BlockSpec((lq, ATT_WIDTH), lambda b: (b, 0))
    cache = pl.BlockSpec((1, ATT_HEADS, ATT_HEAD_DIM, lc), lambda b: (b, 0, 0, 0))
    return pl.pallas_call(
        _band_sample_kernel,
        grid=(nb,),
        in_specs=[blk, blk, blk, cache, cache, pl.BlockSpec(base.shape, lambda b: (0, 0))],
        out_specs=blk,
        out_shape=jax.ShapeDtypeStruct((rows, ATT_WIDTH), BF16),
        scratch_shapes=[pltpu.VMEM((ATT_HEADS, lq, lc + LANES), F32)],
        compiler_params=pltpu.CompilerParams(dimension_semantics=("arbitrary",),
                                             vmem_limit_bytes=VMEM_LIMIT),
        name="band_sample",
    )(q, kn, vn, kc, vc, base)


def _mem_head_norm(y, g_ref, hd, scale):
    blk = y[:, hd * MEM_HEAD_DIM:(hd + 1) * MEM_HEAD_DIM]
    r = lax.rsqrt(jnp.mean(blk * blk, axis=-1, keepdims=True) + EPS)
    return blk * r * (g_ref[...] * scale)


def _memkv_kernel(mem_ref, gsrc_ref, wk_ref, wv_ref, gk_ref, k_ref, v_ref):
    m = _rms(mem_ref[...], gsrc_ref[...]).astype(BF16)
    k = _dot(m, wk_ref[...])
    for hd in range(MEM_HEADS):
        k_ref[:, hd * MEM_HEAD_DIM:(hd + 1) * MEM_HEAD_DIM] = _mem_head_norm(k, gk_ref, hd, 1.0)
    v_ref[...] = _dot(m, wv_ref[...])


def _memory_kv(mem2d, g_src, w_mk, w_mv, g_mk):
    rows = mem2d.shape[0]

    def full(shape):
        return pl.BlockSpec(shape, lambda i: (0,) * len(shape))

    return pl.pallas_call(
        _memkv_kernel,
        grid=(1,),
        in_specs=[full((rows, D_MODEL)), full((1, D_MODEL)), full((D_MODEL, D_MODEL)),
                  full((D_MODEL, D_MODEL)), full((1, MEM_HEAD_DIM))],
        out_specs=[full((rows, D_MODEL)), full((rows, D_MODEL))],
        out_shape=[jax.ShapeDtypeStruct((rows, D_MODEL), F32)] * 2,
        compiler_params=pltpu.CompilerParams(dimension_semantics=("arbitrary",),
                                             vmem_limit_bytes=VMEM_LIMIT),
        name="memory_kv",
    )(mem2d, g_src, w_mk, w_mv, g_mk)


def _outproj_mem_kernel(x_ref, ys_ref, ya_ref, wo_ref, gmx_ref, wq_ref, gmq_ref, mk_ref, mv_ref,
                        wmo_ref, h_ref, obuf_ref, *, nb, rb, dense_rows):
    tr = x_ref.shape[0]
    groups = [slice(r0, r0 + dense_rows) for r0 in range(0, tr, dense_rows)]
    h = [x_ref[g, :] + _dot(ys_ref[g, :], wo_ref[0:SSD_WIDTH, :]) + _dot(ya_ref[g, :], wo_ref[SSD_WIDTH:, :])
         for g in groups]
    hn = [_rms(hg, gmx_ref[...]).astype(BF16) for hg in h]
    q = [_dot(hg, wq_ref[...]) for hg in hn]
    for hd in range(MEM_HEADS):
        sl = slice(hd * MEM_HEAD_DIM, (hd + 1) * MEM_HEAD_DIM)
        qn = [_mem_head_norm(qg, gmq_ref, hd, MEM_HEAD_DIM ** -0.5 * LOG2E).astype(BF16) for qg in q]
        for g, qg in zip(groups, qn):
            for r0 in range(g.start, g.stop, rb):
                b = r0 // rb if nb > 1 else 0
                s = _dot_nt(qg[r0 - g.start:r0 - g.start + min(rb, dense_rows), :], mk_ref[b, :, sl])
                p = jnp.exp2(s - jnp.max(s, axis=-1, keepdims=True))
                o = _dot(p.astype(BF16), mv_ref[b, :, sl]) * (1.0 / jnp.sum(p, axis=-1, keepdims=True))
                obuf_ref[r0:r0 + min(rb, dense_rows), sl] = o.astype(BF16)
    for g, hg in zip(groups, h):
        h_ref[g, :] = hg + _dot(obuf_ref[g, :], wmo_ref[...])


def _outproj_mem(x2d, ys, ya, w_out, g_mem_x, w_mq, g_mq, mk, mv, w_mo, *, tr, seq_len):
    rows = x2d.shape[0]
    n = rows // tr
    nb = max(tr // seq_len, 1)
    tiles_per_stream = max(seq_len // tr, 1)
    rb = tr // nb
    assert n * tr == rows and nb * rb == tr and mk.shape[0] * seq_len == rows

    def row(width):
        return pl.BlockSpec((tr, width), lambda i: (i, 0))

    def const(shape):
        return pl.BlockSpec(shape, lambda i: (0,) * len(shape))

    mem = pl.BlockSpec((nb, N_MEM, D_MODEL), lambda i: (i // tiles_per_stream, 0, 0))
    return pl.pallas_call(
        functools.partial(_outproj_mem_kernel, nb=nb, rb=rb, dense_rows=min(tr, MIX_DENSE_ROWS)),
        grid=(n,),
        in_specs=[row(D_MODEL), row(SSD_WIDTH), row(ATT_WIDTH), const((SSD_WIDTH + ATT_WIDTH, D_MODEL)),
                  const((1, D_MODEL)), const((D_MODEL, D_MODEL)), const((1, MEM_HEAD_DIM)), mem, mem,
                  const((D_MODEL, D_MODEL))],
        out_specs=row(D_MODEL),
        out_shape=jax.ShapeDtypeStruct((rows, D_MODEL), F32),
        scratch_shapes=[pltpu.VMEM((tr, D_MODEL), BF16)],
        compiler_params=pltpu.CompilerParams(dimension_semantics=("arbitrary",),
                                             vmem_limit_bytes=VMEM_LIMIT),
        name="outproj_mem",
    )(x2d, ys, ya, w_out, g_mem_x, w_mq, g_mq, mk, mv, w_mo)


FF_SLAB = 1024


def _ffn_kernel(h_ref, g_ref, w1_ref, w2_ref, y_ref):
    h = h_ref[...]
    hn = _rms(h, g_ref[...]).astype(BF16)
    acc = h
    for s in range(D_FF // FF_SLAB):
        u = jnp.maximum(_dot(hn, w1_ref[:, s * FF_SLAB:(s + 1) * FF_SLAB]), 0.0)
        acc = acc + _dot((u * u).astype(BF16), w2_ref[s * FF_SLAB:(s + 1) * FF_SLAB, :])
    y_ref[...] = acc


def _ffn(h2d, g_ffn, w1, w2, *, tr):
    rows = h2d.shape[0]
    n = rows // tr
    assert n * tr == rows

    def const(shape):
        return pl.BlockSpec(shape, lambda i: (0,) * len(shape))

    row = pl.BlockSpec((tr, D_MODEL), lambda i: (i, 0))
    return pl.pallas_call(
        _ffn_kernel,
        grid=(n,),
        in_specs=[row, const((1, D_MODEL)), const((D_MODEL, D_FF)), const((D_FF, D_MODEL))],
        out_specs=row,
        out_shape=jax.ShapeDtypeStruct((rows, D_MODEL), F32),
        compiler_params=pltpu.CompilerParams(dimension_semantics=("arbitrary",),
                                             vmem_limit_bytes=VMEM_LIMIT),
        name="ffn",
    )(h2d, g_ffn, w1, w2)


def _toeplitz_base(table, offset, width):
    heads, size = table.shape
    n_far = offset - REL_CLIP
    assert n_far >= 0
    parts = [jnp.broadcast_to(table[:, size - 1:], (heads, n_far)), table[:, ::-1]]
    rest = width - n_far - size
    if rest > 0:
        parts.append(jnp.broadcast_to(table[:, :1], (heads, rest)))
    return jnp.concatenate(parts, axis=1)[:, :width].astype(F32)


def _toeplitz_base_t(table, shift, width):
    heads, size = table.shape
    n_low = shift - REL_CLIP
    assert n_low >= 0
    parts = [jnp.broadcast_to(table[:, :1], (heads, n_low)), table]
    rest = width - n_low - size
    if rest > 0:
        parts.append(jnp.broadcast_to(table[:, size - 1:], (heads, rest)))
    return jnp.concatenate(parts, axis=1)[:, :width].astype(F32)


def _prep_weights(g_mix, w_in, conv_w, conv_b, ssd_A_log, ssd_dt_bias, ssd_D, ssd_g_out, att_g_q, att_g_k,
                  w_out, g_mem_x, g_mem_src, w_mq, w_mk, w_mv, g_mq, g_mk, w_mo, g_ffn, w_ff1, w_ff2):
    w_zx, w_dt, w_qkv = _split_w_in(w_in)
    pad_h = lambda v: jnp.pad(v, (0, DT_PAD - SSD_HEADS)).reshape(1, DT_PAD)
    row = lambda v: v.reshape(1, -1)
    expand = (jnp.arange(LANES)[:, None] == jnp.arange(SSD_WIDTH)[None, :] // SSD_HEAD_DIM).astype(BF16)
    return dict(
        g_mix=row(g_mix), w_zx=w_zx, w_dt=w_dt, w_qkv=w_qkv,
        gq_t=row(jnp.tile(att_g_q, ATT_HEADS)), gk_t=row(jnp.tile(att_g_k, ATT_HEADS)),
        conv_w=conv_w, conv_b=row(conv_b), dtb=pad_h(ssd_dt_bias), alog=pad_h(ssd_A_log),
        dskip=row(jnp.repeat(ssd_D, SSD_HEAD_DIM)), gout=row(ssd_g_out), expand=expand,
        w_out=w_out.astype(BF16), g_mem_x=row(g_mem_x), g_mem_src=row(g_mem_src),
        w_mq=w_mq.astype(BF16), w_mk=w_mk.astype(BF16), w_mv=w_mv.astype(BF16),
        g_mq=row(g_mq), g_mk=row(g_mk), w_mo=w_mo.astype(BF16), g_ffn=row(g_ffn),
        w_ff1=w_ff1.astype(BF16), w_ff2=w_ff2.astype(BF16))


def _layer(x, conv_prev, h0, k_cache, v_cache, mem_k, mem_v, p, rel, *, tr, t_scan):
    b, length, _ = x.shape
    rows = b * length
    x2d = x.reshape(rows, D_MODEL)
    prompt = k_cache is None
    tail_rows = PAST if prompt else rows
    cprev8 = jnp.pad(conv_prev, ((0, 0), (8 - (SSD_CONV - 1), 0), (0, 0)))
    tr_in = 2 * tr if prompt and rows % (2 * tr) == 0 and tail_rows % (2 * tr) == 0 else tr
    gz, xc, xtail, dtraw, q, k, v, k_tail, v_tail = _in_proj(
        x2d, p["g_mix"], p["w_zx"], p["w_dt"], p["w_qkv"], p["gq_t"], p["gk_t"], cprev8,
        p["conv_w"], p["conv_b"], tr=tr_in, seq_len=length, tail_rows=tail_rows, v_feature_major=prompt)
    conv_new = xtail[:, 8 - (SSD_CONV - 1):]

    scan_args = (xc.reshape(b, length, SSD_CONV_DIM), dtraw.reshape(b, length, DT_PAD),
                 gz.reshape(b, length, SSD_WIDTH), h0.reshape(b, SSD_WIDTH, SSD_STATE),
                 p["dtb"], p["alog"], p["dskip"], p["gout"], p["expand"])
    mix_args = (p["w_out"], p["g_mem_x"], p["w_mq"], p["g_mq"], mem_k.astype(BF16), mem_v.astype(BF16),
                p["w_mo"])
    y_ssd, h_fin = _ssd_mixer(*scan_args, t=t_scan, lb=t_scan if prompt else length)
    if prompt:
        y_att = _band_prompt(q, k, v, _toeplitz_base_t(rel, ATT_SB, ATT_SB + ATT_WIN))
        k_rows = k_tail.reshape(b, PAST, ATT_HEADS, ATT_HEAD_DIM)
        v_rows = v_tail.reshape(b, PAST, ATT_HEADS, ATT_HEAD_DIM)
    else:
        lc = k_cache.shape[1]
        y_att = _band_sample(q, k, v, jnp.transpose(k_cache, (0, 2, 3, 1)).astype(BF16),
                             jnp.transpose(v_cache, (0, 2, 3, 1)).astype(BF16),
                             _toeplitz_base(rel, lc + LANES, lc + 2 * LANES), lq=length)
        k_rows = k_tail.reshape(b, length, ATT_HEADS, ATT_HEAD_DIM)
        v_rows = v_tail.reshape(b, length, ATT_HEADS, ATT_HEAD_DIM)
    tr_mix = 2 * tr if prompt and rows % (2 * tr) == 0 else tr
    h = _outproj_mem(x2d, y_ssd.reshape(rows, SSD_WIDTH), y_att, *mix_args, tr=tr_mix, seq_len=length)
    y = _ffn(h, p["g_ffn"], p["w_ff1"], p["w_ff2"], tr=tr)
    return (y.reshape(b, length, D_MODEL), h_fin.reshape(b, SSD_HEADS, SSD_HEAD_DIM, SSD_STATE),
            conv_new, k_rows, v_rows)


def kernel(x_prompt, x_sample, mem_prompt, state_ssd, state_conv, cache_attn_k, cache_attn_v, cache_mem_k,
           cache_mem_v, g_mix, w_in, conv_w, conv_b, ssd_A_log, ssd_dt_bias, ssd_D, ssd_g_out, att_g_q,
           att_g_k, att_rel_bias, w_out, g_mem_x, g_mem_src, w_mq, w_mk, w_mv, g_mq, g_mk, w_mo, g_ffn,
           w_ff1, w_ff2):
    depth = g_mix.shape[0]
    b_p, seq, _ = x_prompt.shape
    b_s, dec_seq, _ = x_sample.shape
    yp, ys = x_prompt, x_sample
    outs = [[] for _ in range(10)]
    for l in range(depth):
        p = _prep_weights(g_mix[l], w_in[l], conv_w[l], conv_b[l], ssd_A_log[l], ssd_dt_bias[l], ssd_D[l],
                          ssd_g_out[l], att_g_q[l], att_g_k[l], w_out[l], g_mem_x[l], g_mem_src[l],
                          w_mq[l], w_mk[l], w_mv[l], g_mq[l], g_mk[l], w_mo[l], g_ffn[l], w_ff1[l], w_ff2[l])
        rel = att_rel_bias[l]
        mk, mv = _memory_kv(mem_prompt.reshape(b_p * N_MEM, D_MODEL), p["g_mem_src"], p["w_mk"], p["w_mv"],
                            p["g_mk"])
        mk = mk.reshape(b_p, N_MEM, D_MODEL)
        mv = mv.reshape(b_p, N_MEM, D_MODEL)
        conv0 = jnp.zeros((b_p, SSD_CONV - 1, SSD_CONV_DIM), F32)
        h00 = jnp.zeros((b_p, SSD_HEADS, SSD_HEAD_DIM, SSD_STATE), F32)
        yp, hp, cp, kp, vp = _layer(yp, conv0, h00, None, None, mk, mv, p, rel,
                                    tr=min(256, seq), t_scan=min(256, seq))
        ys, hs, cs, ks_, vs_ = _layer(ys, state_conv[l], state_ssd[l], cache_attn_k[l], cache_attn_v[l],
                                      cache_mem_k[l].reshape(b_s, N_MEM, D_MODEL),
                                      cache_mem_v[l].reshape(b_s, N_MEM, D_MODEL), p, rel,
                                      tr=b_s * dec_seq, t_scan=128)
        for lst, val in zip(outs, (hp, cp, kp, vp,
                                   mk.reshape(b_p, N_MEM, MEM_HEADS, MEM_HEAD_DIM),
                                   mv.reshape(b_p, N_MEM, MEM_HEADS, MEM_HEAD_DIM),
                                   hs, cs, ks_, vs_)):
            lst.append(val)
    return (yp, ys) + tuple(jnp.stack(o) for o in outs)
```

```python
import functools

import jax
import jax.numpy as jnp
from jax import lax
from jax.experimental import pallas as pl
from jax.experimental.pallas import tpu as pltpu

F32 = jnp.float32
BF16 = jnp.bfloat16

D_MODEL = 1024
CHUNK = 64
SSD_HEADS = 16
SSD_HEAD_DIM = 64
SSD_WIDTH = SSD_HEADS * SSD_HEAD_DIM
SSD_GROUPS = 2
SSD_STATE = 128
SSD_CONV = 4
SSD_CONV_DIM = SSD_WIDTH + 2 * SSD_GROUPS * SSD_STATE
ATT_HEADS = 16
ATT_HEAD_DIM = 64
ATT_WIDTH = ATT_HEADS * ATT_HEAD_DIM
N_LEFT_CHUNKS = 8
PAST = N_LEFT_CHUNKS * CHUNK
REL_CLIP = 128
N_MEM = 256
MEM_HEADS = 4
MEM_HEAD_DIM = D_MODEL // MEM_HEADS
D_FF = 4 * D_MODEL
EPS = 1e-6

LANES = 128
DT_PAD = LANES
NEG = -1e30
LOG2E = 1.4426950408889634
CONV_ROWS = 64
MIX_DENSE_ROWS = 256
IN_GROUP_ROWS = 256
VMEM_LIMIT = 56 * 1024 * 1024


def _rms(x, g):
    return x * lax.rsqrt(jnp.mean(x * x, axis=-1, keepdims=True) + EPS) * g


def _silu(x):
    h = 0.5 * x
    return h + h * jnp.tanh(h)


def _split2(x):
    hi = x.astype(BF16)
    lo = (x - hi.astype(F32)).astype(BF16)
    return hi, lo


def _split3(x):
    hi = x.astype(BF16)
    r = x - hi.astype(F32)
    mid = r.astype(BF16)
    lo = (r - mid.astype(F32)).astype(BF16)
    return hi, mid, lo


def _resident(shape):
    return pl.BlockSpec(shape, lambda *_: (0,) * len(shape), pipeline_mode=pl.Buffered(1))


def _dot(a, b):
    return jnp.dot(a, b, preferred_element_type=F32)


def _dot_nt(a, b):
    return lax.dot_general(a, b, (((1,), (1,)), ((), ())), preferred_element_type=F32)


W_PREP_ROWS = 128


def _split_win_kernel(w_ref, wzx_ref, wdt_ref, wqkv_ref):
    o_dt = SSD_WIDTH + SSD_CONV_DIM
    o_qkv = o_dt + SSD_HEADS
    wzx_ref[...] = w_ref[:, 0:o_dt].astype(BF16)
    lane = lax.broadcasted_iota(jnp.int32, (w_ref.shape[0], DT_PAD), 1)
    wdt_ref[...] = jnp.where(lane < SSD_HEADS, w_ref[:, o_dt:o_dt + DT_PAD], 0.0).astype(BF16)
    wqkv_ref[...] = w_ref[:, o_qkv:o_qkv + 3 * ATT_WIDTH].astype(BF16)


def _split_w_in(w_in):
    d, width = w_in.shape
    n = d // W_PREP_ROWS
    widths = (SSD_WIDTH + SSD_CONV_DIM, DT_PAD, 3 * ATT_WIDTH)
    return pl.pallas_call(
        _split_win_kernel,
        grid=(n,),
        in_specs=[pl.BlockSpec((W_PREP_ROWS, width), lambda i: (i, 0))],
        out_specs=[pl.BlockSpec((W_PREP_ROWS, w), lambda i: (i, 0)) for w in widths],
        out_shape=[jax.ShapeDtypeStruct((d, w), BF16) for w in widths],
        compiler_params=pltpu.CompilerParams(dimension_semantics=("arbitrary",),
                                             vmem_limit_bytes=VMEM_LIMIT),
        name="split_w_in",
    )(w_in)


def _inproj_kernel(x_ref, gmix_ref, wzx_ref, wdt_ref, wqkv_ref, gq_ref, gk_ref, cprev_ref,
                   convw_ref, convb_ref,
                   gz_ref, xc_ref, xtail_ref, dt_ref, q_ref, k_ref, v_ref, kt_ref, vt_ref, ext_ref, wvt_ref,
                   *, n_tail, v_feature_major, nb, tiles_per_stream):
    i = pl.program_id(0)
    n = pl.num_programs(0)
    tr = x_ref.shape[0]
    rb = tr // nb
    halo = 8
    if v_feature_major:
        @pl.when(i == 0)
        def _():
            wvt_ref[...] = wqkv_ref[:, 2 * ATT_WIDTH:].T
    grp = min(tr, IN_GROUP_ROWS)
    groups = [slice(r0, r0 + grp) for r0 in range(0, tr, grp)]
    xn = [_rms(x_ref[g, :], gmix_ref[...]).astype(BF16) for g in groups]

    def proj(gi, loc, width):
        w_ref, lo = loc
        return _dot(xn[gi], w_ref[:, lo:lo + width])

    o_xbc = (wzx_ref, SSD_WIDTH)
    o_dt = (wdt_ref, 0)
    o_q = (wqkv_ref, 0)
    o_k = (wqkv_ref, ATT_WIDTH)
    o_v = (wqkv_ref, 2 * ATT_WIDTH)
    lane = lax.broadcasted_iota(jnp.int32, (grp, LANES), 1)
    first = lane < ATT_HEAD_DIM

    def head_norm(blk, g_ref, c, scale):
        sq = blk * blk
        s0 = jnp.sum(jnp.where(first, sq, 0.0), axis=-1, keepdims=True)
        s1 = jnp.sum(jnp.where(first, 0.0, sq), axis=-1, keepdims=True)
        r = jnp.where(first, lax.rsqrt(s0 * (1.0 / ATT_HEAD_DIM) + EPS),
                      lax.rsqrt(s1 * (1.0 / ATT_HEAD_DIM) + EPS))
        return blk * r * (g_ref[:, c * LANES:(c + 1) * LANES] * scale)

    chunk = 2 * LANES
    work = []

    def run(count):
        for _ in range(min(count, len(work))):
            work.pop(0)()

    xbc = [proj(gi, o_xbc, SSD_CONV_DIM) for gi in range(len(groups))]

    def xbc_rows(r0, r1):
        gi = r0 // grp
        return xbc[gi][r0 - gi * grp:r1 - gi * grp, :]

    for s in range(nb):
        base = s * (rb + halo)
        if tiles_per_stream == 1:
            carried = cprev_ref[s]
        else:
            carried = jnp.where(i % tiles_per_stream == 0, cprev_ref[s], ext_ref[rb:rb + halo, :])
        ext_ref[base:base + halo, :] = carried
        piece = min(rb, grp)
        for r0 in range(0, rb, piece):
            ext_ref[base + halo + r0:base + halo + r0 + piece, :] = xbc_rows(s * rb + r0, s * rb + r0 + piece)
        xtail_ref[s] = xbc_rows((s + 1) * rb - halo, (s + 1) * rb)

        def conv_piece(s=s, base=base, cb=0, r0=0, nr=rb):
            sl = slice(cb * LANES, (cb + 1) * LANES)
            xe = ext_ref[base + r0:base + r0 + halo + nr, sl]
            acc = convb_ref[:, sl] + convw_ref[SSD_CONV - 1:SSD_CONV, sl] * xe[halo:, :]
            for j in range(1, SSD_CONV):
                tap = pltpu.roll(xe, j, 0)[halo:, :]
                acc = acc + convw_ref[SSD_CONV - 1 - j:SSD_CONV - j, sl] * tap
            xc_ref[s * rb + r0:s * rb + r0 + nr, sl] = _silu(acc)

        nr = min(rb, CONV_ROWS)
        for cb in range(SSD_CONV_DIM // LANES):
            for r0 in range(0, rb, nr):
                work.append(functools.partial(conv_piece, cb=cb, r0=r0, nr=nr))

    def gate(zc, c, g):
        gz_ref[g, c * LANES:(c + 1) * LANES] = _silu(zc)

    def norm_q(blk, c, g):
        q_ref[g, c * LANES:(c + 1) * LANES] = head_norm(
            blk, gq_ref, c, ATT_HEAD_DIM ** -0.5 * LOG2E).astype(BF16)

    def norm_k(blk, c, g):
        kn = head_norm(blk, gk_ref, c, 1.0)
        k_ref[g, c * LANES:(c + 1) * LANES] = kn.astype(BF16)
        kt_ref[g, c * LANES:(c + 1) * LANES] = kn

    def chunks(loc, width, consumer, per_chunk):
        w_ref, lo = loc
        for c0 in range(0, width, chunk):
            for gi, g in enumerate(groups):
                y = proj(gi, (w_ref, lo + c0), chunk)
                for t in range(chunk // LANES):
                    work.append(functools.partial(consumer, y[:, t * LANES:(t + 1) * LANES], c0 // LANES + t, g))
                run(per_chunk)

    per_chunk = -(-len(work) // (12 * len(groups))) + 2
    chunks((wzx_ref, 0), SSD_WIDTH, gate, per_chunk)
    chunks(o_q, ATT_WIDTH, norm_q, per_chunk)
    chunks(o_k, ATT_WIDTH, norm_k, per_chunk)
    for c0 in range(0, ATT_WIDTH, chunk):
        for gi, g in enumerate(groups):
            if v_feature_major:
                v_ref[c0:c0 + chunk, g] = _dot_nt(wvt_ref[c0:c0 + chunk, :], xn[gi]).astype(BF16)
            else:
                v = proj(gi, (wqkv_ref, 2 * ATT_WIDTH + c0), chunk)
                v_ref[g, c0:c0 + chunk] = v.astype(BF16)
                vt_ref[g, c0:c0 + chunk] = v
            run(3)
    run(len(work))
    for gi, g in enumerate(groups):
        dt_ref[g, :] = proj(gi, o_dt, DT_PAD)
    if v_feature_major:
        @pl.when(i >= n - n_tail)
        def _():
            for gi, g in enumerate(groups):
                vt_ref[g, :] = proj(gi, o_v, ATT_WIDTH)


def _in_proj(x2d, g_mix, w_zx, w_dt, w_qkv, gq_t, gk_t, cprev8, conv_w, conv_b, *, tr, seq_len,
             tail_rows, v_feature_major):
    rows = x2d.shape[0]
    n = rows // tr
    n_tail = tail_rows // tr
    nb = max(tr // seq_len, 1)
    tiles_per_stream = max(seq_len // tr, 1)
    n_streams = rows // seq_len
    assert n * tr == rows and n_tail * tr == tail_rows and cprev8.shape == (n_streams, 8, SSD_CONV_DIM)
    assert (tr // nb) % 8 == 0

    def row(width):
        return pl.BlockSpec((tr, width), lambda i: (i, 0))

    def const(shape):
        return pl.BlockSpec(shape, lambda i: (0,) * len(shape))

    per_stream = pl.BlockSpec((nb, 8, SSD_CONV_DIM), lambda i: (i // tiles_per_stream, 0, 0))
    tail = pl.BlockSpec((tr, ATT_WIDTH), lambda i: (jnp.maximum(i - (n - n_tail), 0), 0))
    if v_feature_major:
        v_spec = pl.BlockSpec((ATT_WIDTH, tr), lambda i: (0, i))
        v_shape = jax.ShapeDtypeStruct((ATT_WIDTH, rows), BF16)
    else:
        v_spec = row(ATT_WIDTH)
        v_shape = jax.ShapeDtypeStruct((rows, ATT_WIDTH), BF16)
    return pl.pallas_call(
        functools.partial(_inproj_kernel, n_tail=n_tail, v_feature_major=v_feature_major, nb=nb,
                          tiles_per_stream=tiles_per_stream),
        grid=(n,),
        in_specs=[row(D_MODEL), const((1, D_MODEL)), _resident((D_MODEL, SSD_WIDTH + SSD_CONV_DIM)),
                  _resident((D_MODEL, DT_PAD)), _resident((D_MODEL, 3 * ATT_WIDTH)),
                  const((1, ATT_WIDTH)), const((1, ATT_WIDTH)),
                  per_stream, const((SSD_CONV, SSD_CONV_DIM)), const((1, SSD_CONV_DIM))],
        out_specs=[row(SSD_WIDTH), row(SSD_CONV_DIM), per_stream, row(DT_PAD), row(ATT_WIDTH),
                   row(ATT_WIDTH), v_spec, tail, tail],
        out_shape=[jax.ShapeDtypeStruct((rows, SSD_WIDTH), F32),
                   jax.ShapeDtypeStruct((rows, SSD_CONV_DIM), F32),
                   jax.ShapeDtypeStruct((n_streams, 8, SSD_CONV_DIM), F32),
                   jax.ShapeDtypeStruct((rows, DT_PAD), F32),
                   jax.ShapeDtypeStruct((rows, ATT_WIDTH), BF16),
                   jax.ShapeDtypeStruct((rows, ATT_WIDTH), BF16),
                   v_shape,
                   jax.ShapeDtypeStruct((tail_rows, ATT_WIDTH), F32),
                   jax.ShapeDtypeStruct((tail_rows, ATT_WIDTH), F32)],
        scratch_shapes=[pltpu.VMEM((nb * (tr // nb + 8), SSD_CONV_DIM), F32),
                        pltpu.VMEM((ATT_WIDTH, D_MODEL) if v_feature_major else (16, LANES), BF16)],
        compiler_params=pltpu.CompilerParams(dimension_semantics=("arbitrary",),
                                             vmem_limit_bytes=VMEM_LIMIT),
        name="in_proj",
    )(x2d, g_mix, w_zx, w_dt, w_qkv, gq_t, gk_t, cprev8, conv_w, conv_b)


def _ssd_pairs(xc_ref, dtraw_ref, gz_ref, dtb_ref, alog_ref, dskip_ref, expand_ref, ht_ref, ybuf_ref,
               *, t, lb):
    def rows_of(ref, sl):
        v = ref[0, :, sl]
        if lb < t:
            v = jnp.concatenate([v, jnp.zeros((t - lb, v.shape[1]), v.dtype)], axis=0)
        return v

    lane = lax.broadcasted_iota(jnp.int32, (t, LANES), 1)
    rowi = lax.broadcasted_iota(jnp.int32, (t, LANES), 0)
    dt = jax.nn.softplus(rows_of(dtraw_ref, slice(None)) + dtb_ref[...])
    dt = jnp.where((lane < SSD_HEADS) & (rowi < lb), dt, 0.0)
    a_neg = -jnp.exp(alog_ref[...]) * LOG2E
    a = dt * a_neg

    rr = lax.broadcasted_iota(jnp.int32, (t, t), 0)
    cc = lax.broadcasted_iota(jnp.int32, (t, t), 1)
    causal = rr >= cc
    tril = jnp.where(causal, 1.0, 0.0).astype(BF16)
    a1, a2, a3 = _split3(a)
    a_cum = _dot(tril, a1) + _dot(tril, a2) + _dot(tril, a3)
    a_last = a_cum[t - 1:t, :]
    ea = jnp.exp2(a_cum)
    cd = jnp.exp2(a_last)
    a_t = a_cum.T
    w_t = jnp.exp2(a_t[:, t - 1:t] - a_t)

    stacked = jnp.concatenate([ea, dt, jnp.broadcast_to(cd, (8, LANES))], axis=0)
    s_hi, s_lo = _split2(stacked)
    expanded = _dot(s_hi, expand_ref[...]) + _dot(s_lo, expand_ref[...])
    ea_x = expanded[0:t, :]
    dt_x = expanded[t:2 * t, :]
    cd_x = expanded[2 * t:2 * t + 1, :]

    first = lane < SSD_HEAD_DIM
    heads_per_group = SSD_HEADS // SSD_GROUPS
    o_b = SSD_WIDTH
    o_c = SSD_WIDTH + SSD_GROUPS * SSD_STATE
    ssq = [jnp.zeros((lb, 1), F32)]
    per_group = {}

    def group_values(g):
        if g not in per_group:
            bg = rows_of(xc_ref, slice(o_b + g * SSD_STATE, o_b + (g + 1) * SSD_STATE))
            cg = rows_of(xc_ref, slice(o_c + g * SSD_STATE, o_c + (g + 1) * SSD_STATE)).astype(BF16)
            cb_mat = jnp.where(causal, _dot_nt(cg, bg.astype(BF16)), 0.0)
            per_group[g] = (cg, cb_mat, bg.T)
        return per_group[g]

    def prepare(j):
        cg, cb_mat, bg_t = group_values(j // (heads_per_group // 2))
        sl = slice(j * LANES, (j + 1) * LANES)
        xh = rows_of(xc_ref, sl)
        xdt = xh * dt_x[:, sl]
        ops = []
        for hh in range(2):
            h = 2 * j + hh
            xm = jnp.where(first if hh == 0 else jnp.logical_not(first), xdt, 0.0).astype(BF16)
            seg = a_cum[:, h:h + 1] - a_t[h:h + 1, :]
            m = (cb_mat * jnp.exp2(jnp.minimum(seg, 0.0))).astype(BF16)
            bw = (bg_t * w_t[h:h + 1, :]).astype(BF16)
            ops.append((m, bw, xm))
        return cg, xh, ops

    def finish(j, prepared):
        cg, xh, ops = prepared
        sl = slice(j * LANES, (j + 1) * LANES)
        y_pair = _dot(ops[0][0], ops[0][2]) + _dot(ops[1][0], ops[1][2])
        s_pair = _dot(ops[0][1], ops[0][2]) + _dot(ops[1][1], ops[1][2])
        h_in = ht_ref[:, sl]
        y_off = _dot(cg, h_in.astype(BF16)) * ea_x[:, sl]
        ht_ref[:, sl] = cd_x[:, sl] * h_in + s_pair
        y = y_pair + y_off + dskip_ref[:, sl] * xh
        yg = y[0:lb, :] * gz_ref[0, :, sl]
        ybuf_ref[:, sl] = yg
        ssq[0] = ssq[0] + jnp.sum(yg * yg, axis=-1, keepdims=True)

    n_tiles = SSD_WIDTH // LANES
    ahead = prepare(0)
    for j in range(n_tiles):
        cur = ahead
        if j + 1 < n_tiles:
            ahead = prepare(j + 1)
        finish(j, cur)
    return ssq[0]


def _ssd_kernel(xc_ref, dtraw_ref, gz_ref, h0_ref, dtb_ref, alog_ref, dskip_ref, gout_ref, expand_ref,
                y_ref, hfin_ref,
                ht_ref, ybuf_ref, *, t, lb):
    c = pl.program_id(1)
    nc = pl.num_programs(1)

    @pl.when(c == 0)
    def _():
        ht_ref[...] = h0_ref[0].T

    ssq = _ssd_pairs(xc_ref, dtraw_ref, gz_ref, dtb_ref, alog_ref, dskip_ref, expand_ref, ht_ref,
                     ybuf_ref, t=t, lb=lb)
    r = lax.rsqrt(ssq * (1.0 / SSD_WIDTH) + EPS)
    y_ref[0] = (ybuf_ref[...] * r * gout_ref[...]).astype(BF16)

    @pl.when(c == nc - 1)
    def _():
        hfin_ref[0] = ht_ref[...].T


def _ssd_mixer(xc, dtraw, gz, h0, dtb, alog, dskip, gout, expand, *, t, lb):
    b, length, _ = xc.shape
    nc = length // lb
    assert nc * lb == length and (lb == t or nc == 1)

    def seq(width):
        return pl.BlockSpec((1, lb, width), lambda bi, ci: (bi, ci, 0))

    def per_b(shape):
        return pl.BlockSpec((1,) + shape, lambda bi, ci: (bi, 0, 0))

    def const(shape):
        return pl.BlockSpec(shape, lambda bi, ci: (0,) * len(shape))

    hp = SSD_WIDTH
    return pl.pallas_call(
        functools.partial(_ssd_kernel, t=t, lb=lb),
        grid=(b, nc),
        in_specs=[seq(SSD_CONV_DIM), seq(DT_PAD), seq(SSD_WIDTH), per_b((hp, SSD_STATE)),
                  const((1, DT_PAD)), const((1, DT_PAD)), const((1, SSD_WIDTH)), const((1, SSD_WIDTH)),
                  const((LANES, SSD_WIDTH))],
        out_specs=[seq(SSD_WIDTH), per_b((hp, SSD_STATE))],
        out_shape=[jax.ShapeDtypeStruct((b, length, SSD_WIDTH), BF16),
                   jax.ShapeDtypeStruct((b, hp, SSD_STATE), F32)],
        scratch_shapes=[pltpu.VMEM((SSD_STATE, hp), F32),
                        pltpu.VMEM((lb, SSD_WIDTH), F32)],
        compiler_params=pltpu.CompilerParams(dimension_semantics=("arbitrary", "arbitrary"),
                                             vmem_limit_bytes=VMEM_LIMIT),
        name="ssd_mixer",
    )(xc, dtraw, gz, h0, dtb, alog, dskip, gout, expand)


ATT_SB = 2 * CHUNK
ATT_WIN = ATT_SB + PAST


def _toeplitz_rows(base_row, rows, offset, width):
    ext = base_row.shape[1]
    rolled = pltpu.roll(jnp.broadcast_to(base_row, (rows, ext)), 0, 1, stride=1, stride_axis=0)
    return rolled[:, offset:offset + width]


def _band_subblock(q, kw, vt, bias_t, first):
    return _band_softmax_pv(_band_scores(q, kw, bias_t, first), vt)


def _band_scores(q, kw, bias_t, first):
    zero = jnp.zeros_like(q)
    q2 = jnp.concatenate([jnp.where(first, q, zero), jnp.where(first, zero, q)], axis=0)
    return _dot_nt(kw, q2) + bias_t


def _band_softmax(s):
    m = jnp.max(s, axis=0, keepdims=True)
    p = jnp.exp2(s - m)
    return p.astype(BF16), 1.0 / jnp.sum(p, axis=0, keepdims=True)


def _band_pv(p, inv_l, vt):
    sb = p.shape[1] // 2
    half = ATT_HEAD_DIM
    o_t = _dot(vt, p)
    o_pair_t = jnp.concatenate([o_t[0:half, 0:sb] * inv_l[:, 0:sb],
                                o_t[half:2 * half, sb:2 * sb] * inv_l[:, sb:2 * sb]], axis=0)
    return o_pair_t.T.astype(BF16)


def _band_softmax_pv(s, vt):
    p, inv_l = _band_softmax(s)
    return _band_pv(p, inv_l, vt)


def _band_prompt_kernel(q_ref, k_ref, vt_ref, base_ref, o_ref, bias_ref, sc_ref, pb_ref, il_ref):
    length = q_ref.shape[0]
    sb, w = ATT_SB, ATT_WIN
    kj = lax.broadcasted_iota(jnp.int32, (w, sb), 0) // CHUNK
    qi = lax.broadcasted_iota(jnp.int32, (w, sb), 1) // CHUNK
    in_band = (kj >= qi) & (kj <= qi + N_LEFT_CHUNKS)
    for hh in range(2):
        toe = _toeplitz_rows(base_ref[0, hh:hh + 1, :], w, w, sb)
        bias_ref[:, hh * sb:(hh + 1) * sb] = jnp.where(in_band, toe * LOG2E, NEG)

    lane = lax.broadcasted_iota(jnp.int32, (sb, LANES), 1)
    first = lane < ATT_HEAD_DIM
    n_sub = length // sb
    n_head = min(PAST // sb, n_sub)
    for j in range(n_head):
        n = (j + 1) * sb
        o_ref[j * sb:(j + 1) * sb, :] = _band_subblock(
            q_ref[j * sb:(j + 1) * sb, :], k_ref[0:n, :], vt_ref[:, 0:n], bias_ref[w - n:, :], first)

    n_main = n_sub - n_head
    if n_main == 0:
        return
    assert n_main % 2 == 0

    def scores_into(slot, j):
        r0 = pl.multiple_of(j * sb, sb)
        k0 = pl.multiple_of(j * sb - PAST, sb)
        sc_ref[slot] = _band_scores(q_ref[pl.ds(r0, sb), :], k_ref[pl.ds(k0, w), :], bias_ref[...], first)

    def softmax_into(slot):
        p, inv_l = _band_softmax(sc_ref[slot])
        pb_ref[slot] = p
        il_ref[slot] = jnp.broadcast_to(inv_l, (8, 2 * sb))

    def values_out(slot, j):
        k0 = pl.multiple_of(j * sb - PAST, sb)
        return _band_pv(pb_ref[slot], il_ref[slot, 0:1, :], vt_ref[:, pl.ds(k0, w)])

    last = n_sub - 1
    scores_into(0, n_head)
    scores_into(1, n_head + 1)
    softmax_into(0)

    per_trip = 4 if n_main % 4 == 0 else 2

    def body(i, carry):
        outs = []
        for u in range(per_trip):
            j = n_head + per_trip * i + u
            slot = u % 2
            scores_into(slot, jnp.minimum(j + 2, last))
            softmax_into(1 - slot)
            outs.append((j, values_out(slot, j)))
        for j, o in outs:
            o_ref[pl.ds(pl.multiple_of(j * sb, sb), sb), :] = o
        return carry

    lax.fori_loop(0, n_main // per_trip, body, 0)


def _band_prompt(q, k, vt, base):
    length = q.shape[0]
    assert length % ATT_SB == 0
    ext = ATT_SB + ATT_WIN
    col = pl.BlockSpec((length, LANES), lambda c: (0, c))
    return pl.pallas_call(
        _band_prompt_kernel,
        grid=(ATT_WIDTH // LANES,),
        in_specs=[col, col, pl.BlockSpec((LANES, length), lambda c: (c, 0)),
                  pl.BlockSpec((1, 2, ext), lambda c: (c, 0, 0))],
        out_specs=col,
        out_shape=jax.ShapeDtypeStruct((length, ATT_WIDTH), BF16),
        scratch_shapes=[pltpu.VMEM((ATT_WIN, 2 * ATT_SB), F32),
                        pltpu.VMEM((2, ATT_WIN, 2 * ATT_SB), F32),
                        pltpu.VMEM((2, ATT_WIN, 2 * ATT_SB), BF16),
                        pltpu.VMEM((2, 8, 2 * ATT_SB), F32)],
        compiler_params=pltpu.CompilerParams(dimension_semantics=("arbitrary",),
                                             vmem_limit_bytes=VMEM_LIMIT),
        name="band_prompt",
    )(q, k, vt, base.reshape(ATT_HEADS // 2, 2, ext))


def _band_sample_kernel(q_ref, kn_ref, vn_ref, kc_ref, vc_ref, base_ref, o_ref, bias_ref):
    lq = q_ref.shape[0]
    lc = kc_ref.shape[3]

    @pl.when(pl.program_id(0) == 0)
    def _():
        for h in range(ATT_HEADS):
            bias_ref[h] = _toeplitz_rows(base_ref[h:h + 1, :], lq, LANES, lc + LANES) * LOG2E

    for h in range(ATT_HEADS):
        sl = slice(h * ATT_HEAD_DIM, (h + 1) * ATT_HEAD_DIM)
        q = q_ref[:, sl]
        sc = _dot(q, kc_ref[0, h]) + bias_ref[h, :, 0:lc]
        sn = _dot_nt(q, kn_ref[:, sl]) + bias_ref[h, :, lc:lc + lq]
        m = jnp.maximum(jnp.max(sc, axis=-1, keepdims=True), jnp.max(sn, axis=-1, keepdims=True))
        pc = jnp.exp2(sc - m)
        pn = jnp.exp2(sn - m)
        l = jnp.sum(pc, axis=-1, keepdims=True) + jnp.sum(pn, axis=-1, keepdims=True)
        o = _dot_nt(pc.astype(BF16), vc_ref[0, h]) + _dot(pn.astype(BF16), vn_ref[:, sl])
        o_ref[:, sl] = (o * (1.0 / l)).astype(BF16)


def _band_sample(q, kn, vn, kc, vc, base, *, lq):
    rows = q.shape[0]
    nb = rows // lq
    lc = kc.shape[3]
    assert lq <= LANES and lc % LANES == 0 and base.shape == (ATT_HEADS, lc + 2 * LANES)
    blk = pl.BlockSpec((lq, ATT_WIDTH), lambda b: (b, 0))
    cache = pl.BlockSpec((1, ATT_HEADS, ATT_HEAD_DIM, lc), lambda b: (b, 0, 0, 0))
    return pl.pallas_call(
        _band_sample_kernel,
        grid=(nb,),
        in_specs=[blk, blk, blk, cache, cache, pl.BlockSpec(base.shape, lambda b: (0, 0))],
        out_specs=blk,
        out_shape=jax.ShapeDtypeStruct((rows, ATT_WIDTH), BF16),
        scratch_shapes=[pltpu.VMEM((ATT_HEADS, lq, lc + LANES), F32)],
        compiler_params=pltpu.CompilerParams(dimension_semantics=("arbitrary",),
                                             vmem_limit_bytes=VMEM_LIMIT),
        name="band_sample",
    )(q, kn, vn, kc, vc, base)


def _mem_head_norm(y, g_ref, hd, scale):
    blk = y[:, hd * MEM_HEAD_DIM:(hd + 1) * MEM_HEAD_DIM]
    r = lax.rsqrt(jnp.mean(blk * blk, axis=-1, keepdims=True) + EPS)
    return blk * r * (g_ref[...] * scale)


def _memkv_kernel(mem_ref, gsrc_ref, wk_ref, wv_ref, gk_ref, k_ref, v_ref):
    m = _rms(mem_ref[...], gsrc_ref[...]).astype(BF16)
    k = _dot(m, wk_ref[...])
    for hd in range(MEM_HEADS):
        k_ref[:, hd * MEM_HEAD_DIM:(hd + 1) * MEM_HEAD_DIM] = _mem_head_norm(k, gk_ref, hd, 1.0)
    v_ref[...] = _dot(m, wv_ref[...])


def _memory_kv(mem2d, g_src, w_mk, w_mv, g_mk):
    rows = mem2d.shape[0]

    def full(shape):
        return pl.BlockSpec(shape, lambda i: (0,) * len(shape))

    return pl.pallas_call(
        _memkv_kernel,
        grid=(1,),
        in_specs=[full((rows, D_MODEL)), full((1, D_MODEL)), full((D_MODEL, D_MODEL)),
                  full((D_MODEL, D_MODEL)), full((1, MEM_HEAD_DIM))],
        out_specs=[full((rows, D_MODEL)), full((rows, D_MODEL))],
        out_shape=[jax.ShapeDtypeStruct((rows, D_MODEL), F32)] * 2,
        compiler_params=pltpu.CompilerParams(dimension_semantics=("arbitrary",),
                                             vmem_limit_bytes=VMEM_LIMIT),
        name="memory_kv",
    )(mem2d, g_src, w_mk, w_mv, g_mk)


FF_SLAB = 1024


def _mix_ffn_kernel(x_ref, ys_ref, ya_ref, wo_ref, gmx_ref, wq_ref, gmq_ref, mk_ref, mv_ref,
                    wmo_ref, gffn_ref, w1_ref, w2_ref, y_ref, obuf_ref, *, nb, rb, dense_rows):
    tr = x_ref.shape[0]
    groups = [slice(r0, r0 + dense_rows) for r0 in range(0, tr, dense_rows)]
    h = [x_ref[g, :] + _dot(ys_ref[g, :], wo_ref[0:SSD_WIDTH, :]) + _dot(ya_ref[g, :], wo_ref[SSD_WIDTH:, :])
         for g in groups]
    hn = [_rms(hg, gmx_ref[...]).astype(BF16) for hg in h]
    q = [_dot(hg, wq_ref[...]) for hg in hn]
    for hd in range(MEM_HEADS):
        sl = slice(hd * MEM_HEAD_DIM, (hd + 1) * MEM_HEAD_DIM)
        qn = [_mem_head_norm(qg, gmq_ref, hd, MEM_HEAD_DIM ** -0.5 * LOG2E).astype(BF16) for qg in q]
        for g, qg in zip(groups, qn):
            for r0 in range(g.start, g.stop, rb):
                b = r0 // rb if nb > 1 else 0
                s = _dot_nt(qg[r0 - g.start:r0 - g.start + min(rb, dense_rows), :], mk_ref[b, :, sl])
                p = jnp.exp2(s - jnp.max(s, axis=-1, keepdims=True))
                o = _dot(p.astype(BF16), mv_ref[b, :, sl]) * (1.0 / jnp.sum(p, axis=-1, keepdims=True))
                obuf_ref[r0:r0 + min(rb, dense_rows), sl] = o.astype(BF16)
    h = [hg + _dot(obuf_ref[g, :], wmo_ref[...]) for g, hg in zip(groups, h)]
    hn = [_rms(hg, gffn_ref[...]).astype(BF16) for hg in h]
    for s in range(D_FF // FF_SLAB):
        for gi in range(len(groups)):
            u = jnp.maximum(_dot(hn[gi], w1_ref[:, s * FF_SLAB:(s + 1) * FF_SLAB]), 0.0)
            h[gi] = h[gi] + _dot((u * u).astype(BF16), w2_ref[s * FF_SLAB:(s + 1) * FF_SLAB, :])
    for g, hg in zip(groups, h):
        y_ref[g, :] = hg


def _mix_ffn(x2d, ys, ya, w_out, g_mem_x, w_mq, g_mq, mk, mv, w_mo, g_ffn, w1, w2, *, tr, seq_len):
    rows = x2d.shape[0]
    n = rows // tr
    nb = max(tr // seq_len, 1)
    tiles_per_stream = max(seq_len // tr, 1)
    rb = tr // nb
    assert n * tr == rows and nb * rb == tr and mk.shape[0] * seq_len == rows

    def row(width):
        return pl.BlockSpec((tr, width), lambda i: (i, 0))

    def const(shape):
        return pl.BlockSpec(shape, lambda i: (0,) * len(shape))

    mem = pl.BlockSpec((nb, N_MEM, D_MODEL), lambda i: (i // tiles_per_stream, 0, 0))
    return pl.pallas_call(
        functools.partial(_mix_ffn_kernel, nb=nb, rb=rb, dense_rows=min(tr, MIX_DENSE_ROWS)),
        grid=(n,),
        in_specs=[row(D_MODEL), row(SSD_WIDTH), row(ATT_WIDTH), _resident((SSD_WIDTH + ATT_WIDTH, D_MODEL)),
                  const((1, D_MODEL)), _resident((D_MODEL, D_MODEL)), const((1, MEM_HEAD_DIM)), mem, mem,
                  _resident((D_MODEL, D_MODEL)), const((1, D_MODEL)), _resident((D_MODEL, D_FF)),
                  _resident((D_FF, D_MODEL))],
        out_specs=row(D_MODEL),
        out_shape=jax.ShapeDtypeStruct((rows, D_MODEL), F32),
        scratch_shapes=[pltpu.VMEM((tr, D_MODEL), BF16)],
        compiler_params=pltpu.CompilerParams(dimension_semantics=("arbitrary",),
                                             vmem_limit_bytes=VMEM_LIMIT),
        name="mix_ffn",
    )(x2d, ys, ya, w_out, g_mem_x, w_mq, g_mq, mk, mv, w_mo, g_ffn, w1, w2)


def _toeplitz_base(table, offset, width):
    heads, size = table.shape
    n_far = offset - REL_CLIP
    assert n_far >= 0
    parts = [jnp.broadcast_to(table[:, size - 1:], (heads, n_far)), table[:, ::-1]]
    rest = width - n_far - size
    if rest > 0:
        parts.append(jnp.broadcast_to(table[:, :1], (heads, rest)))
    return jnp.concatenate(parts, axis=1)[:, :width].astype(F32)


def _toeplitz_base_t(table, shift, width):
    heads, size = table.shape
    n_low = shift - REL_CLIP
    assert n_low >= 0
    parts = [jnp.broadcast_to(table[:, :1], (heads, n_low)), table]
    rest = width - n_low - size
    if rest > 0:
        parts.append(jnp.broadcast_to(table[:, size - 1:], (heads, rest)))
    return jnp.concatenate(parts, axis=1)[:, :width].astype(F32)


def _prep_weights(g_mix, w_in, conv_w, conv_b, ssd_A_log, ssd_dt_bias, ssd_D, ssd_g_out, att_g_q, att_g_k,
                  w_out, g_mem_x, g_mem_src, w_mq, w_mk, w_mv, g_mq, g_mk, w_mo, g_ffn, w_ff1, w_ff2):
    w_zx, w_dt, w_qkv = _split_w_in(w_in)
    pad_h = lambda v: jnp.pad(v, (0, DT_PAD - SSD_HEADS)).reshape(1, DT_PAD)
    row = lambda v: v.reshape(1, -1)
    expand = (jnp.arange(LANES)[:, None] == jnp.arange(SSD_WIDTH)[None, :] // SSD_HEAD_DIM).astype(BF16)
    return dict(
        g_mix=row(g_mix), w_zx=w_zx, w_dt=w_dt, w_qkv=w_qkv,
        gq_t=row(jnp.tile(att_g_q, ATT_HEADS)), gk_t=row(jnp.tile(att_g_k, ATT_HEADS)),
        conv_w=conv_w, conv_b=row(conv_b), dtb=pad_h(ssd_dt_bias), alog=pad_h(ssd_A_log),
        dskip=row(jnp.repeat(ssd_D, SSD_HEAD_DIM)), gout=row(ssd_g_out), expand=expand,
        w_out=w_out.astype(BF16), g_mem_x=row(g_mem_x), g_mem_src=row(g_mem_src),
        w_mq=w_mq.astype(BF16), w_mk=w_mk.astype(BF16), w_mv=w_mv.astype(BF16),
        g_mq=row(g_mq), g_mk=row(g_mk), w_mo=w_mo.astype(BF16), g_ffn=row(g_ffn),
        w_ff1=w_ff1.astype(BF16), w_ff2=w_ff2.astype(BF16))


def _layer(x, conv_prev, h0, k_cache, v_cache, mem_k, mem_v, p, rel, *, tr, t_scan):
    b, length, _ = x.shape
    rows = b * length
    x2d = x.reshape(rows, D_MODEL)
    prompt = k_cache is None
    tail_rows = PAST if prompt else rows
    cprev8 = jnp.pad(conv_prev, ((0, 0), (8 - (SSD_CONV - 1), 0), (0, 0)))
    tr_in = 2 * tr if prompt and rows % (2 * tr) == 0 and tail_rows % (2 * tr) == 0 else tr
    gz, xc, xtail, dtraw, q, k, v, k_tail, v_tail = _in_proj(
        x2d, p["g_mix"], p["w_zx"], p["w_dt"], p["w_qkv"], p["gq_t"], p["gk_t"], cprev8,
        p["conv_w"], p["conv_b"], tr=tr_in, seq_len=length, tail_rows=tail_rows, v_feature_major=prompt)
    conv_new = xtail[:, 8 - (SSD_CONV - 1):]

    scan_args = (xc.reshape(b, length, SSD_CONV_DIM), dtraw.reshape(b, length, DT_PAD),
                 gz.reshape(b, length, SSD_WIDTH), h0.reshape(b, SSD_WIDTH, SSD_STATE),
                 p["dtb"], p["alog"], p["dskip"], p["gout"], p["expand"])
    mix_args = (p["w_out"], p["g_mem_x"], p["w_mq"], p["g_mq"], mem_k.astype(BF16), mem_v.astype(BF16),
                p["w_mo"], p["g_ffn"], p["w_ff1"], p["w_ff2"])
    y_ssd, h_fin = _ssd_mixer(*scan_args, t=t_scan, lb=t_scan if prompt else length)
    if prompt:
        y_att = _band_prompt(q, k, v, _toeplitz_base_t(rel, ATT_SB, ATT_SB + ATT_WIN))
        k_rows = k_tail.reshape(b, PAST, ATT_HEADS, ATT_HEAD_DIM)
        v_rows = v_tail.reshape(b, PAST, ATT_HEADS, ATT_HEAD_DIM)
    else:
        lc = k_cache.shape[1]
        y_att = _band_sample(q, k, v, jnp.transpose(k_cache, (0, 2, 3, 1)).astype(BF16),
                             jnp.transpose(v_cache, (0, 2, 3, 1)).astype(BF16),
                             _toeplitz_base(rel, lc + LANES, lc + 2 * LANES), lq=length)
        k_rows = k_tail.reshape(b, length, ATT_HEADS, ATT_HEAD_DIM)
        v_rows = v_tail.reshape(b, length, ATT_HEADS, ATT_HEAD_DIM)
    tr_mix = 2 * tr if prompt and rows % (2 * tr) == 0 else tr
    y = _mix_ffn(x2d, y_ssd.reshape(rows, SSD_WIDTH), y_att, *mix_args, tr=tr_mix, seq_len=length)
    return (y.reshape(b, length, D_MODEL), h_fin.reshape(b, SSD_HEADS, SSD_HEAD_DIM, SSD_STATE),
            conv_new, k_rows, v_rows)


def kernel(x_prompt, x_sample, mem_prompt, state_ssd, state_conv, cache_attn_k, cache_attn_v, cache_mem_k,
           cache_mem_v, g_mix, w_in, conv_w, conv_b, ssd_A_log, ssd_dt_bias, ssd_D, ssd_g_out, att_g_q,
           att_g_k, att_rel_bias, w_out, g_mem_x, g_mem_src, w_mq, w_mk, w_mv, g_mq, g_mk, w_mo, g_ffn,
           w_ff1, w_ff2):
    depth = g_mix.shape[0]
    b_p, seq, _ = x_prompt.shape
    b_s, dec_seq, _ = x_sample.shape
    yp, ys = x_prompt, x_sample
    outs = [[] for _ in range(10)]
    for l in range(depth):
        p = _prep_weights(g_mix[l], w_in[l], conv_w[l], conv_b[l], ssd_A_log[l], ssd_dt_bias[l], ssd_D[l],
                          ssd_g_out[l], att_g_q[l], att_g_k[l], w_out[l], g_mem_x[l], g_mem_src[l],
                          w_mq[l], w_mk[l], w_mv[l], g_mq[l], g_mk[l], w_mo[l], g_ffn[l], w_ff1[l], w_ff2[l])
        rel = att_rel_bias[l]
        mk, mv = _memory_kv(mem_prompt.reshape(b_p * N_MEM, D_MODEL), p["g_mem_src"], p["w_mk"], p["w_mv"],
                            p["g_mk"])
        mk = mk.reshape(b_p, N_MEM, D_MODEL)
        mv = mv.reshape(b_p, N_MEM, D_MODEL)
        conv0 = jnp.zeros((b_p, SSD_CONV - 1, SSD_CONV_DIM), F32)
        h00 = jnp.zeros((b_p, SSD_HEADS, SSD_HEAD_DIM, SSD_STATE), F32)
        yp, hp, cp, kp, vp = _layer(yp, conv0, h00, None, None, mk, mv, p, rel,
                                    tr=min(256, seq), t_scan=min(256, seq))
        ys, hs, cs, ks_, vs_ = _layer(ys, state_conv[l], state_ssd[l], cache_attn_k[l], cache_attn_v[l],
                                      cache_mem_k[l].reshape(b_s, N_MEM, D_MODEL),
                                      cache_mem_v[l].reshape(b_s, N_MEM, D_MODEL), p, rel,
                                      tr=b_s * dec_seq, t_scan=128)
        for lst, val in zip(outs, (hp, cp, kp, vp,
                                   mk.reshape(b_p, N_MEM, MEM_HEADS, MEM_HEAD_DIM),
                                   mv.reshape(b_p, N_MEM, MEM_HEADS, MEM_HEAD_DIM),
                                   hs, cs, ks_, vs_)):
            lst.append(val)
    return (yp, ys) + tuple(jnp.stack(o) for o in outs)
```

```python
import functools

import jax
import jax.numpy as jnp
from jax import lax
from jax.experimental import pallas as pl
from jax.experimental.pallas import tpu as pltpu

F32 = jnp.float32
BF16 = jnp.bfloat16

D_MODEL = 1024
CHUNK = 64
SSD_HEADS = 16
SSD_HEAD_DIM = 64
SSD_WIDTH = SSD_HEADS * SSD_HEAD_DIM
SSD_GROUPS = 2
SSD_STATE = 128
SSD_CONV = 4
SSD_CONV_DIM = SSD_WIDTH + 2 * SSD_GROUPS * SSD_STATE
ATT_HEADS = 16
ATT_HEAD_DIM = 64
ATT_WIDTH = ATT_HEADS * ATT_HEAD_DIM
N_LEFT_CHUNKS = 8
PAST = N_LEFT_CHUNKS * CHUNK
REL_CLIP = 128
N_MEM = 256
MEM_HEADS = 4
MEM_HEAD_DIM = D_MODEL // MEM_HEADS
D_FF = 4 * D_MODEL
EPS = 1e-6

LANES = 128
DT_PAD = LANES
NEG = -1e30
LOG2E = 1.4426950408889634
CONV_ROWS = 64
MIX_DENSE_ROWS = 256
IN_GROUP_ROWS = 256
VMEM_LIMIT = 56 * 1024 * 1024


def _rms(x, g):
    return x * lax.rsqrt(jnp.mean(x * x, axis=-1, keepdims=True) + EPS) * g


def _silu(x):
    h = 0.5 * x
    return h + h * jnp.tanh(h)


def _split2(x):
    hi = x.astype(BF16)
    lo = (x - hi.astype(F32)).astype(BF16)
    return hi, lo


def _split3(x):
    hi = x.astype(BF16)
    r = x - hi.astype(F32)
    mid = r.astype(BF16)
    lo = (r - mid.astype(F32)).astype(BF16)
    return hi, mid, lo


def _resident(shape):
    return pl.BlockSpec(shape, lambda *_: (0,) * len(shape), pipeline_mode=pl.Buffered(1))


def _dot(a, b):
    return jnp.dot(a, b, preferred_element_type=F32)


def _dot_nt(a, b):
    return lax.dot_general(a, b, (((1,), (1,)), ((), ())), preferred_element_type=F32)


W_PREP_ROWS = 128


def _split_win_kernel(w_ref, wzx_ref, wdt_ref, wqkv_ref):
    o_dt = SSD_WIDTH + SSD_CONV_DIM
    o_qkv = o_dt + SSD_HEADS
    wzx_ref[...] = w_ref[:, 0:o_dt].astype(BF16)
    lane = lax.broadcasted_iota(jnp.int32, (w_ref.shape[0], DT_PAD), 1)
    wdt_ref[...] = jnp.where(lane < SSD_HEADS, w_ref[:, o_dt:o_dt + DT_PAD], 0.0).astype(BF16)
    wqkv_ref[...] = w_ref[:, o_qkv:o_qkv + 3 * ATT_WIDTH].astype(BF16)


def _split_w_in(w_in):
    d, width = w_in.shape
    n = d // W_PREP_ROWS
    widths = (SSD_WIDTH + SSD_CONV_DIM, DT_PAD, 3 * ATT_WIDTH)
    return pl.pallas_call(
        _split_win_kernel,
        grid=(n,),
        in_specs=[pl.BlockSpec((W_PREP_ROWS, width), lambda i: (i, 0))],
        out_specs=[pl.BlockSpec((W_PREP_ROWS, w), lambda i: (i, 0)) for w in widths],
        out_shape=[jax.ShapeDtypeStruct((d, w), BF16) for w in widths],
        compiler_params=pltpu.CompilerParams(dimension_semantics=("arbitrary",),
                                             vmem_limit_bytes=VMEM_LIMIT),
        name="split_w_in",
    )(w_in)


def _inproj_kernel(x_ref, gmix_ref, wzx_ref, wdt_ref, wqkv_ref, gq_ref, gk_ref, cprev_ref,
                   convw_ref, convb_ref,
                   gz_ref, xc_ref, xtail_ref, dt_ref, q_ref, k_ref, v_ref, kt_ref, vt_ref, ext_ref, wvt_ref,
                   *, n_tail, v_feature_major, nb, tiles_per_stream):
    i = pl.program_id(0)
    n = pl.num_programs(0)
    tr = x_ref.shape[0]
    rb = tr // nb
    halo = 8
    if v_feature_major:
        @pl.when(i == 0)
        def _():
            wvt_ref[...] = wqkv_ref[:, 2 * ATT_WIDTH:].T
    grp = min(tr, IN_GROUP_ROWS)
    groups = [slice(r0, r0 + grp) for r0 in range(0, tr, grp)]
    xn = [_rms(x_ref[g, :], gmix_ref[...]).astype(BF16) for g in groups]

    def proj(gi, loc, width):
        w_ref, lo = loc
        return _dot(xn[gi], w_ref[:, lo:lo + width])

    o_xbc = (wzx_ref, SSD_WIDTH)
    o_dt = (wdt_ref, 0)
    o_q = (wqkv_ref, 0)
    o_k = (wqkv_ref, ATT_WIDTH)
    o_v = (wqkv_ref, 2 * ATT_WIDTH)
    lane = lax.broadcasted_iota(jnp.int32, (grp, LANES), 1)
    first = lane < ATT_HEAD_DIM

    def head_norm(blk, g_ref, c, scale):
        sq = blk * blk
        s0 = jnp.sum(jnp.where(first, sq, 0.0), axis=-1, keepdims=True)
        s1 = jnp.sum(jnp.where(first, 0.0, sq), axis=-1, keepdims=True)
        r = jnp.where(first, lax.rsqrt(s0 * (1.0 / ATT_HEAD_DIM) + EPS),
                      lax.rsqrt(s1 * (1.0 / ATT_HEAD_DIM) + EPS))
        return blk * r * (g_ref[:, c * LANES:(c + 1) * LANES] * scale)

    chunk = 2 * LANES
    work = []

    def run(count):
        for _ in range(min(count, len(work))):
            work.pop(0)()

    xbc = [proj(gi, o_xbc, SSD_CONV_DIM) for gi in range(len(groups))]

    def xbc_rows(r0, r1):
        gi = r0 // grp
        return xbc[gi][r0 - gi * grp:r1 - gi * grp, :]

    for s in range(nb):
        base = s * (rb + halo)
        if tiles_per_stream == 1:
            carried = cprev_ref[s]
        else:
            carried = jnp.where(i % tiles_per_stream == 0, cprev_ref[s], ext_ref[rb:rb + halo, :])
        ext_ref[base:base + halo, :] = carried
        piece = min(rb, grp)
        for r0 in range(0, rb, piece):
            ext_ref[base + halo + r0:base + halo + r0 + piece, :] = xbc_rows(s * rb + r0, s * rb + r0 + piece)
        xtail_ref[s] = xbc_rows((s + 1) * rb - halo, (s + 1) * rb)

        def conv_piece(s=s, base=base, cb=0, r0=0, nr=rb):
            sl = slice(cb * LANES, (cb + 1) * LANES)
            xe = ext_ref[base + r0:base + r0 + halo + nr, sl]
            acc = convb_ref[:, sl] + convw_ref[SSD_CONV - 1:SSD_CONV, sl] * xe[halo:, :]
            for j in range(1, SSD_CONV):
                tap = pltpu.roll(xe, j, 0)[halo:, :]
                acc = acc + convw_ref[SSD_CONV - 1 - j:SSD_CONV - j, sl] * tap
            xc_ref[s * rb + r0:s * rb + r0 + nr, sl] = _silu(acc)

        nr = min(rb, CONV_ROWS)
        for cb in range(SSD_CONV_DIM // LANES):
            for r0 in range(0, rb, nr):
                work.append(functools.partial(conv_piece, cb=cb, r0=r0, nr=nr))

    def gate(zc, c, g):
        gz_ref[g, c * LANES:(c + 1) * LANES] = _silu(zc)

    def norm_q(blk, c, g):
        q_ref[g, c * LANES:(c + 1) * LANES] = head_norm(
            blk, gq_ref, c, ATT_HEAD_DIM ** -0.5 * LOG2E).astype(BF16)

    def norm_k(blk, c, g):
        kn = head_norm(blk, gk_ref, c, 1.0)
        k_ref[g, c * LANES:(c + 1) * LANES] = kn.astype(BF16)
        kt_ref[g, c * LANES:(c + 1) * LANES] = kn

    def chunks(loc, width, consumer, per_chunk):
        w_ref, lo = loc
        for c0 in range(0, width, chunk):
            for gi, g in enumerate(groups):
                y = proj(gi, (w_ref, lo + c0), chunk)
                for t in range(chunk // LANES):
                    work.append(functools.partial(consumer, y[:, t * LANES:(t + 1) * LANES], c0 // LANES + t, g))
                run(per_chunk)

    per_chunk = -(-len(work) // (12 * len(groups))) + 2
    chunks((wzx_ref, 0), SSD_WIDTH, gate, per_chunk)
    chunks(o_q, ATT_WIDTH, norm_q, per_chunk)
    chunks(o_k, ATT_WIDTH, norm_k, per_chunk)
    for c0 in range(0, ATT_WIDTH, chunk):
        for gi, g in enumerate(groups):
            if v_feature_major:
                v_ref[c0:c0 + chunk, g] = _dot_nt(wvt_ref[c0:c0 + chunk, :], xn[gi]).astype(BF16)
            else:
                v = proj(gi, (wqkv_ref, 2 * ATT_WIDTH + c0), chunk)
                v_ref[g, c0:c0 + chunk] = v.astype(BF16)
                vt_ref[g, c0:c0 + chunk] = v
            run(3)
    run(len(work))
    for gi, g in enumerate(groups):
        dt_ref[g, :] = proj(gi, o_dt, DT_PAD)
    if v_feature_major:
        @pl.when(i >= n - n_tail)
        def _():
            for gi, g in enumerate(groups):
                vt_ref[g, :] = proj(gi, o_v, ATT_WIDTH)


def _in_proj(x2d, g_mix, w_zx, w_dt, w_qkv, gq_t, gk_t, cprev8, conv_w, conv_b, *, tr, seq_len,
             tail_rows, v_feature_major):
    rows = x2d.shape[0]
    n = rows // tr
    n_tail = tail_rows // tr
    nb = max(tr // seq_len, 1)
    tiles_per_stream = max(seq_len // tr, 1)
    n_streams = rows // seq_len
    assert n * tr == rows and n_tail * tr == tail_rows and cprev8.shape == (n_streams, 8, SSD_CONV_DIM)
    assert (tr // nb) % 8 == 0

    def row(width):
        return pl.BlockSpec((tr, width), lambda i: (i, 0))

    def const(shape):
        return pl.BlockSpec(shape, lambda i: (0,) * len(shape))

    per_stream = pl.BlockSpec((nb, 8, SSD_CONV_DIM), lambda i: (i // tiles_per_stream, 0, 0))
    tail = pl.BlockSpec((tr, ATT_WIDTH), lambda i: (jnp.maximum(i - (n - n_tail), 0), 0))
    if v_feature_major:
        v_spec = pl.BlockSpec((ATT_WIDTH, tr), lambda i: (0, i))
        v_shape = jax.ShapeDtypeStruct((ATT_WIDTH, rows), BF16)
    else:
        v_spec = row(ATT_WIDTH)
        v_shape = jax.ShapeDtypeStruct((rows, ATT_WIDTH), BF16)
    return pl.pallas_call(
        functools.partial(_inproj_kernel, n_tail=n_tail, v_feature_major=v_feature_major, nb=nb,
                          tiles_per_stream=tiles_per_stream),
        grid=(n,),
        in_specs=[row(D_MODEL), const((1, D_MODEL)), _resident((D_MODEL, SSD_WIDTH + SSD_CONV_DIM)),
                  _resident((D_MODEL, DT_PAD)), _resident((D_MODEL, 3 * ATT_WIDTH)),
                  const((1, ATT_WIDTH)), const((1, ATT_WIDTH)),
                  per_stream, const((SSD_CONV, SSD_CONV_DIM)), const((1, SSD_CONV_DIM))],
        out_specs=[row(SSD_WIDTH), row(SSD_CONV_DIM), per_stream, row(DT_PAD), row(ATT_WIDTH),
                   row(ATT_WIDTH), v_spec, tail, tail],
        out_shape=[jax.ShapeDtypeStruct((rows, SSD_WIDTH), F32),
                   jax.ShapeDtypeStruct((rows, SSD_CONV_DIM), F32),
                   jax.ShapeDtypeStruct((n_streams, 8, SSD_CONV_DIM), F32),
                   jax.ShapeDtypeStruct((rows, DT_PAD), F32),
                   jax.ShapeDtypeStruct((rows, ATT_WIDTH), BF16),
                   jax.ShapeDtypeStruct((rows, ATT_WIDTH), BF16),
                   v_shape,
                   jax.ShapeDtypeStruct((tail_rows, ATT_WIDTH), F32),
                   jax.ShapeDtypeStruct((tail_rows, ATT_WIDTH), F32)],
        scratch_shapes=[pltpu.VMEM((nb * (tr // nb + 8), SSD_CONV_DIM), F32),
                        pltpu.VMEM((ATT_WIDTH, D_MODEL) if v_feature_major else (16, LANES), BF16)],
        compiler_params=pltpu.CompilerParams(dimension_semantics=("arbitrary",),
                                             vmem_limit_bytes=VMEM_LIMIT),
        name="in_proj",
    )(x2d, g_mix, w_zx, w_dt, w_qkv, gq_t, gk_t, cprev8, conv_w, conv_b)


def _ssd_pairs(xc_ref, dtraw_ref, gz_ref, dtb_ref, alog_ref, dskip_ref, expand_ref, ht_ref, ybuf_ref,
               *, t, lb):
    def rows_of(ref, sl):
        v = ref[0, :, sl]
        if lb < t:
            v = jnp.concatenate([v, jnp.zeros((t - lb, v.shape[1]), v.dtype)], axis=0)
        return v

    lane = lax.broadcasted_iota(jnp.int32, (t, LANES), 1)
    rowi = lax.broadcasted_iota(jnp.int32, (t, LANES), 0)
    dt = jax.nn.softplus(rows_of(dtraw_ref, slice(None)) + dtb_ref[...])
    dt = jnp.where((lane < SSD_HEADS) & (rowi < lb), dt, 0.0)
    a_neg = -jnp.exp(alog_ref[...]) * LOG2E
    a = dt * a_neg

    rr = lax.broadcasted_iota(jnp.int32, (t, t), 0)
    cc = lax.broadcasted_iota(jnp.int32, (t, t), 1)
    causal = rr >= cc
    tril = jnp.where(causal, 1.0, 0.0).astype(BF16)
    a1, a2, a3 = _split3(a)
    a_cum = _dot(tril, a1) + _dot(tril, a2) + _dot(tril, a3)
    a_last = a_cum[t - 1:t, :]
    ea = jnp.exp2(a_cum)
    cd = jnp.exp2(a_last)
    a_t = a_cum.T
    w_t = jnp.exp2(a_t[:, t - 1:t] - a_t)

    stacked = jnp.concatenate([ea, dt, jnp.broadcast_to(cd, (8, LANES))], axis=0)
    s_hi, s_lo = _split2(stacked)
    expanded = _dot(s_hi, expand_ref[...]) + _dot(s_lo, expand_ref[...])
    ea_x = expanded[0:t, :]
    dt_x = expanded[t:2 * t, :]
    cd_x = expanded[2 * t:2 * t + 1, :]

    first = lane < SSD_HEAD_DIM
    heads_per_group = SSD_HEADS // SSD_GROUPS
    o_b = SSD_WIDTH
    o_c = SSD_WIDTH + SSD_GROUPS * SSD_STATE
    ssq = [jnp.zeros((lb, 1), F32)]
    per_group = {}

    def group_values(g):
        if g not in per_group:
            bg = rows_of(xc_ref, slice(o_b + g * SSD_STATE, o_b + (g + 1) * SSD_STATE))
            cg = rows_of(xc_ref, slice(o_c + g * SSD_STATE, o_c + (g + 1) * SSD_STATE)).astype(BF16)
            cb_mat = jnp.where(causal, _dot_nt(cg, bg.astype(BF16)), 0.0)
            per_group[g] = (cg, cb_mat, bg.T)
        return per_group[g]

    def prepare(j):
        cg, cb_mat, bg_t = group_values(j // (heads_per_group // 2))
        sl = slice(j * LANES, (j + 1) * LANES)
        xh = rows_of(xc_ref, sl)
        xdt = xh * dt_x[:, sl]
        ops = []
        for hh in range(2):
            h = 2 * j + hh
            xm = jnp.where(first if hh == 0 else jnp.logical_not(first), xdt, 0.0).astype(BF16)
            seg = a_cum[:, h:h + 1] - a_t[h:h + 1, :]
            m = (cb_mat * jnp.exp2(jnp.minimum(seg, 0.0))).astype(BF16)
            bw = (bg_t * w_t[h:h + 1, :]).astype(BF16)
            ops.append((m, bw, xm))
        return cg, xh, ops

    def finish(j, prepared):
        cg, xh, ops = prepared
        sl = slice(j * LANES, (j + 1) * LANES)
        y_pair = _dot(ops[0][0], ops[0][2]) + _dot(ops[1][0], ops[1][2])
        s_pair = _dot(ops[0][1], ops[0][2]) + _dot(ops[1][1], ops[1][2])
        h_in = ht_ref[:, sl]
        y_off = _dot(cg, h_in.astype(BF16)) * ea_x[:, sl]
        ht_ref[:, sl] = cd_x[:, sl] * h_in + s_pair
        y = y_pair + y_off + dskip_ref[:, sl] * xh
        yg = y[0:lb, :] * gz_ref[0, :, sl]
        ybuf_ref[:, sl] = yg
        ssq[0] = ssq[0] + jnp.sum(yg * yg, axis=-1, keepdims=True)

    n_tiles = SSD_WIDTH // LANES
    ahead = prepare(0)
    for j in range(n_tiles):
        cur = ahead
        if j + 1 < n_tiles:
            ahead = prepare(j + 1)
        finish(j, cur)
    return ssq[0]


def _ssd_kernel(xc_ref, dtraw_ref, gz_ref, h0_ref, dtb_ref, alog_ref, dskip_ref, gout_ref, expand_ref,
                y_ref, hfin_ref,
                ht_ref, ybuf_ref, *, t, lb):
    c = pl.program_id(1)
    nc = pl.num_programs(1)

    @pl.when(c == 0)
    def _():
        ht_ref[...] = h0_ref[0].T

    ssq = _ssd_pairs(xc_ref, dtraw_ref, gz_ref, dtb_ref, alog_ref, dskip_ref, expand_ref, ht_ref,
                     ybuf_ref, t=t, lb=lb)
    r = lax.rsqrt(ssq * (1.0 / SSD_WIDTH) + EPS)
    y_ref[0] = (ybuf_ref[...] * r * gout_ref[...]).astype(BF16)

    @pl.when(c == nc - 1)
    def _():
        hfin_ref[0] = ht_ref[...].T


def _ssd_mixer(xc, dtraw, gz, h0, dtb, alog, dskip, gout, expand, *, t, lb):
    b, length, _ = xc.shape
    nc = length // lb
    assert nc * lb == length and (lb == t or nc == 1)

    def seq(width):
        return pl.BlockSpec((1, lb, width), lambda bi, ci: (bi, ci, 0))

    def per_b(shape):
        return pl.BlockSpec((1,) + shape, lambda bi, ci: (bi, 0, 0))

    def const(shape):
        return pl.BlockSpec(shape, lambda bi, ci: (0,) * len(shape))

    hp = SSD_WIDTH
    return pl.pallas_call(
        functools.partial(_ssd_kernel, t=t, lb=lb),
        grid=(b, nc),
        in_specs=[seq(SSD_CONV_DIM), seq(DT_PAD), seq(SSD_WIDTH), per_b((hp, SSD_STATE)),
                  const((1, DT_PAD)), const((1, DT_PAD)), const((1, SSD_WIDTH)), const((1, SSD_WIDTH)),
                  const((LANES, SSD_WIDTH))],
        out_specs=[seq(SSD_WIDTH), per_b((hp, SSD_STATE))],
        out_shape=[jax.ShapeDtypeStruct((b, length, SSD_WIDTH), BF16),
                   jax.ShapeDtypeStruct((b, hp, SSD_STATE), F32)],
        scratch_shapes=[pltpu.VMEM((SSD_STATE, hp), F32),
                        pltpu.VMEM((lb, SSD_WIDTH), F32)],
        compiler_params=pltpu.CompilerParams(dimension_semantics=("arbitrary", "arbitrary"),
                                             vmem_limit_bytes=VMEM_LIMIT),
        name="ssd_mixer",
    )(xc, dtraw, gz, h0, dtb, alog, dskip, gout, expand)


ATT_SB = 2 * CHUNK
ATT_WIN = ATT_SB + PAST


def _toeplitz_rows(base_row, rows, offset, width):
    ext = base_row.shape[1]
    rolled = pltpu.roll(jnp.broadcast_to(base_row, (rows, ext)), 0, 1, stride=1, stride_axis=0)
    return rolled[:, offset:offset + width]


def _band_subblock(q, kw, vt, bias_t, first):
    return _band_softmax_pv(_band_scores(q, kw, bias_t, first), vt)


def _band_scores(q, kw, bias_t, first):
    zero = jnp.zeros_like(q)
    q2 = jnp.concatenate([jnp.where(first, q, zero), jnp.where(first, zero, q)], axis=0)
    return _dot_nt(kw, q2) + bias_t


def _band_softmax(s):
    m = jnp.max(s, axis=0, keepdims=True)
    return jnp.exp2((s - m).astype(BF16))


def _band_pv(p, vt):
    sb = p.shape[1] // 2
    half = ATT_HEAD_DIM
    ones = jnp.ones((16, vt.shape[1]), BF16)
    o_t = _dot(jnp.concatenate([vt, ones], axis=0), p)
    inv_l = 1.0 / o_t[2 * half:2 * half + 1, :]
    o_pair_t = jnp.concatenate([o_t[0:half, 0:sb] * inv_l[:, 0:sb],
                                o_t[half:2 * half, sb:2 * sb] * inv_l[:, sb:2 * sb]], axis=0)
    return o_pair_t.T.astype(BF16)


def _band_softmax_pv(s, vt):
    return _band_pv(_band_softmax(s), vt)


def _band_prompt_kernel(q_ref, k_ref, vt_ref, base_ref, o_ref, bias_ref, sc_ref, pb_ref):
    length = q_ref.shape[0]
    sb, w = ATT_SB, ATT_WIN
    kj = lax.broadcasted_iota(jnp.int32, (w, sb), 0) // CHUNK
    qi = lax.broadcasted_iota(jnp.int32, (w, sb), 1) // CHUNK
    in_band = (kj >= qi) & (kj <= qi + N_LEFT_CHUNKS)
    for hh in range(2):
        toe = _toeplitz_rows(base_ref[0, hh:hh + 1, :], w, w, sb)
        bias_ref[:, hh * sb:(hh + 1) * sb] = jnp.where(in_band, toe * LOG2E, NEG)

    lane = lax.broadcasted_iota(jnp.int32, (sb, LANES), 1)
    first = lane < ATT_HEAD_DIM
    n_sub = length // sb
    n_head = min(PAST // sb, n_sub)
    for j in range(n_head):
        n = (j + 1) * sb
        o_ref[j * sb:(j + 1) * sb, :] = _band_subblock(
            q_ref[j * sb:(j + 1) * sb, :], k_ref[0:n, :], vt_ref[:, 0:n], bias_ref[w - n:, :], first)

    n_main = n_sub - n_head
    if n_main == 0:
        return
    assert n_main % 2 == 0

    def scores_into(slot, j):
        r0 = pl.multiple_of(j * sb, sb)
        k0 = pl.multiple_of(j * sb - PAST, sb)
        sc_ref[slot] = _band_scores(q_ref[pl.ds(r0, sb), :], k_ref[pl.ds(k0, w), :], bias_ref[...], first)

    def softmax_into(slot):
        pb_ref[slot] = _band_softmax(sc_ref[slot])

    def values_out(slot, j):
        k0 = pl.multiple_of(j * sb - PAST, sb)
        return _band_pv(pb_ref[slot], vt_ref[:, pl.ds(k0, w)])

    last = n_sub - 1
    scores_into(0, n_head)
    scores_into(1, n_head + 1)
    softmax_into(0)

    per_trip = 4 if n_main % 4 == 0 else 2

    def body(i, carry):
        outs = []
        for u in range(per_trip):
            j = n_head + per_trip * i + u
            slot = u % 2
            scores_into(slot, jnp.minimum(j + 2, last))
            softmax_into(1 - slot)
            outs.append((j, values_out(slot, j)))
        for j, o in outs:
            o_ref[pl.ds(pl.multiple_of(j * sb, sb), sb), :] = o
        return carry

    lax.fori_loop(0, n_main // per_trip, body, 0)


def _band_prompt(q, k, vt, base):
    length = q.shape[0]
    assert length % ATT_SB == 0
    ext = ATT_SB + ATT_WIN
    col = pl.BlockSpec((length, LANES), lambda c: (0, c))
    return pl.pallas_call(
        _band_prompt_kernel,
        grid=(ATT_WIDTH // LANES,),
        in_specs=[col, col, pl.BlockSpec((LANES, length), lambda c: (c, 0)),
                  pl.BlockSpec((1, 2, ext), lambda c: (c, 0, 0))],
        out_specs=col,
        out_shape=jax.ShapeDtypeStruct((length, ATT_WIDTH), BF16),
        scratch_shapes=[pltpu.VMEM((ATT_WIN, 2 * ATT_SB), F32),
                        pltpu.VMEM((2, ATT_WIN, 2 * ATT_SB), F32),
                        pltpu.VMEM((2, ATT_WIN, 2 * ATT_SB), BF16)],
        compiler_params=pltpu.CompilerParams(dimension_semantics=("arbitrary",),
                                             vmem_limit_bytes=VMEM_LIMIT),
        name="band_prompt",
    )(q, k, vt, base.reshape(ATT_HEADS // 2, 2, ext))


def _band_sample_kernel(q_ref, kn_ref, vn_ref, kc_ref, vc_ref, base_ref, o_ref, bias_ref):
    lq = q_ref.shape[0]
    lc = kc_ref.shape[3]

    @pl.when(pl.program_id(0) == 0)
    def _():
        for h in range(ATT_HEADS):
            bias_ref[h] = _toeplitz_rows(base_ref[h:h + 1, :], lq, LANES, lc + LANES) * LOG2E

    for h in range(ATT_HEADS):
        sl = slice(h * ATT_HEAD_DIM, (h + 1) * ATT_HEAD_DIM)
        q = q_ref[:, sl]
        sc = _dot(q, kc_ref[0, h]) + bias_ref[h, :, 0:lc]
        sn = _dot_nt(q, kn_ref[:, sl]) + bias_ref[h, :, lc:lc + lq]
        m = jnp.maximum(jnp.max(sc, axis=-1, keepdims=True), jnp.max(sn, axis=-1, keepdims=True))
        pc = jnp.exp2(sc - m)
        pn = jnp.exp2(sn - m)
        l = jnp.sum(pc, axis=-1, keepdims=True) + jnp.sum(pn, axis=-1, keepdims=True)
        o = _dot_nt(pc.astype(BF16), vc_ref[0, h]) + _dot(pn.astype(BF16), vn_ref[:, sl])
        o_ref[:, sl] = (o * (1.0 / l)).astype(BF16)


def _band_sample(q, kn, vn, kc, vc, base, *, lq):
    rows = q.shape[0]
    nb = rows // lq
    lc = kc.shape[3]
    assert lq <= LANES and lc % LANES == 0 and base.shape == (ATT_HEADS, lc + 2 * LANES)
    blk = pl.BlockSpec((lq, ATT_WIDTH), lambda b: (b, 0))
    cache = pl.BlockSpec((1, ATT_HEADS, ATT_HEAD_DIM, lc), lambda b: (b, 0, 0, 0))
    return pl.pallas_call(
        _band_sample_kernel,
        grid=(nb,),
        in_specs=[blk, blk, blk, cache, cache, pl.BlockSpec(base.shape, lambda b: (0, 0))],
        out_specs=blk,
        out_shape=jax.ShapeDtypeStruct((rows, ATT_WIDTH), BF16),
        scratch_shapes=[pltpu.VMEM((ATT_HEADS, lq, lc + LANES), F32)],
        compiler_params=pltpu.CompilerParams(dimension_semantics=("arbitrary",),
                                             vmem_limit_bytes=VMEM_LIMIT),
        name="band_sample",
    )(q, kn, vn, kc, vc, base)


def _mem_head_norm(y, g_ref, hd, scale):
    blk = y[:, hd * MEM_HEAD_DIM:(hd + 1) * MEM_HEAD_DIM]
    r = lax.rsqrt(jnp.mean(blk * blk, axis=-1, keepdims=True) + EPS)
    return blk * r * (g_ref[...] * scale)


def _memkv_kernel(mem_ref, gsrc_ref, wk_ref, wv_ref, gk_ref, k_ref, v_ref):
    m = _rms(mem_ref[...], gsrc_ref[...]).astype(BF16)
    k = _dot(m, wk_ref[...])
    for hd in range(MEM_HEADS):
        k_ref[:, hd * MEM_HEAD_DIM:(hd + 1) * MEM_HEAD_DIM] = _mem_head_norm(k, gk_ref, hd, 1.0)
    v_ref[...] = _dot(m, wv_ref[...])


def _memory_kv(mem2d, g_src, w_mk, w_mv, g_mk):
    rows = mem2d.shape[0]

    def full(shape):
        return pl.BlockSpec(shape, lambda i: (0,) * len(shape))

    return pl.pallas_call(
        _memkv_kernel,
        grid=(1,),
        in_specs=[full((rows, D_MODEL)), full((1, D_MODEL)), full((D_MODEL, D_MODEL)),
                  full((D_MODEL, D_MODEL)), full((1, MEM_HEAD_DIM))],
        out_specs=[full((rows, D_MODEL)), full((rows, D_MODEL))],
        out_shape=[jax.ShapeDtypeStruct((rows, D_MODEL), F32)] * 2,
        compiler_params=pltpu.CompilerParams(dimension_semantics=("arbitrary",),
                                             vmem_limit_bytes=VMEM_LIMIT),
        name="memory_kv",
    )(mem2d, g_src, w_mk, w_mv, g_mk)


FF_SLAB = 1024


def _mix_ffn_kernel(x_ref, ys_ref, ya_ref, wo_ref, gmx_ref, wq_ref, gmq_ref, mk_ref, mv_ref,
                    wmo_ref, gffn_ref, w1_ref, w2_ref, y_ref, obuf_ref, *, nb, rb, dense_rows):
    tr = x_ref.shape[0]
    groups = [slice(r0, r0 + dense_rows) for r0 in range(0, tr, dense_rows)]
    h = [x_ref[g, :] + _dot(ys_ref[g, :], wo_ref[0:SSD_WIDTH, :]) + _dot(ya_ref[g, :], wo_ref[SSD_WIDTH:, :])
         for g in groups]
    hn = [_rms(hg, gmx_ref[...]).astype(BF16) for hg in h]
    q = [_dot(hg, wq_ref[...]) for hg in hn]
    for hd in range(MEM_HEADS):
        sl = slice(hd * MEM_HEAD_DIM, (hd + 1) * MEM_HEAD_DIM)
        qn = [_mem_head_norm(qg, gmq_ref, hd, MEM_HEAD_DIM ** -0.5 * LOG2E).astype(BF16) for qg in q]
        for g, qg in zip(groups, qn):
            for r0 in range(g.start, g.stop, rb):
                b = r0 // rb if nb > 1 else 0
                s = _dot_nt(qg[r0 - g.start:r0 - g.start + min(rb, dense_rows), :], mk_ref[b, :, sl])
                p = jnp.exp2(s - jnp.max(s, axis=-1, keepdims=True))
                o = _dot(p.astype(BF16), mv_ref[b, :, sl]) * (1.0 / jnp.sum(p, axis=-1, keepdims=True))
                obuf_ref[r0:r0 + min(rb, dense_rows), sl] = o.astype(BF16)
    h = [hg + _dot(obuf_ref[g, :], wmo_ref[...]) for g, hg in zip(groups, h)]
    hn = [_rms(hg, gffn_ref[...]).astype(BF16) for hg in h]
    for s in range(D_FF // FF_SLAB):
        for gi in range(len(groups)):
            u = jnp.maximum(_dot(hn[gi], w1_ref[:, s * FF_SLAB:(s + 1) * FF_SLAB]), 0.0)
            h[gi] = h[gi] + _dot((u * u).astype(BF16), w2_ref[s * FF_SLAB:(s + 1) * FF_SLAB, :])
    for g, hg in zip(groups, h):
        y_ref[g, :] = hg


def _mix_ffn(x2d, ys, ya, w_out, g_mem_x, w_mq, g_mq, mk, mv, w_mo, g_ffn, w1, w2, *, tr, seq_len):
    rows = x2d.shape[0]
    n = rows // tr
    nb = max(tr // seq_len, 1)
    tiles_per_stream = max(seq_len // tr, 1)
    rb = tr // nb
    assert n * tr == rows and nb * rb == tr and mk.shape[0] * seq_len == rows

    def row(width):
        return pl.BlockSpec((tr, width), lambda i: (i, 0))

    def const(shape):
        return pl.BlockSpec(shape, lambda i: (0,) * len(shape))

    mem = pl.BlockSpec((nb, N_MEM, D_MODEL), lambda i: (i // tiles_per_stream, 0, 0))
    return pl.pallas_call(
        functools.partial(_mix_ffn_kernel, nb=nb, rb=rb, dense_rows=min(tr, MIX_DENSE_ROWS)),
        grid=(n,),
        in_specs=[row(D_MODEL), row(SSD_WIDTH), row(ATT_WIDTH), _resident((SSD_WIDTH + ATT_WIDTH, D_MODEL)),
                  const((1, D_MODEL)), _resident((D_MODEL, D_MODEL)), const((1, MEM_HEAD_DIM)), mem, mem,
                  _resident((D_MODEL, D_MODEL)), const((1, D_MODEL)), _resident((D_MODEL, D_FF)),
                  _resident((D_FF, D_MODEL))],
        out_specs=row(D_MODEL),
        out_shape=jax.ShapeDtypeStruct((rows, D_MODEL), F32),
        scratch_shapes=[pltpu.VMEM((tr, D_MODEL), BF16)],
        compiler_params=pltpu.CompilerParams(dimension_semantics=("arbitrary",),
                                             vmem_limit_bytes=VMEM_LIMIT),
        name="mix_ffn",
    )(x2d, ys, ya, w_out, g_mem_x, w_mq, g_mq, mk, mv, w_mo, g_ffn, w1, w2)


def _toeplitz_base(table, offset, width):
    heads, size = table.shape
    n_far = offset - REL_CLIP
    assert n_far >= 0
    parts = [jnp.broadcast_to(table[:, size - 1:], (heads, n_far)), table[:, ::-1]]
    rest = width - n_far - size
    if rest > 0:
        parts.append(jnp.broadcast_to(table[:, :1], (heads, rest)))
    return jnp.concatenate(parts, axis=1)[:, :width].astype(F32)


def _toeplitz_base_t(table, shift, width):
    heads, size = table.shape
    n_low = shift - REL_CLIP
    assert n_low >= 0
    parts = [jnp.broadcast_to(table[:, :1], (heads, n_low)), table]
    rest = width - n_low - size
    if rest > 0:
        parts.append(jnp.broadcast_to(table[:, size - 1:], (heads, rest)))
    return jnp.concatenate(parts, axis=1)[:, :width].astype(F32)


def _prep_weights(g_mix, w_in, conv_w, conv_b, ssd_A_log, ssd_dt_bias, ssd_D, ssd_g_out, att_g_q, att_g_k,
                  w_out, g_mem_x, g_mem_src, w_mq, w_mk, w_mv, g_mq, g_mk, w_mo, g_ffn, w_ff1, w_ff2):
    w_zx, w_dt, w_qkv = _split_w_in(w_in)
    pad_h = lambda v: jnp.pad(v, (0, DT_PAD - SSD_HEADS)).reshape(1, DT_PAD)
    row = lambda v: v.reshape(1, -1)
    expand = (jnp.arange(LANES)[:, None] == jnp.arange(SSD_WIDTH)[None, :] // SSD_HEAD_DIM).astype(BF16)
    return dict(
        g_mix=row(g_mix), w_zx=w_zx, w_dt=w_dt, w_qkv=w_qkv,
        gq_t=row(jnp.tile(att_g_q, ATT_HEADS)), gk_t=row(jnp.tile(att_g_k, ATT_HEADS)),
        conv_w=conv_w, conv_b=row(conv_b), dtb=pad_h(ssd_dt_bias), alog=pad_h(ssd_A_log),
        dskip=row(jnp.repeat(ssd_D, SSD_HEAD_DIM)), gout=row(ssd_g_out), expand=expand,
        w_out=w_out.astype(BF16), g_mem_x=row(g_mem_x), g_mem_src=row(g_mem_src),
        w_mq=w_mq.astype(BF16), w_mk=w_mk.astype(BF16), w_mv=w_mv.astype(BF16),
        g_mq=row(g_mq), g_mk=row(g_mk), w_mo=w_mo.astype(BF16), g_ffn=row(g_ffn),
        w_ff1=w_ff1.astype(BF16), w_ff2=w_ff2.astype(BF16))


def _layer(x, conv_prev, h0, k_cache, v_cache, mem_k, mem_v, p, rel, *, tr, t_scan):
    b, length, _ = x.shape
    rows = b * length
    x2d = x.reshape(rows, D_MODEL)
    prompt = k_cache is None
    tail_rows = PAST if prompt else rows
    cprev8 = jnp.pad(conv_prev, ((0, 0), (8 - (SSD_CONV - 1), 0), (0, 0)))
    tr_in = 2 * tr if prompt and rows % (2 * tr) == 0 and tail_rows % (2 * tr) == 0 else tr
    gz, xc, xtail, dtraw, q, k, v, k_tail, v_tail = _in_proj(
        x2d, p["g_mix"], p["w_zx"], p["w_dt"], p["w_qkv"], p["gq_t"], p["gk_t"], cprev8,
        p["conv_w"], p["conv_b"], tr=tr_in, seq_len=length, tail_rows=tail_rows, v_feature_major=prompt)
    conv_new = xtail[:, 8 - (SSD_CONV - 1):]

    scan_args = (xc.reshape(b, length, SSD_CONV_DIM), dtraw.reshape(b, length, DT_PAD),
                 gz.reshape(b, length, SSD_WIDTH), h0.reshape(b, SSD_WIDTH, SSD_STATE),
                 p["dtb"], p["alog"], p["dskip"], p["gout"], p["expand"])
    mix_args = (p["w_out"], p["g_mem_x"], p["w_mq"], p["g_mq"], mem_k.astype(BF16), mem_v.astype(BF16),
                p["w_mo"], p["g_ffn"], p["w_ff1"], p["w_ff2"])
    y_ssd, h_fin = _ssd_mixer(*scan_args, t=t_scan, lb=t_scan if prompt else length)
    if prompt:
        y_att = _band_prompt(q, k, v, _toeplitz_base_t(rel, ATT_SB, ATT_SB + ATT_WIN))
        k_rows = k_tail.reshape(b, PAST, ATT_HEADS, ATT_HEAD_DIM)
        v_rows = v_tail.reshape(b, PAST, ATT_HEADS, ATT_HEAD_DIM)
    else:
        lc = k_cache.shape[1]
        y_att = _band_sample(q, k, v, jnp.transpose(k_cache, (0, 2, 3, 1)).astype(BF16),
                             jnp.transpose(v_cache, (0, 2, 3, 1)).astype(BF16),
                             _toeplitz_base(rel, lc + LANES, lc + 2 * LANES), lq=length)
        k_rows = k_tail.reshape(b, length, ATT_HEADS, ATT_HEAD_DIM)
        v_rows = v_tail.reshape(b, length, ATT_HEADS, ATT_HEAD_DIM)
    tr_mix = 2 * tr if prompt and rows % (2 * tr) == 0 else tr
    y = _mix_ffn(x2d, y_ssd.reshape(rows, SSD_WIDTH), y_att, *mix_args, tr=tr_mix, seq_len=length)
    return (y.reshape(b, length, D_MODEL), h_fin.reshape(b, SSD_HEADS, SSD_HEAD_DIM, SSD_STATE),
            conv_new, k_rows, v_rows)


def kernel(x_prompt, x_sample, mem_prompt, state_ssd, state_conv, cache_attn_k, cache_attn_v, cache_mem_k,
           cache_mem_v, g_mix, w_in, conv_w, conv_b, ssd_A_log, ssd_dt_bias, ssd_D, ssd_g_out, att_g_q,
           att_g_k, att_rel_bias, w_out, g_mem_x, g_mem_src, w_mq, w_mk, w_mv, g_mq, g_mk, w_mo, g_ffn,
           w_ff1, w_ff2):
    depth = g_mix.shape[0]
    b_p, seq, _ = x_prompt.shape
    b_s, dec_seq, _ = x_sample.shape
    yp, ys = x_prompt, x_sample
    outs = [[] for _ in range(10)]
    for l in range(depth):
        p = _prep_weights(g_mix[l], w_in[l], conv_w[l], conv_b[l], ssd_A_log[l], ssd_dt_bias[l], ssd_D[l],
                          ssd_g_out[l], att_g_q[l], att_g_k[l], w_out[l], g_mem_x[l], g_mem_src[l],
                          w_mq[l], w_mk[l], w_mv[l], g_mq[l], g_mk[l], w_mo[l], g_ffn[l], w_ff1[l], w_ff2[l])
        rel = att_rel_bias[l]
        mk, mv = _memory_kv(mem_prompt.reshape(b_p * N_MEM, D_MODEL), p["g_mem_src"], p["w_mk"], p["w_mv"],
                            p["g_mk"])
        mk = mk.reshape(b_p, N_MEM, D_MODEL)
        mv = mv.reshape(b_p, N_MEM, D_MODEL)
        conv0 = jnp.zeros((b_p, SSD_CONV - 1, SSD_CONV_DIM), F32)
        h00 = jnp.zeros((b_p, SSD_HEADS, SSD_HEAD_DIM, SSD_STATE), F32)
        yp, hp, cp, kp, vp = _layer(yp, conv0, h00, None, None, mk, mv, p, rel,
                                    tr=min(256, seq), t_scan=min(256, seq))
        ys, hs, cs, ks_, vs_ = _layer(ys, state_conv[l], state_ssd[l], cache_attn_k[l], cache_attn_v[l],
                                      cache_mem_k[l].reshape(b_s, N_MEM, D_MODEL),
                                      cache_mem_v[l].reshape(b_s, N_MEM, D_MODEL), p, rel,
                                      tr=b_s * dec_seq, t_scan=128)
        for lst, val in zip(outs, (hp, cp, kp, vp,
                                   mk.reshape(b_p, N_MEM, MEM_HEADS, MEM_HEAD_DIM),
                                   mv.reshape(b_p, N_MEM, MEM_HEADS, MEM_HEAD_DIM),
                                   hs, cs, ks_, vs_)):
            lst.append(val)
    return (yp, ys) + tuple(jnp.stack(o) for o in outs)
```

```python
import functools

import jax
import jax.numpy as jnp
from jax import lax
from jax.experimental import pallas as pl
from jax.experimental.pallas import tpu as pltpu

F32 = jnp.float32
BF16 = jnp.bfloat16

D_MODEL = 1024
CHUNK = 64
SSD_HEADS = 16
SSD_HEAD_DIM = 64
SSD_WIDTH = SSD_HEADS * SSD_HEAD_DIM
SSD_GROUPS = 2
SSD_STATE = 128
SSD_CONV = 4
SSD_CONV_DIM = SSD_WIDTH + 2 * SSD_GROUPS * SSD_STATE
ATT_HEADS = 16
ATT_HEAD_DIM = 64
ATT_WIDTH = ATT_HEADS * ATT_HEAD_DIM
N_LEFT_CHUNKS = 8
PAST = N_LEFT_CHUNKS * CHUNK
REL_CLIP = 128
N_MEM = 256
MEM_HEADS = 4
MEM_HEAD_DIM = D_MODEL // MEM_HEADS
D_FF = 4 * D_MODEL
EPS = 1e-6

LANES = 128
DT_PAD = LANES
NEG = -1e30
LOG2E = 1.4426950408889634
CONV_ROWS = 64
MIX_DENSE_ROWS = 256
IN_GROUP_ROWS = 256
SSD_PARTS = 4
VMEM_LIMIT = 56 * 1024 * 1024


def _rms(x, g):
    return x * lax.rsqrt(jnp.mean(x * x, axis=-1, keepdims=True) + EPS) * g


def _silu(x):
    h = 0.5 * x
    return h + h * jnp.tanh(h)


def _split2(x):
    hi = x.astype(BF16)
    lo = (x - hi.astype(F32)).astype(BF16)
    return hi, lo


def _split3(x):
    hi = x.astype(BF16)
    r = x - hi.astype(F32)
    mid = r.astype(BF16)
    lo = (r - mid.astype(F32)).astype(BF16)
    return hi, mid, lo


def _resident(shape):
    return pl.BlockSpec(shape, lambda *_: (0,) * len(shape), pipeline_mode=pl.Buffered(1))


def _dot(a, b):
    return jnp.dot(a, b, preferred_element_type=F32)


def _dot_nt(a, b):
    return lax.dot_general(a, b, (((1,), (1,)), ((), ())), preferred_element_type=F32)


W_PREP_ROWS = 128


def _split_win_kernel(w_ref, wzx_ref, wdt_ref, wqkv_ref):
    o_dt = SSD_WIDTH + SSD_CONV_DIM
    o_qkv = o_dt + SSD_HEADS
    wzx_ref[...] = w_ref[:, 0:o_dt].astype(BF16)
    lane = lax.broadcasted_iota(jnp.int32, (w_ref.shape[0], DT_PAD), 1)
    wdt_ref[...] = jnp.where(lane < SSD_HEADS, w_ref[:, o_dt:o_dt + DT_PAD], 0.0).astype(BF16)
    wqkv_ref[...] = w_ref[:, o_qkv:o_qkv + 3 * ATT_WIDTH].astype(BF16)


def _split_w_in(w_in):
    d, width = w_in.shape
    n = d // W_PREP_ROWS
    widths = (SSD_WIDTH + SSD_CONV_DIM, DT_PAD, 3 * ATT_WIDTH)
    return pl.pallas_call(
        _split_win_kernel,
        grid=(n,),
        in_specs=[pl.BlockSpec((W_PREP_ROWS, width), lambda i: (i, 0))],
        out_specs=[pl.BlockSpec((W_PREP_ROWS, w), lambda i: (i, 0)) for w in widths],
        out_shape=[jax.ShapeDtypeStruct((d, w), BF16) for w in widths],
        compiler_params=pltpu.CompilerParams(dimension_semantics=("arbitrary",),
                                             vmem_limit_bytes=VMEM_LIMIT),
        name="split_w_in",
    )(w_in)


def _inproj_kernel(x_ref, gmix_ref, wzx_ref, wdt_ref, wqkv_ref, gq_ref, gk_ref, cprev_ref,
                   convw_ref, convb_ref,
                   gz_ref, xc_ref, xtail_ref, dt_ref, q_ref, k_ref, v_ref, kt_ref, vt_ref, ext_ref, wvt_ref,
                   *, n_tail, v_feature_major, nb, tiles_per_stream):
    i = pl.program_id(0)
    n = pl.num_programs(0)
    tr = x_ref.shape[0]
    rb = tr // nb
    halo = 8
    if v_feature_major:
        @pl.when(i == 0)
        def _():
            wvt_ref[...] = wqkv_ref[:, 2 * ATT_WIDTH:].T
    grp = min(tr, IN_GROUP_ROWS)
    groups = [slice(r0, r0 + grp) for r0 in range(0, tr, grp)]
    xn = [_rms(x_ref[g, :], gmix_ref[...]).astype(BF16) for g in groups]

    def proj(gi, loc, width):
        w_ref, lo = loc
        return _dot(xn[gi], w_ref[:, lo:lo + width])

    o_xbc = (wzx_ref, SSD_WIDTH)
    o_dt = (wdt_ref, 0)
    o_q = (wqkv_ref, 0)
    o_k = (wqkv_ref, ATT_WIDTH)
    o_v = (wqkv_ref, 2 * ATT_WIDTH)
    lane = lax.broadcasted_iota(jnp.int32, (grp, LANES), 1)
    first = lane < ATT_HEAD_DIM

    def head_norm(blk, g_ref, c, scale):
        sq = blk * blk
        s0 = jnp.sum(jnp.where(first, sq, 0.0), axis=-1, keepdims=True)
        s1 = jnp.sum(jnp.where(first, 0.0, sq), axis=-1, keepdims=True)
        r = jnp.where(first, lax.rsqrt(s0 * (1.0 / ATT_HEAD_DIM) + EPS),
                      lax.rsqrt(s1 * (1.0 / ATT_HEAD_DIM) + EPS))
        return blk * r * (g_ref[:, c * LANES:(c + 1) * LANES] * scale)

    chunk = 2 * LANES
    work = []

    def run(count):
        for _ in range(min(count, len(work))):
            work.pop(0)()

    xbc = [proj(gi, o_xbc, SSD_CONV_DIM) for gi in range(len(groups))]

    def xbc_rows(r0, r1):
        gi = r0 // grp
        return xbc[gi][r0 - gi * grp:r1 - gi * grp, :]

    for s in range(nb):
        base = s * (rb + halo)
        if tiles_per_stream == 1:
            carried = cprev_ref[s]
        else:
            carried = jnp.where(i % tiles_per_stream == 0, cprev_ref[s], ext_ref[rb:rb + halo, :])
        ext_ref[base:base + halo, :] = carried
        piece = min(rb, grp)
        for r0 in range(0, rb, piece):
            ext_ref[base + halo + r0:base + halo + r0 + piece, :] = xbc_rows(s * rb + r0, s * rb + r0 + piece)
        xtail_ref[s] = xbc_rows((s + 1) * rb - halo, (s + 1) * rb)

        def conv_piece(s=s, base=base, cb=0, r0=0, nr=rb):
            sl = slice(cb * LANES, (cb + 1) * LANES)
            xe = ext_ref[base + r0:base + r0 + halo + nr, sl]
            acc = convb_ref[:, sl] + convw_ref[SSD_CONV - 1:SSD_CONV, sl] * xe[halo:, :]
            for j in range(1, SSD_CONV):
                tap = pltpu.roll(xe, j, 0)[halo:, :]
                acc = acc + convw_ref[SSD_CONV - 1 - j:SSD_CONV - j, sl] * tap
            xc_ref[s * rb + r0:s * rb + r0 + nr, sl] = _silu(acc)

        nr = min(rb, CONV_ROWS)
        for cb in range(SSD_CONV_DIM // LANES):
            for r0 in range(0, rb, nr):
                work.append(functools.partial(conv_piece, cb=cb, r0=r0, nr=nr))

    def gate(zc, c, g):
        gz_ref[g, c * LANES:(c + 1) * LANES] = _silu(zc)

    def norm_q(blk, c, g):
        q_ref[g, c * LANES:(c + 1) * LANES] = head_norm(
            blk, gq_ref, c, ATT_HEAD_DIM ** -0.5 * LOG2E).astype(BF16)

    def norm_k(blk, c, g):
        kn = head_norm(blk, gk_ref, c, 1.0)
        k_ref[g, c * LANES:(c + 1) * LANES] = kn.astype(BF16)
        kt_ref[g, c * LANES:(c + 1) * LANES] = kn

    def chunks(loc, width, consumer, per_chunk):
        w_ref, lo = loc
        for c0 in range(0, width, chunk):
            for gi, g in enumerate(groups):
                y = proj(gi, (w_ref, lo + c0), chunk)
                for t in range(chunk // LANES):
                    work.append(functools.partial(consumer, y[:, t * LANES:(t + 1) * LANES], c0 // LANES + t, g))
                run(per_chunk)

    per_chunk = -(-len(work) // (12 * len(groups))) + 2
    chunks((wzx_ref, 0), SSD_WIDTH, gate, per_chunk)
    chunks(o_q, ATT_WIDTH, norm_q, per_chunk)
    chunks(o_k, ATT_WIDTH, norm_k, per_chunk)
    for c0 in range(0, ATT_WIDTH, chunk):
        for gi, g in enumerate(groups):
            if v_feature_major:
                v_ref[c0:c0 + chunk, g] = _dot_nt(wvt_ref[c0:c0 + chunk, :], xn[gi]).astype(BF16)
            else:
                v = proj(gi, (wqkv_ref, 2 * ATT_WIDTH + c0), chunk)
                v_ref[g, c0:c0 + chunk] = v.astype(BF16)
                vt_ref[g, c0:c0 + chunk] = v
            run(3)
    run(len(work))
    for gi, g in enumerate(groups):
        dt_ref[g, :] = proj(gi, o_dt, DT_PAD)
    if v_feature_major:
        @pl.when(i >= n - n_tail)
        def _():
            for gi, g in enumerate(groups):
                vt_ref[g, :] = proj(gi, o_v, ATT_WIDTH)


def _in_proj(x2d, g_mix, w_zx, w_dt, w_qkv, gq_t, gk_t, cprev8, conv_w, conv_b, *, tr, seq_len,
             tail_rows, v_feature_major):
    rows = x2d.shape[0]
    n = rows // tr
    n_tail = tail_rows // tr
    nb = max(tr // seq_len, 1)
    tiles_per_stream = max(seq_len // tr, 1)
    n_streams = rows // seq_len
    assert n * tr == rows and n_tail * tr == tail_rows and cprev8.shape == (n_streams, 8, SSD_CONV_DIM)
    assert (tr // nb) % 8 == 0

    def row(width):
        return pl.BlockSpec((tr, width), lambda i: (i, 0))

    def const(shape):
        return pl.BlockSpec(shape, lambda i: (0,) * len(shape))

    per_stream = pl.BlockSpec((nb, 8, SSD_CONV_DIM), lambda i: (i // tiles_per_stream, 0, 0))
    tail = pl.BlockSpec((tr, ATT_WIDTH), lambda i: (jnp.maximum(i - (n - n_tail), 0), 0))
    if v_feature_major:
        v_spec = pl.BlockSpec((ATT_WIDTH, tr), lambda i: (0, i))
        v_shape = jax.ShapeDtypeStruct((ATT_WIDTH, rows), BF16)
    else:
        v_spec = row(ATT_WIDTH)
        v_shape = jax.ShapeDtypeStruct((rows, ATT_WIDTH), BF16)
    return pl.pallas_call(
        functools.partial(_inproj_kernel, n_tail=n_tail, v_feature_major=v_feature_major, nb=nb,
                          tiles_per_stream=tiles_per_stream),
        grid=(n,),
        in_specs=[row(D_MODEL), const((1, D_MODEL)), _resident((D_MODEL, SSD_WIDTH + SSD_CONV_DIM)),
                  _resident((D_MODEL, DT_PAD)), _resident((D_MODEL, 3 * ATT_WIDTH)),
                  const((1, ATT_WIDTH)), const((1, ATT_WIDTH)),
                  per_stream, const((SSD_CONV, SSD_CONV_DIM)), const((1, SSD_CONV_DIM))],
        out_specs=[row(SSD_WIDTH), row(SSD_CONV_DIM), per_stream, row(DT_PAD), row(ATT_WIDTH),
                   row(ATT_WIDTH), v_spec, tail, tail],
        out_shape=[jax.ShapeDtypeStruct((rows, SSD_WIDTH), F32),
                   jax.ShapeDtypeStruct((rows, SSD_CONV_DIM), F32),
                   jax.ShapeDtypeStruct((n_streams, 8, SSD_CONV_DIM), F32),
                   jax.ShapeDtypeStruct((rows, DT_PAD), F32),
                   jax.ShapeDtypeStruct((rows, ATT_WIDTH), BF16),
                   jax.ShapeDtypeStruct((rows, ATT_WIDTH), BF16),
                   v_shape,
                   jax.ShapeDtypeStruct((tail_rows, ATT_WIDTH), F32),
                   jax.ShapeDtypeStruct((tail_rows, ATT_WIDTH), F32)],
        scratch_shapes=[pltpu.VMEM((nb * (tr // nb + 8), SSD_CONV_DIM), F32),
                        pltpu.VMEM((ATT_WIDTH, D_MODEL) if v_feature_major else (16, LANES), BF16)],
        compiler_params=pltpu.CompilerParams(dimension_semantics=("arbitrary",),
                                             vmem_limit_bytes=VMEM_LIMIT),
        name="in_proj",
    )(x2d, g_mix, w_zx, w_dt, w_qkv, gq_t, gk_t, cprev8, conv_w, conv_b)


def _ssd_chunk(xc_ref, dtraw_ref, gz_ref, dtb_ref, alog_ref, dskip_ref, expand_ref, ht_ref, ybuf_ref,
               *, t, lb, row0=0):
    def rows_of(ref, sl):
        v = ref[0, row0:row0 + lb, sl]
        if lb < t:
            v = jnp.concatenate([v, jnp.zeros((t - lb, v.shape[1]), v.dtype)], axis=0)
        return v

    lane = lax.broadcasted_iota(jnp.int32, (t, LANES), 1)
    rowi = lax.broadcasted_iota(jnp.int32, (t, LANES), 0)
    dt = jax.nn.softplus(rows_of(dtraw_ref, slice(None)) + dtb_ref[...])
    dt = jnp.where((lane < SSD_HEADS) & (rowi < lb), dt, 0.0)
    a_neg = -jnp.exp(alog_ref[...]) * LOG2E
    a = dt * a_neg

    rr = lax.broadcasted_iota(jnp.int32, (t, t), 0)
    cc = lax.broadcasted_iota(jnp.int32, (t, t), 1)
    causal = rr >= cc
    tril = jnp.where(causal, 1.0, 0.0).astype(BF16)
    a1, a2, a3 = _split3(a)
    a_cum = _dot(tril, a1) + _dot(tril, a2) + _dot(tril, a3)
    a_last = a_cum[t - 1:t, :]
    ea = jnp.exp2(a_cum)
    cd = jnp.exp2(a_last)
    a_t = a_cum.T
    w_t = jnp.exp2(a_t[:, t - 1:t] - a_t)

    stacked = jnp.concatenate([ea, dt, jnp.broadcast_to(cd, (8, LANES))], axis=0)
    s_hi, s_lo = _split2(stacked)
    expanded = _dot(s_hi, expand_ref[...]) + _dot(s_lo, expand_ref[...])
    ea_x = expanded[0:t, :]
    dt_x = expanded[t:2 * t, :]
    cd_x = expanded[2 * t:2 * t + 1, :]

    first = lane < SSD_HEAD_DIM
    heads_per_group = SSD_HEADS // SSD_GROUPS
    o_b = SSD_WIDTH
    o_c = SSD_WIDTH + SSD_GROUPS * SSD_STATE
    ssq = [jnp.zeros((lb, 1), F32)]
    per_group = {}

    def group_values(g):
        if g not in per_group:
            bg = rows_of(xc_ref, slice(o_b + g * SSD_STATE, o_b + (g + 1) * SSD_STATE))
            cg = rows_of(xc_ref, slice(o_c + g * SSD_STATE, o_c + (g + 1) * SSD_STATE)).astype(BF16)
            cb_mat = jnp.where(causal, _dot_nt(cg, bg.astype(BF16)), 0.0)
            per_group[g] = (cg, cb_mat, bg.T)
        return per_group[g]

    def prepare(j):
        cg, cb_mat, bg_t = group_values(j // (heads_per_group // 2))
        sl = slice(j * LANES, (j + 1) * LANES)
        xh = rows_of(xc_ref, sl)
        xdt = xh * dt_x[:, sl]
        ops = []
        for hh in range(2):
            h = 2 * j + hh
            xm = jnp.where(first if hh == 0 else jnp.logical_not(first), xdt, 0.0).astype(BF16)
            seg = a_cum[:, h:h + 1] - a_t[h:h + 1, :]
            m = (cb_mat * jnp.exp2(jnp.minimum(seg, 0.0))).astype(BF16)
            bw = (bg_t * w_t[h:h + 1, :]).astype(BF16)
            ops.append((m, bw, xm))
        return cg, xh, ops

    def finish(j, prepared):
        cg, xh, ops = prepared
        sl = slice(j * LANES, (j + 1) * LANES)
        y_pair = _dot(ops[0][0], ops[0][2]) + _dot(ops[1][0], ops[1][2])
        s_pair = _dot(ops[0][1], ops[0][2]) + _dot(ops[1][1], ops[1][2])
        h_in = ht_ref[:, sl]
        y_off = _dot(cg, h_in.astype(BF16)) * ea_x[:, sl]
        ht_ref[:, sl] = cd_x[:, sl] * h_in + s_pair
        y = y_pair + y_off + dskip_ref[:, sl] * xh
        yg = y[0:lb, :] * gz_ref[0, row0:row0 + lb, sl]
        ybuf_ref[row0:row0 + lb, sl] = yg
        ssq[0] = ssq[0] + jnp.sum(yg * yg, axis=-1, keepdims=True)

    def scan():
        n_tiles = SSD_WIDTH // LANES
        ahead = prepare(0)
        for j in range(n_tiles):
            cur = ahead
            if j + 1 < n_tiles:
                ahead = prepare(j + 1)
            finish(j, cur)
        return ssq[0]

    return scan


def _ssd_kernel(xc_ref, dtraw_ref, gz_ref, h0_ref, dtb_ref, alog_ref, dskip_ref, gout_ref, expand_ref,
                y_ref, hfin_ref,
                ht_ref, ybuf_ref, *, t, lb, parts):
    c = pl.program_id(1)
    nc = pl.num_programs(1)

    @pl.when(c == 0)
    def _():
        ht_ref[...] = h0_ref[0].T

    scans = [_ssd_chunk(xc_ref, dtraw_ref, gz_ref, dtb_ref, alog_ref, dskip_ref, expand_ref, ht_ref,
                        ybuf_ref, t=t, lb=lb, row0=part * lb) for part in range(parts)]
    for part, scan in enumerate(scans):
        rows = slice(part * lb, (part + 1) * lb)
        r = lax.rsqrt(scan() * (1.0 / SSD_WIDTH) + EPS)
        y_ref[0, rows, :] = (ybuf_ref[rows, :] * r * gout_ref[...]).astype(BF16)

    @pl.when(c == nc - 1)
    def _():
        hfin_ref[0] = ht_ref[...].T


def _ssd_mixer(xc, dtraw, gz, h0, dtb, alog, dskip, gout, expand, *, t, lb):
    b, length, _ = xc.shape
    n_chunks = length // lb
    assert n_chunks * lb == length and (lb == t or n_chunks == 1)
    parts = SSD_PARTS if n_chunks % SSD_PARTS == 0 else 1
    nc = n_chunks // parts

    def seq(width):
        return pl.BlockSpec((1, parts * lb, width), lambda bi, ci: (bi, ci, 0))

    def per_b(shape):
        return pl.BlockSpec((1,) + shape, lambda bi, ci: (bi, 0, 0))

    def const(shape):
        return pl.BlockSpec(shape, lambda bi, ci: (0,) * len(shape))

    hp = SSD_WIDTH
    return pl.pallas_call(
        functools.partial(_ssd_kernel, t=t, lb=lb, parts=parts),
        grid=(b, nc),
        in_specs=[seq(SSD_CONV_DIM), seq(DT_PAD), seq(SSD_WIDTH), per_b((hp, SSD_STATE)),
                  const((1, DT_PAD)), const((1, DT_PAD)), const((1, SSD_WIDTH)), const((1, SSD_WIDTH)),
                  const((LANES, SSD_WIDTH))],
        out_specs=[seq(SSD_WIDTH), per_b((hp, SSD_STATE))],
        out_shape=[jax.ShapeDtypeStruct((b, length, SSD_WIDTH), BF16),
                   jax.ShapeDtypeStruct((b, hp, SSD_STATE), F32)],
        scratch_shapes=[pltpu.VMEM((SSD_STATE, hp), F32),
                        pltpu.VMEM((parts * lb, SSD_WIDTH), F32)],
        compiler_params=pltpu.CompilerParams(dimension_semantics=("arbitrary", "arbitrary"),
                                             vmem_limit_bytes=VMEM_LIMIT),
        name="ssd_mixer",
    )(xc, dtraw, gz, h0, dtb, alog, dskip, gout, expand)


ATT_SB = 2 * CHUNK
ATT_WIN = ATT_SB + PAST
ATT_PER_TRIP = 8


def _toeplitz_rows(base_row, rows, offset, width):
    ext = base_row.shape[1]
    rolled = pltpu.roll(jnp.broadcast_to(base_row, (rows, ext)), 0, 1, stride=1, stride_axis=0)
    return rolled[:, offset:offset + width]


def _band_subblock(q, kw, vt, bias_t, first):
    return _band_softmax_pv(_band_scores(q, kw, bias_t, first), vt)


def _band_scores(q, kw, bias_t, first):
    zero = jnp.zeros_like(q)
    q2 = jnp.concatenate([jnp.where(first, q, zero), jnp.where(first, zero, q)], axis=0)
    return _dot_nt(kw, q2) + bias_t


def _band_softmax(s):
    m = jnp.max(s, axis=0, keepdims=True)
    p = jnp.exp2(s - m)
    return p.astype(BF16), 1.0 / jnp.sum(p, axis=0, keepdims=True)


def _band_pv(p, inv_l, vt):
    sb = p.shape[1] // 2
    half = ATT_HEAD_DIM
    o_t = _dot(vt, p)
    o_pair_t = jnp.concatenate([o_t[0:half, 0:sb] * inv_l[:, 0:sb],
                                o_t[half:2 * half, sb:2 * sb] * inv_l[:, sb:2 * sb]], axis=0)
    return o_pair_t.T.astype(BF16)


def _band_softmax_pv(s, vt):
    p, inv_l = _band_softmax(s)
    return _band_pv(p, inv_l, vt)


def _band_prompt_kernel(q_ref, k_ref, vt_ref, base_ref, o_ref, bias_ref, sc_ref, pb_ref, il_ref):
    length = q_ref.shape[0]
    sb, w = ATT_SB, ATT_WIN
    kj = lax.broadcasted_iota(jnp.int32, (w, sb), 0) // CHUNK
    qi = lax.broadcasted_iota(jnp.int32, (w, sb), 1) // CHUNK
    in_band = (kj >= qi) & (kj <= qi + N_LEFT_CHUNKS)
    for hh in range(2):
        toe = _toeplitz_rows(base_ref[0, hh:hh + 1, :], w, w, sb)
        bias_ref[:, hh * sb:(hh + 1) * sb] = jnp.where(in_band, toe * LOG2E, NEG)

    lane = lax.broadcasted_iota(jnp.int32, (sb, LANES), 1)
    first = lane < ATT_HEAD_DIM
    n_sub = length // sb
    n_head = min(PAST // sb, n_sub)
    for j in range(n_head):
        n = (j + 1) * sb
        o_ref[j * sb:(j + 1) * sb, :] = _band_subblock(
            q_ref[j * sb:(j + 1) * sb, :], k_ref[0:n, :], vt_ref[:, 0:n], bias_ref[w - n:, :], first)

    n_main = n_sub - n_head
    if n_main == 0:
        return
    assert n_main % 2 == 0

    def scores_into(slot, j):
        r0 = pl.multiple_of(j * sb, sb)
        k0 = pl.multiple_of(j * sb - PAST, sb)
        sc_ref[slot] = _band_scores(q_ref[pl.ds(r0, sb), :], k_ref[pl.ds(k0, w), :], bias_ref[...], first)

    def softmax_into(slot):
        p, inv_l = _band_softmax(sc_ref[slot])
        pb_ref[slot] = p
        il_ref[slot] = jnp.broadcast_to(inv_l, (8, 2 * sb))

    def values_out(slot, j):
        k0 = pl.multiple_of(j * sb - PAST, sb)
        return _band_pv(pb_ref[slot], il_ref[slot, 0:1, :], vt_ref[:, pl.ds(k0, w)])

    last = n_sub - 1
    scores_into(0, n_head)
    scores_into(1, n_head + 1)
    softmax_into(0)

    def trip(per_trip, base):
        def body(i, carry):
            outs = []
            for u in range(per_trip):
                j = base + per_trip * i + u
                slot = u % 2
                scores_into(slot, jnp.minimum(j + 2, last))
                softmax_into(1 - slot)
                outs.append((j, values_out(slot, j)))
            for j, o in outs:
                o_ref[pl.ds(pl.multiple_of(j * sb, sb), sb), :] = o
            return carry
        return body

    n_long = n_main // ATT_PER_TRIP
    if n_long:
        lax.fori_loop(0, n_long, trip(ATT_PER_TRIP, n_head), 0)
    rest = n_main - n_long * ATT_PER_TRIP
    if rest:
        trip(rest, n_head + n_long * ATT_PER_TRIP)(0, 0)


def _band_prompt(q, k, vt, base):
    length = q.shape[0]
    assert length % ATT_SB == 0
    ext = ATT_SB + ATT_WIN
    col = pl.BlockSpec((length, LANES), lambda c: (0, c))
    return pl.pallas_call(
        _band_prompt_kernel,
        grid=(ATT_WIDTH // LANES,),
        in_specs=[col, col, pl.BlockSpec((LANES, length), lambda c: (c, 0)),
                  pl.BlockSpec((1, 2, ext), lambda c: (c, 0, 0))],
        out_specs=col,
        out_shape=jax.ShapeDtypeStruct((length, ATT_WIDTH), BF16),
        scratch_shapes=[pltpu.VMEM((ATT_WIN, 2 * ATT_SB), F32),
                        pltpu.VMEM((2, ATT_WIN, 2 * ATT_SB), F32),
                        pltpu.VMEM((2, ATT_WIN, 2 * ATT_SB), BF16),
                        pltpu.VMEM((2, 8, 2 * ATT_SB), F32)],
        compiler_params=pltpu.CompilerParams(dimension_semantics=("arbitrary",),
                                             vmem_limit_bytes=VMEM_LIMIT),
        name="band_prompt",
    )(q, k, vt, base.reshape(ATT_HEADS // 2, 2, ext))


def _band_sample_kernel(q_ref, kn_ref, vn_ref, kc_ref, vc_ref, base_ref, o_ref, bias_ref):
    lq = q_ref.shape[0]
    lc = kc_ref.shape[3]

    @pl.when(pl.program_id(0) == 0)
    def _():
        for h in range(ATT_HEADS):
            bias_ref[h] = _toeplitz_rows(base_ref[h:h + 1, :], lq, LANES, lc + LANES) * LOG2E

    for h in range(ATT_HEADS):
        sl = slice(h * ATT_HEAD_DIM, (h + 1) * ATT_HEAD_DIM)
        q = q_ref[:, sl]
        sc = _dot(q, kc_ref[0, h]) + bias_ref[h, :, 0:lc]
        sn = _dot_nt(q, kn_ref[:, sl]) + bias_ref[h, :, lc:lc + lq]
        m = jnp.maximum(jnp.max(sc, axis=-1, keepdims=True), jnp.max(sn, axis=-1, keepdims=True))
        pc = jnp.exp2(sc - m)
        pn = jnp.exp2(sn - m)
        l = jnp.sum(pc, axis=-1, keepdims=True) + jnp.sum(pn, axis=-1, keepdims=True)
        o = _dot_nt(pc.astype(BF16), vc_ref[0, h]) + _dot(pn.astype(BF16), vn_ref[:, sl])
        o_ref[:, sl] = (o * (1.0 / l)).astype(BF16)


def _band_sample(q, kn, vn, kc, vc, base, *, lq):
    rows = q.shape[0]
    nb = rows // lq
    lc = kc.shape[3]
    assert lq <= LANES and lc % LANES == 0 and base.shape == (ATT_HEADS, lc + 2 * LANES)
    blk = pl.BlockSpec((lq, ATT_WIDTH), lambda b: (b, 0))
    cache = pl.BlockSpec((1, ATT_HEADS, ATT_HEAD_DIM, lc), lambda b: (b, 0, 0, 0))
    return pl.pallas_call(
        _band_sample_kernel,
        grid=(nb,),
        in_specs=[blk, blk, blk, cache, cache, pl.BlockSpec(base.shape, lambda b: (0, 0))],
        out_specs=blk,
        out_shape=jax.ShapeDtypeStruct((rows, ATT_WIDTH), BF16),
        scratch_shapes=[pltpu.VMEM((ATT_HEADS, lq, lc + LANES), F32)],
        compiler_params=pltpu.CompilerParams(dimension_semantics=("arbitrary",),
                                             vmem_limit_bytes=VMEM_LIMIT),
        name="band_sample",
    )(q, kn, vn, kc, vc, base)


def _mem_head_norm(y, g_ref, hd, scale):
    blk = y[:, hd * MEM_HEAD_DIM:(hd + 1) * MEM_HEAD_DIM]
    r = lax.rsqrt(jnp.mean(blk * blk, axis=-1, keepdims=True) + EPS)
    return blk * r * (g_ref[...] * scale)


def _memkv_kernel(mem_ref, gsrc_ref, wk_ref, wv_ref, gk_ref, k_ref, v_ref):
    m = _rms(mem_ref[...], gsrc_ref[...]).astype(BF16)
    k = _dot(m, wk_ref[...])
    for hd in range(MEM_HEADS):
        k_ref[:, hd * MEM_HEAD_DIM:(hd + 1) * MEM_HEAD_DIM] = _mem_head_norm(k, gk_ref, hd, 1.0)
    v_ref[...] = _dot(m, wv_ref[...])


def _memory_kv(mem2d, g_src, w_mk, w_mv, g_mk):
    rows = mem2d.shape[0]

    def full(shape):
        return pl.BlockSpec(shape, lambda i: (0,) * len(shape))

    return pl.pallas_call(
        _memkv_kernel,
        grid=(1,),
        in_specs=[full((rows, D_MODEL)), full((1, D_MODEL)), full((D_MODEL, D_MODEL)),
                  full((D_MODEL, D_MODEL)), full((1, MEM_HEAD_DIM))],
        out_specs=[full((rows, D_MODEL)), full((rows, D_MODEL))],
        out_shape=[jax.ShapeDtypeStruct((rows, D_MODEL), F32)] * 2,
        compiler_params=pltpu.CompilerParams(dimension_semantics=("arbitrary",),
                                             vmem_limit_bytes=VMEM_LIMIT),
        name="memory_kv",
    )(mem2d, g_src, w_mk, w_mv, g_mk)


FF_SLAB = 1024


def _mix_ffn_kernel(x_ref, ys_ref, ya_ref, wo_ref, gmx_ref, wq_ref, gmq_ref, mk_ref, mv_ref,
                    wmo_ref, gffn_ref, w1_ref, w2_ref, y_ref, obuf_ref, *, nb, rb, dense_rows):
    tr = x_ref.shape[0]
    groups = [slice(r0, r0 + dense_rows) for r0 in range(0, tr, dense_rows)]
    h = [x_ref[g, :] + _dot(ys_ref[g, :], wo_ref[0:SSD_WIDTH, :]) + _dot(ya_ref[g, :], wo_ref[SSD_WIDTH:, :])
         for g in groups]
    hn = [_rms(hg, gmx_ref[...]).astype(BF16) for hg in h]
    q = [_dot(hg, wq_ref[...]) for hg in hn]
    for hd in range(MEM_HEADS):
        sl = slice(hd * MEM_HEAD_DIM, (hd + 1) * MEM_HEAD_DIM)
        qn = [_mem_head_norm(qg, gmq_ref, hd, MEM_HEAD_DIM ** -0.5 * LOG2E).astype(BF16) for qg in q]
        for g, qg in zip(groups, qn):
            for r0 in range(g.start, g.stop, rb):
                b = r0 // rb if nb > 1 else 0
                s = _dot_nt(qg[r0 - g.start:r0 - g.start + min(rb, dense_rows), :], mk_ref[b, :, sl])
                p = jnp.exp2(s - jnp.max(s, axis=-1, keepdims=True))
                o = _dot(p.astype(BF16), mv_ref[b, :, sl]) * (1.0 / jnp.sum(p, axis=-1, keepdims=True))
                obuf_ref[r0:r0 + min(rb, dense_rows), sl] = o.astype(BF16)
    h = [hg + _dot(obuf_ref[g, :], wmo_ref[...]) for g, hg in zip(groups, h)]
    hn = [_rms(hg, gffn_ref[...]).astype(BF16) for hg in h]
    for s in range(D_FF // FF_SLAB):
        for gi in range(len(groups)):
            u = jnp.maximum(_dot(hn[gi], w1_ref[:, s * FF_SLAB:(s + 1) * FF_SLAB]), 0.0)
            h[gi] = h[gi] + _dot((u * u).astype(BF16), w2_ref[s * FF_SLAB:(s + 1) * FF_SLAB, :])
    for g, hg in zip(groups, h):
        y_ref[g, :] = hg


def _mix_ffn(x2d, ys, ya, w_out, g_mem_x, w_mq, g_mq, mk, mv, w_mo, g_ffn, w1, w2, *, tr, seq_len):
    rows = x2d.shape[0]
    n = rows // tr
    nb = max(tr // seq_len, 1)
    tiles_per_stream = max(seq_len // tr, 1)
    rb = tr // nb
    assert n * tr == rows and nb * rb == tr and mk.shape[0] * seq_len == rows

    def row(width):
        return pl.BlockSpec((tr, width), lambda i: (i, 0))

    def const(shape):
        return pl.BlockSpec(shape, lambda i: (0,) * len(shape))

    mem = pl.BlockSpec((nb, N_MEM, D_MODEL), lambda i: (i // tiles_per_stream, 0, 0))
    return pl.pallas_call(
        functools.partial(_mix_ffn_kernel, nb=nb, rb=rb, dense_rows=min(tr, MIX_DENSE_ROWS)),
        grid=(n,),
        in_specs=[row(D_MODEL), row(SSD_WIDTH), row(ATT_WIDTH), _resident((SSD_WIDTH + ATT_WIDTH, D_MODEL)),
                  const((1, D_MODEL)), _resident((D_MODEL, D_MODEL)), const((1, MEM_HEAD_DIM)), mem, mem,
                  _resident((D_MODEL, D_MODEL)), const((1, D_MODEL)), _resident((D_MODEL, D_FF)),
                  _resident((D_FF, D_MODEL))],
        out_specs=row(D_MODEL),
        out_shape=jax.ShapeDtypeStruct((rows, D_MODEL), F32),
        scratch_shapes=[pltpu.VMEM((tr, D_MODEL), BF16)],
        compiler_params=pltpu.CompilerParams(dimension_semantics=("arbitrary",),
                                             vmem_limit_bytes=VMEM_LIMIT),
        name="mix_ffn",
    )(x2d, ys, ya, w_out, g_mem_x, w_mq, g_mq, mk, mv, w_mo, g_ffn, w1, w2)


def _toeplitz_base(table, offset, width):
    heads, size = table.shape
    n_far = offset - REL_CLIP
    assert n_far >= 0
    parts = [jnp.broadcast_to(table[:, size - 1:], (heads, n_far)), table[:, ::-1]]
    rest = width - n_far - size
    if rest > 0:
        parts.append(jnp.broadcast_to(table[:, :1], (heads, rest)))
    return jnp.concatenate(parts, axis=1)[:, :width].astype(F32)


def _toeplitz_base_t(table, shift, width):
    heads, size = table.shape
    n_low = shift - REL_CLIP
    assert n_low >= 0
    parts = [jnp.broadcast_to(table[:, :1], (heads, n_low)), table]
    rest = width - n_low - size
    if rest > 0:
        parts.append(jnp.broadcast_to(table[:, size - 1:], (heads, rest)))
    return jnp.concatenate(parts, axis=1)[:, :width].astype(F32)


def _prep_weights(g_mix, w_in, conv_w, conv_b, ssd_A_log, ssd_dt_bias, ssd_D, ssd_g_out, att_g_q, att_g_k,
                  w_out, g_mem_x, g_mem_src, w_mq, w_mk, w_mv, g_mq, g_mk, w_mo, g_ffn, w_ff1, w_ff2):
    w_zx, w_dt, w_qkv = _split_w_in(w_in)
    pad_h = lambda v: jnp.pad(v, (0, DT_PAD - SSD_HEADS)).reshape(1, DT_PAD)
    row = lambda v: v.reshape(1, -1)
    expand = (jnp.arange(LANES)[:, None] == jnp.arange(SSD_WIDTH)[None, :] // SSD_HEAD_DIM).astype(BF16)
    return dict(
        g_mix=row(g_mix), w_zx=w_zx, w_dt=w_dt, w_qkv=w_qkv,
        gq_t=row(jnp.tile(att_g_q, ATT_HEADS)), gk_t=row(jnp.tile(att_g_k, ATT_HEADS)),
        conv_w=conv_w, conv_b=row(conv_b), dtb=pad_h(ssd_dt_bias), alog=pad_h(ssd_A_log),
        dskip=row(jnp.repeat(ssd_D, SSD_HEAD_DIM)), gout=row(ssd_g_out), expand=expand,
        w_out=w_out.astype(BF16), g_mem_x=row(g_mem_x), g_mem_src=row(g_mem_src),
        w_mq=w_mq.astype(BF16), w_mk=w_mk.astype(BF16), w_mv=w_mv.astype(BF16),
        g_mq=row(g_mq), g_mk=row(g_mk), w_mo=w_mo.astype(BF16), g_ffn=row(g_ffn),
        w_ff1=w_ff1.astype(BF16), w_ff2=w_ff2.astype(BF16))


def _layer(x, conv_prev, h0, k_cache, v_cache, mem_k, mem_v, p, rel, *, tr, t_scan):
    b, length, _ = x.shape
    rows = b * length
    x2d = x.reshape(rows, D_MODEL)
    prompt = k_cache is None
    tail_rows = PAST if prompt else rows
    cprev8 = jnp.pad(conv_prev, ((0, 0), (8 - (SSD_CONV - 1), 0), (0, 0)))
    tr_in = 2 * tr if prompt and rows % (2 * tr) == 0 and tail_rows % (2 * tr) == 0 else tr
    gz, xc, xtail, dtraw, q, k, v, k_tail, v_tail = _in_proj(
        x2d, p["g_mix"], p["w_zx"], p["w_dt"], p["w_qkv"], p["gq_t"], p["gk_t"], cprev8,
        p["conv_w"], p["conv_b"], tr=tr_in, seq_len=length, tail_rows=tail_rows, v_feature_major=prompt)
    conv_new = xtail[:, 8 - (SSD_CONV - 1):]

    scan_args = (xc.reshape(b, length, SSD_CONV_DIM), dtraw.reshape(b, length, DT_PAD),
                 gz.reshape(b, length, SSD_WIDTH), h0.reshape(b, SSD_WIDTH, SSD_STATE),
                 p["dtb"], p["alog"], p["dskip"], p["gout"], p["expand"])
    mix_args = (p["w_out"], p["g_mem_x"], p["w_mq"], p["g_mq"], mem_k.astype(BF16), mem_v.astype(BF16),
                p["w_mo"], p["g_ffn"], p["w_ff1"], p["w_ff2"])
    y_ssd, h_fin = _ssd_mixer(*scan_args, t=t_scan, lb=t_scan if prompt else length)
    if prompt:
        y_att = _band_prompt(q, k, v, _toeplitz_base_t(rel, ATT_SB, ATT_SB + ATT_WIN))
        k_rows = k_tail.reshape(b, PAST, ATT_HEADS, ATT_HEAD_DIM)
        v_rows = v_tail.reshape(b, PAST, ATT_HEADS, ATT_HEAD_DIM)
    else:
        lc = k_cache.shape[1]
        y_att = _band_sample(q, k, v, jnp.transpose(k_cache, (0, 2, 3, 1)).astype(BF16),
                             jnp.transpose(v_cache, (0, 2, 3, 1)).astype(BF16),
                             _toeplitz_base(rel, lc + LANES, lc + 2 * LANES), lq=length)
        k_rows = k_tail.reshape(b, length, ATT_HEADS, ATT_HEAD_DIM)
        v_rows = v_tail.reshape(b, length, ATT_HEADS, ATT_HEAD_DIM)
    tr_mix = 2 * tr if prompt and rows % (2 * tr) == 0 else tr
    y = _mix_ffn(x2d, y_ssd.reshape(rows, SSD_WIDTH), y_att, *mix_args, tr=tr_mix, seq_len=length)
    return (y.reshape(b, length, D_MODEL), h_fin.reshape(b, SSD_HEADS, SSD_HEAD_DIM, SSD_STATE),
            conv_new, k_rows, v_rows)


def kernel(x_prompt, x_sample, mem_prompt, state_ssd, state_conv, cache_attn_k, cache_attn_v, cache_mem_k,
           cache_mem_v, g_mix, w_in, conv_w, conv_b, ssd_A_log, ssd_dt_bias, ssd_D, ssd_g_out, att_g_q,
           att_g_k, att_rel_bias, w_out, g_mem_x, g_mem_src, w_mq, w_mk, w_mv, g_mq, g_mk, w_mo, g_ffn,
           w_ff1, w_ff2):
    depth = g_mix.shape[0]
    b_p, seq, _ = x_prompt.shape
    b_s, dec_seq, _ = x_sample.shape
    yp, ys = x_prompt, x_sample
    outs = [[] for _ in range(10)]
    for l in range(depth):
        p = _prep_weights(g_mix[l], w_in[l], conv_w[l], conv_b[l], ssd_A_log[l], ssd_dt_bias[l], ssd_D[l],
                          ssd_g_out[l], att_g_q[l], att_g_k[l], w_out[l], g_mem_x[l], g_mem_src[l],
                          w_mq[l], w_mk[l], w_mv[l], g_mq[l], g_mk[l], w_mo[l], g_ffn[l], w_ff1[l], w_ff2[l])
        rel = att_rel_bias[l]
        mk, mv = _memory_kv(mem_prompt.reshape(b_p * N_MEM, D_MODEL), p["g_mem_src"], p["w_mk"], p["w_mv"],
                            p["g_mk"])
        mk = mk.reshape(b_p, N_MEM, D_MODEL)
        mv = mv.reshape(b_p, N_MEM, D_MODEL)
        conv0 = jnp.zeros((b_p, SSD_CONV - 1, SSD_CONV_DIM), F32)
        h00 = jnp.zeros((b_p, SSD_HEADS, SSD_HEAD_DIM, SSD_STATE), F32)
        yp, hp, cp, kp, vp = _layer(yp, conv0, h00, None, None, mk, mv, p, rel,
                                    tr=min(256, seq), t_scan=min(256, seq))
        ys, hs, cs, ks_, vs_ = _layer(ys, state_conv[l], state_ssd[l], cache_attn_k[l], cache_attn_v[l],
                                      cache_mem_k[l].reshape(b_s, N_MEM, D_MODEL),
                                      cache_mem_v[l].reshape(b_s, N_MEM, D_MODEL), p, rel,
                                      tr=b_s * dec_seq, t_scan=128)
        for lst, val in zip(outs, (hp, cp, kp, vp,
                                   mk.reshape(b_p, N_MEM, MEM_HEADS, MEM_HEAD_DIM),
                                   mv.reshape(b_p, N_MEM, MEM_HEADS, MEM_HEAD_DIM),
                                   hs, cs, ks_, vs_)):
            lst.append(val)
    return (yp, ys) + tuple(jnp.stack(o) for o in outs)
```

```python
import functools

import jax
import jax.numpy as jnp
from jax import lax
from jax.experimental import pallas as pl
from jax.experimental.pallas import tpu as pltpu

F32 = jnp.float32
BF16 = jnp.bfloat16

D_MODEL = 1024
CHUNK = 64
SSD_HEADS = 16
SSD_HEAD_DIM = 64
SSD_WIDTH = SSD_HEADS * SSD_HEAD_DIM
SSD_GROUPS = 2
SSD_STATE = 128
SSD_CONV = 4
SSD_CONV_DIM = SSD_WIDTH + 2 * SSD_GROUPS * SSD_STATE
ATT_HEADS = 16
ATT_HEAD_DIM = 64
ATT_WIDTH = ATT_HEADS * ATT_HEAD_DIM
N_LEFT_CHUNKS = 8
PAST = N_LEFT_CHUNKS * CHUNK
REL_CLIP = 128
N_MEM = 256
MEM_HEADS = 4
MEM_HEAD_DIM = D_MODEL // MEM_HEADS
D_FF = 4 * D_MODEL
EPS = 1e-6

LANES = 128
DT_PAD = LANES
NEG = -1e30
LOG2E = 1.4426950408889634
CONV_ROWS = 64
MIX_DENSE_ROWS = 256
IN_GROUP_ROWS = 256
SSD_PARTS = 4
VMEM_LIMIT = 56 * 1024 * 1024


def _rms(x, g):
    return x * lax.rsqrt(jnp.mean(x * x, axis=-1, keepdims=True) + EPS) * g


def _silu(x):
    h = 0.5 * x
    return h + h * jnp.tanh(h)


def _split2(x):
    hi = x.astype(BF16)
    lo = (x - hi.astype(F32)).astype(BF16)
    return hi, lo


def _split3(x):
    hi = x.astype(BF16)
    r = x - hi.astype(F32)
    mid = r.astype(BF16)
    lo = (r - mid.astype(F32)).astype(BF16)
    return hi, mid, lo


def _resident(shape):
    return pl.BlockSpec(shape, lambda *_: (0,) * len(shape), pipeline_mode=pl.Buffered(1))


def _dot(a, b):
    return jnp.dot(a, b, preferred_element_type=F32)


def _dot_nt(a, b):
    return lax.dot_general(a, b, (((1,), (1,)), ((), ())), preferred_element_type=F32)


W_PREP_ROWS = 128


def _split_win_kernel(w_ref, wzx_ref, wdt_ref, wqkv_ref):
    o_dt = SSD_WIDTH + SSD_CONV_DIM
    o_qkv = o_dt + SSD_HEADS
    wzx_ref[...] = w_ref[:, 0:o_dt].astype(BF16)
    lane = lax.broadcasted_iota(jnp.int32, (w_ref.shape[0], DT_PAD), 1)
    wdt_ref[...] = jnp.where(lane < SSD_HEADS, w_ref[:, o_dt:o_dt + DT_PAD], 0.0).astype(BF16)
    wqkv_ref[...] = w_ref[:, o_qkv:o_qkv + 3 * ATT_WIDTH].astype(BF16)


def _split_w_in(w_in):
    d, width = w_in.shape
    n = d // W_PREP_ROWS
    widths = (SSD_WIDTH + SSD_CONV_DIM, DT_PAD, 3 * ATT_WIDTH)
    return pl.pallas_call(
        _split_win_kernel,
        grid=(n,),
        in_specs=[pl.BlockSpec((W_PREP_ROWS, width), lambda i: (i, 0))],
        out_specs=[pl.BlockSpec((W_PREP_ROWS, w), lambda i: (i, 0)) for w in widths],
        out_shape=[jax.ShapeDtypeStruct((d, w), BF16) for w in widths],
        compiler_params=pltpu.CompilerParams(dimension_semantics=("arbitrary",),
                                             vmem_limit_bytes=VMEM_LIMIT),
        name="split_w_in",
    )(w_in)


def _inproj_kernel(x_ref, gmix_ref, wzx_ref, wdt_ref, wqkv_ref, gq_ref, gk_ref, cprev_ref,
                   convw_ref, convb_ref,
                   gz_ref, xc_ref, xtail_ref, dt_ref, q_ref, k_ref, v_ref, kt_ref, vt_ref, ext_ref, wvt_ref,
                   *, n_tail, v_feature_major, nb, tiles_per_stream):
    i = pl.program_id(0)
    n = pl.num_programs(0)
    tr = x_ref.shape[0]
    rb = tr // nb
    halo = 8
    if v_feature_major:
        @pl.when(i == 0)
        def _():
            wvt_ref[...] = wqkv_ref[:, 2 * ATT_WIDTH:].T
    grp = min(tr, IN_GROUP_ROWS)
    groups = [slice(r0, r0 + grp) for r0 in range(0, tr, grp)]
    xn = [_rms(x_ref[g, :], gmix_ref[...]).astype(BF16) for g in groups]

    def proj(gi, loc, width):
        w_ref, lo = loc
        return _dot(xn[gi], w_ref[:, lo:lo + width])

    o_xbc = (wzx_ref, SSD_WIDTH)
    o_dt = (wdt_ref, 0)
    o_q = (wqkv_ref, 0)
    o_k = (wqkv_ref, ATT_WIDTH)
    o_v = (wqkv_ref, 2 * ATT_WIDTH)
    lane = lax.broadcasted_iota(jnp.int32, (grp, LANES), 1)
    first = lane < ATT_HEAD_DIM

    def head_norm(blk, g_ref, c, scale):
        sq = blk * blk
        s0 = jnp.sum(jnp.where(first, sq, 0.0), axis=-1, keepdims=True)
        s1 = jnp.sum(jnp.where(first, 0.0, sq), axis=-1, keepdims=True)
        r = jnp.where(first, lax.rsqrt(s0 * (1.0 / ATT_HEAD_DIM) + EPS),
                      lax.rsqrt(s1 * (1.0 / ATT_HEAD_DIM) + EPS))
        return blk * r * (g_ref[:, c * LANES:(c + 1) * LANES] * scale)

    chunk = 2 * LANES
    work = []

    def run(count):
        for _ in range(min(count, len(work))):
            work.pop(0)()

    xbc = [proj(gi, o_xbc, SSD_CONV_DIM) for gi in range(len(groups))]

    def xbc_rows(r0, r1):
        gi = r0 // grp
        return xbc[gi][r0 - gi * grp:r1 - gi * grp, :]

    for s in range(nb):
        base = s * (rb + halo)
        if tiles_per_stream == 1:
            carried = cprev_ref[s]
        else:
            carried = jnp.where(i % tiles_per_stream == 0, cprev_ref[s], ext_ref[rb:rb + halo, :])
        ext_ref[base:base + halo, :] = carried
        piece = min(rb, grp)
        for r0 in range(0, rb, piece):
            ext_ref[base + halo + r0:base + halo + r0 + piece, :] = xbc_rows(s * rb + r0, s * rb + r0 + piece)
        xtail_ref[s] = xbc_rows((s + 1) * rb - halo, (s + 1) * rb)

        def conv_piece(s=s, base=base, cb=0, r0=0, nr=rb):
            sl = slice(cb * LANES, (cb + 1) * LANES)
            xe = ext_ref[base + r0:base + r0 + halo + nr, sl]
            acc = convb_ref[:, sl] + convw_ref[SSD_CONV - 1:SSD_CONV, sl] * xe[halo:, :]
            for j in range(1, SSD_CONV):
                tap = pltpu.roll(xe, j, 0)[halo:, :]
                acc = acc + convw_ref[SSD_CONV - 1 - j:SSD_CONV - j, sl] * tap
            xc_ref[s * rb + r0:s * rb + r0 + nr, sl] = _silu(acc)

        nr = min(rb, CONV_ROWS)
        for cb in range(SSD_CONV_DIM // LANES):
            for r0 in range(0, rb, nr):
                work.append(functools.partial(conv_piece, cb=cb, r0=r0, nr=nr))

    def gate(zc, c, g):
        gz_ref[g, c * LANES:(c + 1) * LANES] = _silu(zc)

    def norm_q(blk, c, g):
        q_ref[g, c * LANES:(c + 1) * LANES] = head_norm(
            blk, gq_ref, c, ATT_HEAD_DIM ** -0.5 * LOG2E).astype(BF16)

    def norm_k(blk, c, g):
        kn = head_norm(blk, gk_ref, c, 1.0)
        k_ref[g, c * LANES:(c + 1) * LANES] = kn.astype(BF16)
        kt_ref[g, c * LANES:(c + 1) * LANES] = kn

    def chunks(loc, width, consumer, per_chunk):
        w_ref, lo = loc
        for c0 in range(0, width, chunk):
            for gi, g in enumerate(groups):
                y = proj(gi, (w_ref, lo + c0), chunk)
                for t in range(chunk // LANES):
                    work.append(functools.partial(consumer, y[:, t * LANES:(t + 1) * LANES], c0 // LANES + t, g))
                run(per_chunk)

    per_chunk = -(-len(work) // (12 * len(groups))) + 2
    chunks((wzx_ref, 0), SSD_WIDTH, gate, per_chunk)
    chunks(o_q, ATT_WIDTH, norm_q, per_chunk)
    chunks(o_k, ATT_WIDTH, norm_k, per_chunk)
    for c0 in range(0, ATT_WIDTH, chunk):
        for gi, g in enumerate(groups):
            if v_feature_major:
                v_ref[c0:c0 + chunk, g] = _dot_nt(wvt_ref[c0:c0 + chunk, :], xn[gi]).astype(BF16)
            else:
                v = proj(gi, (wqkv_ref, 2 * ATT_WIDTH + c0), chunk)
                v_ref[g, c0:c0 + chunk] = v.astype(BF16)
                vt_ref[g, c0:c0 + chunk] = v
            run(3)
    run(len(work))
    for gi, g in enumerate(groups):
        dt_ref[g, :] = proj(gi, o_dt, DT_PAD)
    if v_feature_major:
        @pl.when(i >= n - n_tail)
        def _():
            for gi, g in enumerate(groups):
                vt_ref[g, :] = proj(gi, o_v, ATT_WIDTH)


def _in_proj(x2d, g_mix, w_zx, w_dt, w_qkv, gq_t, gk_t, cprev8, conv_w, conv_b, *, tr, seq_len,
             tail_rows, v_feature_major):
    rows = x2d.shape[0]
    n = rows // tr
    n_tail = tail_rows // tr
    nb = max(tr // seq_len, 1)
    tiles_per_stream = max(seq_len // tr, 1)
    n_streams = rows // seq_len
    assert n * tr == rows and n_tail * tr == tail_rows and cprev8.shape == (n_streams, 8, SSD_CONV_DIM)
    assert (tr // nb) % 8 == 0

    def row(width):
        return pl.BlockSpec((tr, width), lambda i: (i, 0))

    def const(shape):
        return pl.BlockSpec(shape, lambda i: (0,) * len(shape))

    per_stream = pl.BlockSpec((nb, 8, SSD_CONV_DIM), lambda i: (i // tiles_per_stream, 0, 0))
    tail = pl.BlockSpec((tr, ATT_WIDTH), lambda i: (jnp.maximum(i - (n - n_tail), 0), 0))
    if v_feature_major:
        v_spec = pl.BlockSpec((ATT_WIDTH, tr), lambda i: (0, i))
        v_shape = jax.ShapeDtypeStruct((ATT_WIDTH, rows), BF16)
    else:
        v_spec = row(ATT_WIDTH)
        v_shape = jax.ShapeDtypeStruct((rows, ATT_WIDTH), BF16)
    return pl.pallas_call(
        functools.partial(_inproj_kernel, n_tail=n_tail, v_feature_major=v_feature_major, nb=nb,
                          tiles_per_stream=tiles_per_stream),
        grid=(n,),
        in_specs=[row(D_MODEL), const((1, D_MODEL)), _resident((D_MODEL, SSD_WIDTH + SSD_CONV_DIM)),
                  _resident((D_MODEL, DT_PAD)), _resident((D_MODEL, 3 * ATT_WIDTH)),
                  const((1, ATT_WIDTH)), const((1, ATT_WIDTH)),
                  per_stream, const((SSD_CONV, SSD_CONV_DIM)), const((1, SSD_CONV_DIM))],
        out_specs=[row(SSD_WIDTH), row(SSD_CONV_DIM), per_stream, row(DT_PAD), row(ATT_WIDTH),
                   row(ATT_WIDTH), v_spec, tail, tail],
        out_shape=[jax.ShapeDtypeStruct((rows, SSD_WIDTH), F32),
                   jax.ShapeDtypeStruct((rows, SSD_CONV_DIM), F32),
                   jax.ShapeDtypeStruct((n_streams, 8, SSD_CONV_DIM), F32),
                   jax.ShapeDtypeStruct((rows, DT_PAD), F32),
                   jax.ShapeDtypeStruct((rows, ATT_WIDTH), BF16),
                   jax.ShapeDtypeStruct((rows, ATT_WIDTH), BF16),
                   v_shape,
                   jax.ShapeDtypeStruct((tail_rows, ATT_WIDTH), F32),
                   jax.ShapeDtypeStruct((tail_rows, ATT_WIDTH), F32)],
        scratch_shapes=[pltpu.VMEM((nb * (tr // nb + 8), SSD_CONV_DIM), F32),
                        pltpu.VMEM((ATT_WIDTH, D_MODEL) if v_feature_major else (16, LANES), BF16)],
        compiler_params=pltpu.CompilerParams(dimension_semantics=("arbitrary",),
                                             vmem_limit_bytes=VMEM_LIMIT),
        name="in_proj",
    )(x2d, g_mix, w_zx, w_dt, w_qkv, gq_t, gk_t, cprev8, conv_w, conv_b)


def _ssd_chunk(xc_ref, dtraw_ref, gz_ref, dtb_ref, alog_ref, dskip_ref, expand_ref, ht_ref, ybuf_ref,
               *, t, lb, row0=0):
    def rows_of(ref, sl):
        v = ref[0, row0:row0 + lb, sl]
        if lb < t:
            v = jnp.concatenate([v, jnp.zeros((t - lb, v.shape[1]), v.dtype)], axis=0)
        return v

    lane = lax.broadcasted_iota(jnp.int32, (t, LANES), 1)
    rowi = lax.broadcasted_iota(jnp.int32, (t, LANES), 0)
    dt = jax.nn.softplus(rows_of(dtraw_ref, slice(None)) + dtb_ref[...])
    dt = jnp.where((lane < SSD_HEADS) & (rowi < lb), dt, 0.0)
    a_neg = -jnp.exp(alog_ref[...]) * LOG2E
    a = dt * a_neg

    rr = lax.broadcasted_iota(jnp.int32, (t, t), 0)
    cc = lax.broadcasted_iota(jnp.int32, (t, t), 1)
    causal = rr >= cc
    tril = jnp.where(causal, 1.0, 0.0).astype(BF16)
    a1, a2, a3 = _split3(a)
    a_cum = _dot(tril, a1) + _dot(tril, a2) + _dot(tril, a3)
    a_last = a_cum[t - 1:t, :]
    ea = jnp.exp2(a_cum)
    cd = jnp.exp2(a_last)
    a_t = a_cum.T
    w_t = jnp.exp2(a_t[:, t - 1:t] - a_t)

    stacked = jnp.concatenate([ea, dt, jnp.broadcast_to(cd, (8, LANES))], axis=0)
    s_hi, s_lo = _split2(stacked)
    expanded = _dot(s_hi, expand_ref[...]) + _dot(s_lo, expand_ref[...])
    ea_x = expanded[0:t, :]
    dt_x = expanded[t:2 * t, :]
    cd_x = expanded[2 * t:2 * t + 1, :]

    first = lane < SSD_HEAD_DIM
    heads_per_group = SSD_HEADS // SSD_GROUPS
    o_b = SSD_WIDTH
    o_c = SSD_WIDTH + SSD_GROUPS * SSD_STATE
    ssq = [jnp.zeros((lb, 1), F32)]
    per_group = {}

    def group_values(g):
        if g not in per_group:
            bg = rows_of(xc_ref, slice(o_b + g * SSD_STATE, o_b + (g + 1) * SSD_STATE))
            cg = rows_of(xc_ref, slice(o_c + g * SSD_STATE, o_c + (g + 1) * SSD_STATE)).astype(BF16)
            cb_mat = jnp.where(causal, _dot_nt(cg, bg.astype(BF16)), 0.0)
            per_group[g] = (cg, cb_mat, bg.T)
        return per_group[g]

    def prepare(j):
        cg, cb_mat, bg_t = group_values(j // (heads_per_group // 2))
        sl = slice(j * LANES, (j + 1) * LANES)
        xh = rows_of(xc_ref, sl)
        xdt = xh * dt_x[:, sl]
        ops = []
        for hh in range(2):
            h = 2 * j + hh
            xm = jnp.where(first if hh == 0 else jnp.logical_not(first), xdt, 0.0).astype(BF16)
            seg = a_cum[:, h:h + 1] - a_t[h:h + 1, :]
            m = (cb_mat * jnp.exp2(jnp.minimum(seg, 0.0))).astype(BF16)
            bw = (bg_t * w_t[h:h + 1, :]).astype(BF16)
            ops.append((m, bw, xm))
        return cg, xh, ops

    def finish(j, prepared):
        cg, xh, ops = prepared
        sl = slice(j * LANES, (j + 1) * LANES)
        y_pair = _dot(ops[0][0], ops[0][2]) + _dot(ops[1][0], ops[1][2])
        s_pair = _dot(ops[0][1], ops[0][2]) + _dot(ops[1][1], ops[1][2])
        h_in = ht_ref[:, sl]
        y_off = _dot(cg, h_in.astype(BF16)) * ea_x[:, sl]
        ht_ref[:, sl] = cd_x[:, sl] * h_in + s_pair
        y = y_pair + y_off + dskip_ref[:, sl] * xh
        yg = y[0:lb, :] * gz_ref[0, row0:row0 + lb, sl]
        ybuf_ref[row0:row0 + lb, sl] = yg
        ssq[0] = ssq[0] + jnp.sum(yg * yg, axis=-1, keepdims=True)

    def scan():
        n_tiles = SSD_WIDTH // LANES
        ahead = prepare(0)
        for j in range(n_tiles):
            cur = ahead
            if j + 1 < n_tiles:
                ahead = prepare(j + 1)
            finish(j, cur)
        return ssq[0]

    return scan


def _ssd_kernel(xc_ref, dtraw_ref, gz_ref, h0_ref, dtb_ref, alog_ref, dskip_ref, gout_ref, expand_ref,
                y_ref, hfin_ref,
                ht_ref, ybuf_ref, *, t, lb, parts):
    c = pl.program_id(1)
    nc = pl.num_programs(1)

    @pl.when(c == 0)
    def _():
        ht_ref[...] = h0_ref[0].T

    scans = [_ssd_chunk(xc_ref, dtraw_ref, gz_ref, dtb_ref, alog_ref, dskip_ref, expand_ref, ht_ref,
                        ybuf_ref, t=t, lb=lb, row0=part * lb) for part in range(parts)]
    for part, scan in enumerate(scans):
        rows = slice(part * lb, (part + 1) * lb)
        r = lax.rsqrt(scan() * (1.0 / SSD_WIDTH) + EPS)
        y_ref[0, rows, :] = (ybuf_ref[rows, :] * r * gout_ref[...]).astype(BF16)

    @pl.when(c == nc - 1)
    def _():
        hfin_ref[0] = ht_ref[...].T


def _ssd_mixer(xc, dtraw, gz, h0, dtb, alog, dskip, gout, expand, *, t, lb):
    b, length, _ = xc.shape
    n_chunks = length // lb
    assert n_chunks * lb == length and (lb == t or n_chunks == 1)
    parts = SSD_PARTS if n_chunks % SSD_PARTS == 0 else 1
    nc = n_chunks // parts

    def seq(width):
        return pl.BlockSpec((1, parts * lb, width), lambda bi, ci: (bi, ci, 0))

    def per_b(shape):
        return pl.BlockSpec((1,) + shape, lambda bi, ci: (bi, 0, 0))

    def const(shape):
        return pl.BlockSpec(shape, lambda bi, ci: (0,) * len(shape))

    hp = SSD_WIDTH
    return pl.pallas_call(
        functools.partial(_ssd_kernel, t=t, lb=lb, parts=parts),
        grid=(b, nc),
        in_specs=[seq(SSD_CONV_DIM), seq(DT_PAD), seq(SSD_WIDTH), per_b((hp, SSD_STATE)),
                  const((1, DT_PAD)), const((1, DT_PAD)), const((1, SSD_WIDTH)), const((1, SSD_WIDTH)),
                  const((LANES, SSD_WIDTH))],
        out_specs=[seq(SSD_WIDTH), per_b((hp, SSD_STATE))],
        out_shape=[jax.ShapeDtypeStruct((b, length, SSD_WIDTH), BF16),
                   jax.ShapeDtypeStruct((b, hp, SSD_STATE), F32)],
        scratch_shapes=[pltpu.VMEM((SSD_STATE, hp), F32),
                        pltpu.VMEM((parts * lb, SSD_WIDTH), F32)],
        compiler_params=pltpu.CompilerParams(dimension_semantics=("arbitrary", "arbitrary"),
                                             vmem_limit_bytes=VMEM_LIMIT),
        name="ssd_mixer",
    )(xc, dtraw, gz, h0, dtb, alog, dskip, gout, expand)


ATT_SB = 2 * CHUNK
ATT_WIN = ATT_SB + PAST
ATT_PER_TRIP = 8


def _toeplitz_rows(base_row, rows, offset, width):
    ext = base_row.shape[1]
    rolled = pltpu.roll(jnp.broadcast_to(base_row, (rows, ext)), 0, 1, stride=1, stride_axis=0)
    return rolled[:, offset:offset + width]


def _band_subblock(q, kw, vt, bias_t, first):
    return _band_softmax_pv(_band_scores(q, kw, bias_t, first), vt)


def _band_scores(q, kw, bias_t, first):
    zero = jnp.zeros_like(q)
    q2 = jnp.concatenate([jnp.where(first, q, zero), jnp.where(first, zero, q)], axis=0)
    return _dot_nt(kw, q2) + bias_t


def _band_softmax(s):
    m = jnp.max(s, axis=0, keepdims=True)
    p = jnp.exp2(s - m)
    return p.astype(BF16), 1.0 / jnp.sum(p, axis=0, keepdims=True)


def _band_pv(p, inv_l, vt):
    sb = p.shape[1] // 2
    half = ATT_HEAD_DIM
    o_t = _dot(vt, p)
    o_pair_t = jnp.concatenate([o_t[0:half, 0:sb] * inv_l[:, 0:sb],
                                o_t[half:2 * half, sb:2 * sb] * inv_l[:, sb:2 * sb]], axis=0)
    return o_pair_t.T.astype(BF16)


def _band_softmax_pv(s, vt):
    p, inv_l = _band_softmax(s)
    return _band_pv(p, inv_l, vt)


def _band_prompt_kernel(q_ref, k_ref, vt_ref, base_ref, o_ref, bias_ref, sc_ref, pb_ref, il_ref):
    length = q_ref.shape[0]
    sb, w = ATT_SB, ATT_WIN
    kj = lax.broadcasted_iota(jnp.int32, (w, sb), 0) // CHUNK
    qi = lax.broadcasted_iota(jnp.int32, (w, sb), 1) // CHUNK
    in_band = (kj >= qi) & (kj <= qi + N_LEFT_CHUNKS)
    for hh in range(2):
        toe = _toeplitz_rows(base_ref[0, hh:hh + 1, :], w, w, sb)
        bias_ref[:, hh * sb:(hh + 1) * sb] = jnp.where(in_band, toe * LOG2E, NEG)

    lane = lax.broadcasted_iota(jnp.int32, (sb, LANES), 1)
    first = lane < ATT_HEAD_DIM
    n_sub = length // sb
    n_head = min(PAST // sb, n_sub)
    for j in range(n_head):
        n = (j + 1) * sb
        o_ref[j * sb:(j + 1) * sb, :] = _band_subblock(
            q_ref[j * sb:(j + 1) * sb, :], k_ref[0:n, :], vt_ref[:, 0:n], bias_ref[w - n:, :], first)

    n_main = n_sub - n_head
    if n_main == 0:
        return
    assert n_main % 2 == 0

    def scores_into(slot, j):
        r0 = pl.multiple_of(j * sb, sb)
        k0 = pl.multiple_of(j * sb - PAST, sb)
        sc_ref[slot] = _band_scores(q_ref[pl.ds(r0, sb), :], k_ref[pl.ds(k0, w), :], bias_ref[...], first)

    def softmax_into(slot):
        p, inv_l = _band_softmax(sc_ref[slot])
        pb_ref[slot] = p
        il_ref[slot] = jnp.broadcast_to(inv_l, (8, 2 * sb))

    def values_out(slot, j):
        k0 = pl.multiple_of(j * sb - PAST, sb)
        return _band_pv(pb_ref[slot], il_ref[slot, 0:1, :], vt_ref[:, pl.ds(k0, w)])

    last = n_sub - 1
    scores_into(0, n_head)
    scores_into(1, n_head + 1)
    softmax_into(0)

    def trip(per_trip, base):
        def body(i, carry):
            outs = []
            for u in range(per_trip):
                j = base + per_trip * i + u
                slot = u % 2
                scores_into(slot, jnp.minimum(j + 2, last))
                softmax_into(1 - slot)
                outs.append((j, values_out(slot, j)))
            for j, o in outs:
                o_ref[pl.ds(pl.multiple_of(j * sb, sb), sb), :] = o
            return carry
        return body

    n_long = n_main // ATT_PER_TRIP
    if n_long:
        lax.fori_loop(0, n_long, trip(ATT_PER_TRIP, n_head), 0)
    rest = n_main - n_long * ATT_PER_TRIP
    if rest:
        trip(rest, n_head + n_long * ATT_PER_TRIP)(0, 0)


def _band_prompt(q, k, vt, base):
    length = q.shape[0]
    assert length % ATT_SB == 0
    ext = ATT_SB + ATT_WIN
    col = pl.BlockSpec((length, LANES), lambda c: (0, c))
    return pl.pallas_call(
        _band_prompt_kernel,
        grid=(ATT_WIDTH // LANES,),
        in_specs=[col, col, pl.BlockSpec((LANES, length), lambda c: (c, 0)),
                  pl.BlockSpec((1, 2, ext), lambda c: (c, 0, 0))],
        out_specs=col,
        out_shape=jax.ShapeDtypeStruct((length, ATT_WIDTH), BF16),
        scratch_shapes=[pltpu.VMEM((ATT_WIN, 2 * ATT_SB), F32),
                        pltpu.VMEM((2, ATT_WIN, 2 * ATT_SB), F32),
                        pltpu.VMEM((2, ATT_WIN, 2 * ATT_SB), BF16),
                        pltpu.VMEM((2, 8, 2 * ATT_SB), F32)],
        compiler_params=pltpu.CompilerParams(dimension_semantics=("arbitrary",),
                                             vmem_limit_bytes=VMEM_LIMIT),
        name="band_prompt",
    )(q, k, vt, base.reshape(ATT_HEADS // 2, 2, ext))


def _band_sample_kernel(q_ref, kn_ref, vn_ref, kc_ref, vc_ref, base_ref, o_ref, bias_ref):
    lq = q_ref.shape[0]
    lc = kc_ref.shape[3]

    @pl.when(pl.program_id(0) == 0)
    def _():
        for h in range(ATT_HEADS):
            bias_ref[h] = _toeplitz_rows(base_ref[h:h + 1, :], lq, LANES, lc + LANES) * LOG2E

    for h in range(ATT_HEADS):
        sl = slice(h * ATT_HEAD_DIM, (h + 1) * ATT_HEAD_DIM)
        q = q_ref[:, sl]
        sc = _dot(q, kc_ref[0, h].astype(BF16)) + bias_ref[h, :, 0:lc]
        sn = _dot_nt(q, kn_ref[:, sl]) + bias_ref[h, :, lc:lc + lq]
        m = jnp.maximum(jnp.max(sc, axis=-1, keepdims=True), jnp.max(sn, axis=-1, keepdims=True))
        pc = jnp.exp2(sc - m)
        pn = jnp.exp2(sn - m)
        l = jnp.sum(pc, axis=-1, keepdims=True) + jnp.sum(pn, axis=-1, keepdims=True)
        o = _dot_nt(pc.astype(BF16), vc_ref[0, h].astype(BF16)) + _dot(pn.astype(BF16), vn_ref[:, sl])
        o_ref[:, sl] = (o * (1.0 / l)).astype(BF16)


def _band_sample(q, kn, vn, kc, vc, base, *, lq):
    rows = q.shape[0]
    nb = rows // lq
    lc = kc.shape[3]
    assert lq <= LANES and lc % LANES == 0 and base.shape == (ATT_HEADS, lc + 2 * LANES)
    blk = pl.BlockSpec((lq, ATT_WIDTH), lambda b: (b, 0))
    cache = pl.BlockSpec((1, ATT_HEADS, ATT_HEAD_DIM, lc), lambda b: (b, 0, 0, 0))
    return pl.pallas_call(
        _band_sample_kernel,
        grid=(nb,),
        in_specs=[blk, blk, blk, cache, cache, pl.BlockSpec(base.shape, lambda b: (0, 0))],
        out_specs=blk,
        out_shape=jax.ShapeDtypeStruct((rows, ATT_WIDTH), BF16),
        scratch_shapes=[pltpu.VMEM((ATT_HEADS, lq, lc + LANES), F32)],
        compiler_params=pltpu.CompilerParams(dimension_semantics=("arbitrary",),
                                             vmem_limit_bytes=VMEM_LIMIT),
        name="band_sample",
    )(q, kn, vn, kc, vc, base)


def _mem_head_norm(y, g_ref, hd, scale):
    blk = y[:, hd * MEM_HEAD_DIM:(hd + 1) * MEM_HEAD_DIM]
    r = lax.rsqrt(jnp.mean(blk * blk, axis=-1, keepdims=True) + EPS)
    return blk * r * (g_ref[...] * scale)


def _memkv_kernel(mem_ref, gsrc_ref, wk_ref, wv_ref, gk_ref, k_ref, v_ref):
    m = _rms(mem_ref[...], gsrc_ref[...]).astype(BF16)
    k = _dot(m, wk_ref[...])
    for hd in range(MEM_HEADS):
        k_ref[:, hd * MEM_HEAD_DIM:(hd + 1) * MEM_HEAD_DIM] = _mem_head_norm(k, gk_ref, hd, 1.0)
    v_ref[...] = _dot(m, wv_ref[...])


def _memory_kv(mem2d, g_src, w_mk, w_mv, g_mk):
    rows = mem2d.shape[0]

    def full(shape):
        return pl.BlockSpec(shape, lambda i: (0,) * len(shape))

    return pl.pallas_call(
        _memkv_kernel,
        grid=(1,),
        in_specs=[full((rows, D_MODEL)), full((1, D_MODEL)), full((D_MODEL, D_MODEL)),
                  full((D_MODEL, D_MODEL)), full((1, MEM_HEAD_DIM))],
        out_specs=[full((rows, D_MODEL)), full((rows, D_MODEL))],
        out_shape=[jax.ShapeDtypeStruct((rows, D_MODEL), F32)] * 2,
        compiler_params=pltpu.CompilerParams(dimension_semantics=("arbitrary",),
                                             vmem_limit_bytes=VMEM_LIMIT),
        name="memory_kv",
    )(mem2d, g_src, w_mk, w_mv, g_mk)


FF_SLAB = 1024


def _mix_ffn_kernel(x_ref, ys_ref, ya_ref, wo_ref, gmx_ref, wq_ref, gmq_ref, mk_ref, mv_ref,
                    wmo_ref, gffn_ref, w1_ref, w2_ref, y_ref, obuf_ref, *, nb, rb, dense_rows):
    tr = x_ref.shape[0]
    groups = [slice(r0, r0 + dense_rows) for r0 in range(0, tr, dense_rows)]
    h = [x_ref[g, :] + _dot(ys_ref[g, :], wo_ref[0:SSD_WIDTH, :]) + _dot(ya_ref[g, :], wo_ref[SSD_WIDTH:, :])
         for g in groups]
    hn = [_rms(hg, gmx_ref[...]).astype(BF16) for hg in h]
    q = [_dot(hg, wq_ref[...]) for hg in hn]
    for hd in range(MEM_HEADS):
        sl = slice(hd * MEM_HEAD_DIM, (hd + 1) * MEM_HEAD_DIM)
        qn = [_mem_head_norm(qg, gmq_ref, hd, MEM_HEAD_DIM ** -0.5 * LOG2E).astype(BF16) for qg in q]
        for g, qg in zip(groups, qn):
            for r0 in range(g.start, g.stop, rb):
                b = r0 // rb if nb > 1 else 0
                s = _dot_nt(qg[r0 - g.start:r0 - g.start + min(rb, dense_rows), :], mk_ref[b, :, sl])
                p = jnp.exp2(s - jnp.max(s, axis=-1, keepdims=True))
                o = _dot(p.astype(BF16), mv_ref[b, :, sl]) * (1.0 / jnp.sum(p, axis=-1, keepdims=True))
                obuf_ref[r0:r0 + min(rb, dense_rows), sl] = o.astype(BF16)
    h = [hg + _dot(obuf_ref[g, :], wmo_ref[...]) for g, hg in zip(groups, h)]
    hn = [_rms(hg, gffn_ref[...]).astype(BF16) for hg in h]
    for s in range(D_FF // FF_SLAB):
        for gi in range(len(groups)):
            u = jnp.maximum(_dot(hn[gi], w1_ref[:, s * FF_SLAB:(s + 1) * FF_SLAB]), 0.0)
            h[gi] = h[gi] + _dot((u * u).astype(BF16), w2_ref[s * FF_SLAB:(s + 1) * FF_SLAB, :])
    for g, hg in zip(groups, h):
        y_ref[g, :] = hg


def _mix_ffn(x2d, ys, ya, w_out, g_mem_x, w_mq, g_mq, mk, mv, w_mo, g_ffn, w1, w2, *, tr, seq_len):
    rows = x2d.shape[0]
    n = rows // tr
    nb = max(tr // seq_len, 1)
    tiles_per_stream = max(seq_len // tr, 1)
    rb = tr // nb
    assert n * tr == rows and nb * rb == tr and mk.shape[0] * seq_len == rows

    def row(width):
        return pl.BlockSpec((tr, width), lambda i: (i, 0))

    def const(shape):
        return pl.BlockSpec(shape, lambda i: (0,) * len(shape))

    mem = pl.BlockSpec((nb, N_MEM, D_MODEL), lambda i: (i // tiles_per_stream, 0, 0))
    return pl.pallas_call(
        functools.partial(_mix_ffn_kernel, nb=nb, rb=rb, dense_rows=min(tr, MIX_DENSE_ROWS)),
        grid=(n,),
        in_specs=[row(D_MODEL), row(SSD_WIDTH), row(ATT_WIDTH), _resident((SSD_WIDTH + ATT_WIDTH, D_MODEL)),
                  const((1, D_MODEL)), _resident((D_MODEL, D_MODEL)), const((1, MEM_HEAD_DIM)), mem, mem,
                  _resident((D_MODEL, D_MODEL)), const((1, D_MODEL)), _resident((D_MODEL, D_FF)),
                  _resident((D_FF, D_MODEL))],
        out_specs=row(D_MODEL),
        out_shape=jax.ShapeDtypeStruct((rows, D_MODEL), F32),
        scratch_shapes=[pltpu.VMEM((tr, D_MODEL), BF16)],
        compiler_params=pltpu.CompilerParams(dimension_semantics=("arbitrary",),
                                             vmem_limit_bytes=VMEM_LIMIT),
        name="mix_ffn",
    )(x2d, ys, ya, w_out, g_mem_x, w_mq, g_mq, mk, mv, w_mo, g_ffn, w1, w2)


def _toeplitz_base(table, offset, width):
    heads, size = table.shape
    n_far = offset - REL_CLIP
    assert n_far >= 0
    parts = [jnp.broadcast_to(table[:, size - 1:], (heads, n_far)), table[:, ::-1]]
    rest = width - n_far - size
    if rest > 0:
        parts.append(jnp.broadcast_to(table[:, :1], (heads, rest)))
    return jnp.concatenate(parts, axis=1)[:, :width].astype(F32)


def _toeplitz_base_t(table, shift, width):
    heads, size = table.shape
    n_low = shift - REL_CLIP
    assert n_low >= 0
    parts = [jnp.broadcast_to(table[:, :1], (heads, n_low)), table]
    rest = width - n_low - size
    if rest > 0:
        parts.append(jnp.broadcast_to(table[:, size - 1:], (heads, rest)))
    return jnp.concatenate(parts, axis=1)[:, :width].astype(F32)


def _prep_weights(g_mix, w_in, conv_w, conv_b, ssd_A_log, ssd_dt_bias, ssd_D, ssd_g_out, att_g_q, att_g_k,
                  w_out, g_mem_x, g_mem_src, w_mq, w_mk, w_mv, g_mq, g_mk, w_mo, g_ffn, w_ff1, w_ff2):
    w_zx, w_dt, w_qkv = _split_w_in(w_in)
    pad_h = lambda v: jnp.pad(v, (0, DT_PAD - SSD_HEADS)).reshape(1, DT_PAD)
    row = lambda v: v.reshape(1, -1)
    expand = (jnp.arange(LANES)[:, None] == jnp.arange(SSD_WIDTH)[None, :] // SSD_HEAD_DIM).astype(BF16)
    return dict(
        g_mix=row(g_mix), w_zx=w_zx, w_dt=w_dt, w_qkv=w_qkv,
        gq_t=row(jnp.tile(att_g_q, ATT_HEADS)), gk_t=row(jnp.tile(att_g_k, ATT_HEADS)),
        conv_w=conv_w, conv_b=row(conv_b), dtb=pad_h(ssd_dt_bias), alog=pad_h(ssd_A_log),
        dskip=row(jnp.repeat(ssd_D, SSD_HEAD_DIM)), gout=row(ssd_g_out), expand=expand,
        w_out=w_out.astype(BF16), g_mem_x=row(g_mem_x), g_mem_src=row(g_mem_src),
        w_mq=w_mq.astype(BF16), w_mk=w_mk.astype(BF16), w_mv=w_mv.astype(BF16),
        g_mq=row(g_mq), g_mk=row(g_mk), w_mo=w_mo.astype(BF16), g_ffn=row(g_ffn),
        w_ff1=w_ff1.astype(BF16), w_ff2=w_ff2.astype(BF16))


def _layer(x, conv_prev, h0, k_cache, v_cache, mem_k, mem_v, p, rel, *, tr, t_scan):
    b, length, _ = x.shape
    rows = b * length
    x2d = x.reshape(rows, D_MODEL)
    prompt = k_cache is None
    tail_rows = PAST if prompt else rows
    cprev8 = jnp.pad(conv_prev, ((0, 0), (8 - (SSD_CONV - 1), 0), (0, 0)))
    tr_in = 2 * tr if prompt and rows % (2 * tr) == 0 and tail_rows % (2 * tr) == 0 else tr
    gz, xc, xtail, dtraw, q, k, v, k_tail, v_tail = _in_proj(
        x2d, p["g_mix"], p["w_zx"], p["w_dt"], p["w_qkv"], p["gq_t"], p["gk_t"], cprev8,
        p["conv_w"], p["conv_b"], tr=tr_in, seq_len=length, tail_rows=tail_rows, v_feature_major=prompt)
    conv_new = xtail[:, 8 - (SSD_CONV - 1):]

    scan_args = (xc.reshape(b, length, SSD_CONV_DIM), dtraw.reshape(b, length, DT_PAD),
                 gz.reshape(b, length, SSD_WIDTH), h0.reshape(b, SSD_WIDTH, SSD_STATE),
                 p["dtb"], p["alog"], p["dskip"], p["gout"], p["expand"])
    mix_args = (p["w_out"], p["g_mem_x"], p["w_mq"], p["g_mq"], mem_k.astype(BF16), mem_v.astype(BF16),
                p["w_mo"], p["g_ffn"], p["w_ff1"], p["w_ff2"])
    y_ssd, h_fin = _ssd_mixer(*scan_args, t=t_scan, lb=t_scan if prompt else length)
    if prompt:
        y_att = _band_prompt(q, k, v, _toeplitz_base_t(rel, ATT_SB, ATT_SB + ATT_WIN))
        k_rows = k_tail.reshape(b, PAST, ATT_HEADS, ATT_HEAD_DIM)
        v_rows = v_tail.reshape(b, PAST, ATT_HEADS, ATT_HEAD_DIM)
    else:
        lc = k_cache.shape[1]
        y_att = _band_sample(q, k, v, jnp.transpose(k_cache, (0, 2, 3, 1)),
                             jnp.transpose(v_cache, (0, 2, 3, 1)),
                             _toeplitz_base(rel, lc + LANES, lc + 2 * LANES), lq=length)
        k_rows = k_tail.reshape(b, length, ATT_HEADS, ATT_HEAD_DIM)
        v_rows = v_tail.reshape(b, length, ATT_HEADS, ATT_HEAD_DIM)
    tr_mix = 2 * tr if prompt and rows % (2 * tr) == 0 else tr
    y = _mix_ffn(x2d, y_ssd.reshape(rows, SSD_WIDTH), y_att, *mix_args, tr=tr_mix, seq_len=length)
    return (y.reshape(b, length, D_MODEL), h_fin.reshape(b, SSD_HEADS, SSD_HEAD_DIM, SSD_STATE),
            conv_new, k_rows, v_rows)


def kernel(x_prompt, x_sample, mem_prompt, state_ssd, state_conv, cache_attn_k, cache_attn_v, cache_mem_k,
           cache_mem_v, g_mix, w_in, conv_w, conv_b, ssd_A_log, ssd_dt_bias, ssd_D, ssd_g_out, att_g_q,
           att_g_k, att_rel_bias, w_out, g_mem_x, g_mem_src, w_mq, w_mk, w_mv, g_mq, g_mk, w_mo, g_ffn,
           w_ff1, w_ff2):
    depth = g_mix.shape[0]
    b_p, seq, _ = x_prompt.shape
    b_s, dec_seq, _ = x_sample.shape
    yp, ys = x_prompt, x_sample
    outs = [[] for _ in range(10)]
    for l in range(depth):
        p = _prep_weights(g_mix[l], w_in[l], conv_w[l], conv_b[l], ssd_A_log[l], ssd_dt_bias[l], ssd_D[l],
                          ssd_g_out[l], att_g_q[l], att_g_k[l], w_out[l], g_mem_x[l], g_mem_src[l],
                          w_mq[l], w_mk[l], w_mv[l], g_mq[l], g_mk[l], w_mo[l], g_ffn[l], w_ff1[l], w_ff2[l])
        rel = att_rel_bias[l]
        mk, mv = _memory_kv(mem_prompt.reshape(b_p * N_MEM, D_MODEL), p["g_mem_src"], p["w_mk"], p["w_mv"],
                            p["g_mk"])
        mk = mk.reshape(b_p, N_MEM, D_MODEL)
        mv = mv.reshape(b_p, N_MEM, D_MODEL)
        conv0 = jnp.zeros((b_p, SSD_CONV - 1, SSD_CONV_DIM), F32)
        h00 = jnp.zeros((b_p, SSD_HEADS, SSD_HEAD_DIM, SSD_STATE), F32)
        yp, hp, cp, kp, vp = _layer(yp, conv0, h00, None, None, mk, mv, p, rel,
                                    tr=min(256, seq), t_scan=min(256, seq))
        ys, hs, cs, ks_, vs_ = _layer(ys, state_conv[l], state_ssd[l], cache_attn_k[l], cache_attn_v[l],
                                      cache_mem_k[l].reshape(b_s, N_MEM, D_MODEL),
                                      cache_mem_v[l].reshape(b_s, N_MEM, D_MODEL), p, rel,
                                      tr=b_s * dec_seq, t_scan=128)
        for lst, val in zip(outs, (hp, cp, kp, vp,
                                   mk.reshape(b_p, N_MEM, MEM_HEADS, MEM_HEAD_DIM),
                                   mv.reshape(b_p, N_MEM, MEM_HEADS, MEM_HEAD_DIM),
                                   hs, cs, ks_, vs_)):
            lst.append(val)
    return (yp, ys) + tuple(jnp.stack(o) for o in outs)
```

```python
import functools

import jax
import jax.numpy as jnp
from jax import lax
from jax.experimental import pallas as pl
from jax.experimental.pallas import tpu as pltpu

F32 = jnp.float32
BF16 = jnp.bfloat16

D_MODEL = 1024
CHUNK = 64
SSD_HEADS = 16
SSD_HEAD_DIM = 64
SSD_WIDTH = SSD_HEADS * SSD_HEAD_DIM
SSD_GROUPS = 2
SSD_STATE = 128
SSD_CONV = 4
SSD_CONV_DIM = SSD_WIDTH + 2 * SSD_GROUPS * SSD_STATE
ATT_HEADS = 16
ATT_HEAD_DIM = 64
ATT_WIDTH = ATT_HEADS * ATT_HEAD_DIM
N_LEFT_CHUNKS = 8
PAST = N_LEFT_CHUNKS * CHUNK
REL_CLIP = 128
N_MEM = 256
MEM_HEADS = 4
MEM_HEAD_DIM = D_MODEL // MEM_HEADS
D_FF = 4 * D_MODEL
EPS = 1e-6

LANES = 128
DT_PAD = LANES
NEG = -1e30
LOG2E = 1.4426950408889634
CONV_ROWS = 64
MIX_DENSE_ROWS = 256
IN_GROUP_ROWS = 256
SSD_PARTS = 4
VMEM_LIMIT = 56 * 1024 * 1024


def _rms(x, g):
    return x * lax.rsqrt(jnp.mean(x * x, axis=-1, keepdims=True) + EPS) * g


def _silu(x):
    h = 0.5 * x
    return h + h * jnp.tanh(h)


def _split2(x):
    hi = x.astype(BF16)
    lo = (x - hi.astype(F32)).astype(BF16)
    return hi, lo


def _split3(x):
    hi = x.astype(BF16)
    r = x - hi.astype(F32)
    mid = r.astype(BF16)
    lo = (r - mid.astype(F32)).astype(BF16)
    return hi, mid, lo


def _resident(shape):
    return pl.BlockSpec(shape, lambda *_: (0,) * len(shape), pipeline_mode=pl.Buffered(1))


def _dot(a, b):
    return jnp.dot(a, b, preferred_element_type=F32)


def _dot_nt(a, b):
    return lax.dot_general(a, b, (((1,), (1,)), ((), ())), preferred_element_type=F32)


W_PREP_ROWS = 128


def _split_win_kernel(w_ref, wzx_ref, wdt_ref, wqkv_ref):
    o_dt = SSD_WIDTH + SSD_CONV_DIM
    o_qkv = o_dt + SSD_HEADS
    wzx_ref[...] = w_ref[:, 0:o_dt].astype(BF16)
    lane = lax.broadcasted_iota(jnp.int32, (w_ref.shape[0], DT_PAD), 1)
    wdt_ref[...] = jnp.where(lane < SSD_HEADS, w_ref[:, o_dt:o_dt + DT_PAD], 0.0).astype(BF16)
    wqkv_ref[...] = w_ref[:, o_qkv:o_qkv + 3 * ATT_WIDTH].astype(BF16)


def _split_w_in(w_in):
    d, width = w_in.shape
    n = d // W_PREP_ROWS
    widths = (SSD_WIDTH + SSD_CONV_DIM, DT_PAD, 3 * ATT_WIDTH)
    return pl.pallas_call(
        _split_win_kernel,
        grid=(n,),
        in_specs=[pl.BlockSpec((W_PREP_ROWS, width), lambda i: (i, 0))],
        out_specs=[pl.BlockSpec((W_PREP_ROWS, w), lambda i: (i, 0)) for w in widths],
        out_shape=[jax.ShapeDtypeStruct((d, w), BF16) for w in widths],
        compiler_params=pltpu.CompilerParams(dimension_semantics=("arbitrary",),
                                             vmem_limit_bytes=VMEM_LIMIT),
        name="split_w_in",
    )(w_in)


def _inproj_kernel(x_ref, gmix_ref, wzx_ref, wdt_ref, wqkv_ref, gq_ref, gk_ref, cprev_ref,
                   convw_ref, convb_ref,
                   gz_ref, xc_ref, xtail_ref, dt_ref, q_ref, k_ref, v_ref, kt_ref, vt_ref, ext_ref, wvt_ref,
                   *, n_tail, v_feature_major, nb, tiles_per_stream):
    i = pl.program_id(0)
    n = pl.num_programs(0)
    tr = x_ref.shape[0]
    rb = tr // nb
    halo = 8
    if v_feature_major:
        @pl.when(i == 0)
        def _():
            wvt_ref[...] = wqkv_ref[:, 2 * ATT_WIDTH:].T
    grp = min(tr, IN_GROUP_ROWS)
    groups = [slice(r0, r0 + grp) for r0 in range(0, tr, grp)]
    xn = [_rms(x_ref[g, :], gmix_ref[...]).astype(BF16) for g in groups]

    def proj(gi, loc, width):
        w_ref, lo = loc
        return _dot(xn[gi], w_ref[:, lo:lo + width])

    o_xbc = (wzx_ref, SSD_WIDTH)
    o_dt = (wdt_ref, 0)
    o_q = (wqkv_ref, 0)
    o_k = (wqkv_ref, ATT_WIDTH)
    o_v = (wqkv_ref, 2 * ATT_WIDTH)
    lane = lax.broadcasted_iota(jnp.int32, (grp, LANES), 1)
    first = lane < ATT_HEAD_DIM

    def head_norm(blk, g_ref, c, scale):
        sq = blk * blk
        s0 = jnp.sum(jnp.where(first, sq, 0.0), axis=-1, keepdims=True)
        s1 = jnp.sum(jnp.where(first, 0.0, sq), axis=-1, keepdims=True)
        r = jnp.where(first, lax.rsqrt(s0 * (1.0 / ATT_HEAD_DIM) + EPS),
                      lax.rsqrt(s1 * (1.0 / ATT_HEAD_DIM) + EPS))
        return blk * r * (g_ref[:, c * LANES:(c + 1) * LANES] * scale)

    chunk = 2 * LANES
    work = []

    def run(count):
        for _ in range(min(count, len(work))):
            work.pop(0)()

    xbc = [proj(gi, o_xbc, SSD_CONV_DIM) for gi in range(len(groups))]

    def xbc_rows(r0, r1):
        gi = r0 // grp
        return xbc[gi][r0 - gi * grp:r1 - gi * grp, :]

    for s in range(nb):
        base = s * (rb + halo)
        if tiles_per_stream == 1:
            carried = cprev_ref[s]
        else:
            carried = jnp.where(i % tiles_per_stream == 0, cprev_ref[s], ext_ref[rb:rb + halo, :])
        ext_ref[base:base + halo, :] = carried
        piece = min(rb, grp)
        for r0 in range(0, rb, piece):
            ext_ref[base + halo + r0:base + halo + r0 + piece, :] = xbc_rows(s * rb + r0, s * rb + r0 + piece)
        xtail_ref[s] = xbc_rows((s + 1) * rb - halo, (s + 1) * rb)

        def conv_piece(s=s, base=base, cb=0, r0=0, nr=rb):
            sl = slice(cb * LANES, (cb + 1) * LANES)
            xe = ext_ref[base + r0:base + r0 + halo + nr, sl]
            acc = convb_ref[:, sl] + convw_ref[SSD_CONV - 1:SSD_CONV, sl] * xe[halo:, :]
            for j in range(1, SSD_CONV):
                tap = pltpu.roll(xe, j, 0)[halo:, :]
                acc = acc + convw_ref[SSD_CONV - 1 - j:SSD_CONV - j, sl] * tap
            xc_ref[s * rb + r0:s * rb + r0 + nr, sl] = _silu(acc)

        nr = min(rb, CONV_ROWS)
        for cb in range(SSD_CONV_DIM // LANES):
            for r0 in range(0, rb, nr):
                work.append(functools.partial(conv_piece, cb=cb, r0=r0, nr=nr))

    def gate(zc, c, g):
        gz_ref[g, c * LANES:(c + 1) * LANES] = _silu(zc)

    def norm_q(blk, c, g):
        q_ref[g, c * LANES:(c + 1) * LANES] = head_norm(
            blk, gq_ref, c, ATT_HEAD_DIM ** -0.5 * LOG2E).astype(BF16)

    def norm_k(blk, c, g):
        kn = head_norm(blk, gk_ref, c, 1.0)
        k_ref[g, c * LANES:(c + 1) * LANES] = kn.astype(BF16)
        kt_ref[g, c * LANES:(c + 1) * LANES] = kn

    def chunks(loc, width, consumer, per_chunk):
        w_ref, lo = loc
        for c0 in range(0, width, chunk):
            for gi, g in enumerate(groups):
                y = proj(gi, (w_ref, lo + c0), chunk)
                for t in range(chunk // LANES):
                    work.append(functools.partial(consumer, y[:, t * LANES:(t + 1) * LANES], c0 // LANES + t, g))
                run(per_chunk)

    per_chunk = -(-len(work) // (12 * len(groups))) + 2
    chunks((wzx_ref, 0), SSD_WIDTH, gate, per_chunk)
    chunks(o_q, ATT_WIDTH, norm_q, per_chunk)
    chunks(o_k, ATT_WIDTH, norm_k, per_chunk)
    for c0 in range(0, ATT_WIDTH, chunk):
        for gi, g in enumerate(groups):
            if v_feature_major:
                v_ref[c0:c0 + chunk, g] = _dot_nt(wvt_ref[c0:c0 + chunk, :], xn[gi]).astype(BF16)
            else:
                v = proj(gi, (wqkv_ref, 2 * ATT_WIDTH + c0), chunk)
                v_ref[g, c0:c0 + chunk] = v.astype(BF16)
                vt_ref[g, c0:c0 + chunk] = v
            run(3)
    run(len(work))
    for gi, g in enumerate(groups):
        dt_ref[g, :] = proj(gi, o_dt, DT_PAD)
    if v_feature_major:
        @pl.when(i >= n - n_tail)
        def _():
            for gi, g in enumerate(groups):
                vt_ref[g, :] = proj(gi, o_v, ATT_WIDTH)


def _in_proj(x2d, g_mix, w_zx, w_dt, w_qkv, gq_t, gk_t, cprev8, conv_w, conv_b, *, tr, seq_len,
             tail_rows, v_feature_major):
    rows = x2d.shape[0]
    n = rows // tr
    n_tail = tail_rows // tr
    nb = max(tr // seq_len, 1)
    tiles_per_stream = max(seq_len // tr, 1)
    n_streams = rows // seq_len
    assert n * tr == rows and n_tail * tr == tail_rows and cprev8.shape == (n_streams, 8, SSD_CONV_DIM)
    assert (tr // nb) % 8 == 0

    def row(width):
        return pl.BlockSpec((tr, width), lambda i: (i, 0))

    def const(shape):
        return pl.BlockSpec(shape, lambda i: (0,) * len(shape))

    per_stream = pl.BlockSpec((nb, 8, SSD_CONV_DIM), lambda i: (i // tiles_per_stream, 0, 0))
    tail = pl.BlockSpec((tr, ATT_WIDTH), lambda i: (jnp.maximum(i - (n - n_tail), 0), 0))
    if v_feature_major:
        v_spec = pl.BlockSpec((ATT_WIDTH, tr), lambda i: (0, i))
        v_shape = jax.ShapeDtypeStruct((ATT_WIDTH, rows), BF16)
    else:
        v_spec = row(ATT_WIDTH)
        v_shape = jax.ShapeDtypeStruct((rows, ATT_WIDTH), BF16)
    return pl.pallas_call(
        functools.partial(_inproj_kernel, n_tail=n_tail, v_feature_major=v_feature_major, nb=nb,
                          tiles_per_stream=tiles_per_stream),
        grid=(n,),
        in_specs=[row(D_MODEL), const((1, D_MODEL)), _resident((D_MODEL, SSD_WIDTH + SSD_CONV_DIM)),
                  _resident((D_MODEL, DT_PAD)), _resident((D_MODEL, 3 * ATT_WIDTH)),
                  const((1, ATT_WIDTH)), const((1, ATT_WIDTH)),
                  per_stream, const((SSD_CONV, SSD_CONV_DIM)), const((1, SSD_CONV_DIM))],
        out_specs=[row(SSD_WIDTH), row(SSD_CONV_DIM), per_stream, row(DT_PAD), row(ATT_WIDTH),
                   row(ATT_WIDTH), v_spec, tail, tail],
        out_shape=[jax.ShapeDtypeStruct((rows, SSD_WIDTH), F32),
                   jax.ShapeDtypeStruct((rows, SSD_CONV_DIM), F32),
                   jax.ShapeDtypeStruct((n_streams, 8, SSD_CONV_DIM), F32),
                   jax.ShapeDtypeStruct((rows, DT_PAD), F32),
                   jax.ShapeDtypeStruct((rows, ATT_WIDTH), BF16),
                   jax.ShapeDtypeStruct((rows, ATT_WIDTH), BF16),
                   v_shape,
                   jax.ShapeDtypeStruct((tail_rows, ATT_WIDTH), F32),
                   jax.ShapeDtypeStruct((tail_rows, ATT_WIDTH), F32)],
        scratch_shapes=[pltpu.VMEM((nb * (tr // nb + 8), SSD_CONV_DIM), F32),
                        pltpu.VMEM((ATT_WIDTH, D_MODEL) if v_feature_major else (16, LANES), BF16)],
        compiler_params=pltpu.CompilerParams(dimension_semantics=("arbitrary",),
                                             vmem_limit_bytes=VMEM_LIMIT),
        name="in_proj",
    )(x2d, g_mix, w_zx, w_dt, w_qkv, gq_t, gk_t, cprev8, conv_w, conv_b)


def _ssd_chunk(xc_ref, dtraw_ref, gz_ref, dtb_ref, alog_ref, dskip_ref, expand_ref, ht_ref, ybuf_ref,
               *, t, lb, row0=0):
    def rows_of(ref, sl):
        v = ref[0, row0:row0 + lb, sl]
        if lb < t:
            v = jnp.concatenate([v, jnp.zeros((t - lb, v.shape[1]), v.dtype)], axis=0)
        return v

    lane = lax.broadcasted_iota(jnp.int32, (t, LANES), 1)
    rowi = lax.broadcasted_iota(jnp.int32, (t, LANES), 0)
    dt = jax.nn.softplus(rows_of(dtraw_ref, slice(None)) + dtb_ref[...])
    dt = jnp.where((lane < SSD_HEADS) & (rowi < lb), dt, 0.0)
    a_neg = -jnp.exp(alog_ref[...]) * LOG2E
    a = dt * a_neg

    rr = lax.broadcasted_iota(jnp.int32, (t, t), 0)
    cc = lax.broadcasted_iota(jnp.int32, (t, t), 1)
    causal = rr >= cc
    tril = jnp.where(causal, 1.0, 0.0).astype(BF16)
    a1, a2, a3 = _split3(a)
    a_cum = _dot(tril, a1) + _dot(tril, a2) + _dot(tril, a3)
    a_last = a_cum[t - 1:t, :]
    ea = jnp.exp2(a_cum)
    cd = jnp.exp2(a_last)
    a_t = a_cum.T
    w_t = jnp.exp2(a_t[:, t - 1:t] - a_t)

    stacked = jnp.concatenate([ea, dt, jnp.broadcast_to(cd, (8, LANES))], axis=0)
    s_hi, s_lo = _split2(stacked)
    expanded = _dot(s_hi, expand_ref[...]) + _dot(s_lo, expand_ref[...])
    ea_x = expanded[0:t, :]
    dt_x = expanded[t:2 * t, :]
    cd_x = expanded[2 * t:2 * t + 1, :]

    first = lane < SSD_HEAD_DIM
    heads_per_group = SSD_HEADS // SSD_GROUPS
    o_b = SSD_WIDTH
    o_c = SSD_WIDTH + SSD_GROUPS * SSD_STATE
    ssq = [jnp.zeros((lb, 1), F32)]
    per_group = {}

    def group_values(g):
        if g not in per_group:
            bg = rows_of(xc_ref, slice(o_b + g * SSD_STATE, o_b + (g + 1) * SSD_STATE))
            cg = rows_of(xc_ref, slice(o_c + g * SSD_STATE, o_c + (g + 1) * SSD_STATE)).astype(BF16)
            cb_mat = jnp.where(causal, _dot_nt(cg, bg.astype(BF16)), 0.0)
            per_group[g] = (cg, cb_mat, bg.T)
        return per_group[g]

    def prepare(j):
        cg, cb_mat, bg_t = group_values(j // (heads_per_group // 2))
        sl = slice(j * LANES, (j + 1) * LANES)
        xh = rows_of(xc_ref, sl)
        xdt = xh * dt_x[:, sl]
        ops = []
        for hh in range(2):
            h = 2 * j + hh
            xm = jnp.where(first if hh == 0 else jnp.logical_not(first), xdt, 0.0).astype(BF16)
            seg = a_cum[:, h:h + 1] - a_t[h:h + 1, :]
            m = (cb_mat * jnp.exp2(jnp.minimum(seg, 0.0))).astype(BF16)
            bw = (bg_t * w_t[h:h + 1, :]).astype(BF16)
            ops.append((m, bw, xm))
        return cg, xh, ops

    def finish(j, prepared):
        cg, xh, ops = prepared
        sl = slice(j * LANES, (j + 1) * LANES)
        y_pair = _dot(ops[0][0], ops[0][2]) + _dot(ops[1][0], ops[1][2])
        s_pair = _dot(ops[0][1], ops[0][2]) + _dot(ops[1][1], ops[1][2])
        h_in = ht_ref[:, sl]
        y_off = _dot(cg, h_in.astype(BF16)) * ea_x[:, sl]
        ht_ref[:, sl] = cd_x[:, sl] * h_in + s_pair
        y = y_pair + y_off + dskip_ref[:, sl] * xh
        yg = y[0:lb, :] * gz_ref[0, row0:row0 + lb, sl]
        ybuf_ref[row0:row0 + lb, sl] = yg
        ssq[0] = ssq[0] + jnp.sum(yg * yg, axis=-1, keepdims=True)

    def scan():
        n_tiles = SSD_WIDTH // LANES
        ahead = prepare(0)
        for j in range(n_tiles):
            cur = ahead
            if j + 1 < n_tiles:
                ahead = prepare(j + 1)
            finish(j, cur)
        return ssq[0]

    return scan


def _ssd_kernel(xc_ref, dtraw_ref, gz_ref, h0_ref, dtb_ref, alog_ref, dskip_ref, gout_ref, expand_ref,
                y_ref, hfin_ref,
                ht_ref, ybuf_ref, *, t, lb, parts):
    c = pl.program_id(1)
    nc = pl.num_programs(1)

    @pl.when(c == 0)
    def _():
        ht_ref[...] = h0_ref[0].T

    scans = [_ssd_chunk(xc_ref, dtraw_ref, gz_ref, dtb_ref, alog_ref, dskip_ref, expand_ref, ht_ref,
                        ybuf_ref, t=t, lb=lb, row0=part * lb) for part in range(parts)]
    for part, scan in enumerate(scans):
        rows = slice(part * lb, (part + 1) * lb)
        r = lax.rsqrt(scan() * (1.0 / SSD_WIDTH) + EPS)
        y_ref[0, rows, :] = (ybuf_ref[rows, :] * r * gout_ref[...]).astype(BF16)

    @pl.when(c == nc - 1)
    def _():
        hfin_ref[0] = ht_ref[...].T


def _ssd_mixer(xc, dtraw, gz, h0, dtb, alog, dskip, gout, expand, *, t, lb):
    b, length, _ = xc.shape
    n_chunks = length // lb
    assert n_chunks * lb == length and (lb == t or n_chunks == 1)
    parts = SSD_PARTS if n_chunks % SSD_PARTS == 0 else 1
    nc = n_chunks // parts

    def seq(width):
        return pl.BlockSpec((1, parts * lb, width), lambda bi, ci: (bi, ci, 0))

    def per_b(shape):
        return pl.BlockSpec((1,) + shape, lambda bi, ci: (bi, 0, 0))

    def const(shape):
        return pl.BlockSpec(shape, lambda bi, ci: (0,) * len(shape))

    hp = SSD_WIDTH
    return pl.pallas_call(
        functools.partial(_ssd_kernel, t=t, lb=lb, parts=parts),
        grid=(b, nc),
        in_specs=[seq(SSD_CONV_DIM), seq(DT_PAD), seq(SSD_WIDTH), per_b((hp, SSD_STATE)),
                  const((1, DT_PAD)), const((1, DT_PAD)), const((1, SSD_WIDTH)), const((1, SSD_WIDTH)),
                  const((LANES, SSD_WIDTH))],
        out_specs=[seq(SSD_WIDTH), per_b((hp, SSD_STATE))],
        out_shape=[jax.ShapeDtypeStruct((b, length, SSD_WIDTH), BF16),
                   jax.ShapeDtypeStruct((b, hp, SSD_STATE), F32)],
        scratch_shapes=[pltpu.VMEM((SSD_STATE, hp), F32),
                        pltpu.VMEM((parts * lb, SSD_WIDTH), F32)],
        compiler_params=pltpu.CompilerParams(dimension_semantics=("arbitrary", "arbitrary"),
                                             vmem_limit_bytes=VMEM_LIMIT),
        name="ssd_mixer",
    )(xc, dtraw, gz, h0, dtb, alog, dskip, gout, expand)


ATT_SB = 2 * CHUNK
ATT_WIN = ATT_SB + PAST
ATT_PER_TRIP = 8


def _toeplitz_rows(base_row, rows, offset, width):
    ext = base_row.shape[1]
    rolled = pltpu.roll(jnp.broadcast_to(base_row, (rows, ext)), 0, 1, stride=1, stride_axis=0)
    return rolled[:, offset:offset + width]


def _band_subblock(q, kw, vt, bias_t, first):
    return _band_softmax_pv(_band_scores(q, kw, bias_t, first), vt)


def _band_scores(q, kw, bias_t, first):
    zero = jnp.zeros_like(q)
    q2 = jnp.concatenate([jnp.where(first, q, zero), jnp.where(first, zero, q)], axis=0)
    return _dot_nt(kw, q2) + bias_t


def _band_softmax(s):
    m = jnp.max(s, axis=0, keepdims=True)
    p = jnp.exp2(s - m)
    return p.astype(BF16), 1.0 / jnp.sum(p, axis=0, keepdims=True)


def _band_pv(p, inv_l, vt):
    sb = p.shape[1] // 2
    half = ATT_HEAD_DIM
    o_t = _dot(vt, p)
    o_pair_t = jnp.concatenate([o_t[0:half, 0:sb] * inv_l[:, 0:sb],
                                o_t[half:2 * half, sb:2 * sb] * inv_l[:, sb:2 * sb]], axis=0)
    return o_pair_t.T.astype(BF16)


def _band_softmax_pv(s, vt):
    p, inv_l = _band_softmax(s)
    return _band_pv(p, inv_l, vt)


def _band_prompt_kernel(q_ref, k_ref, vt_ref, base_ref, o_ref, bias_ref, sc_ref, pb_ref, il_ref):
    length = q_ref.shape[0]
    sb, w = ATT_SB, ATT_WIN
    kj = lax.broadcasted_iota(jnp.int32, (w, sb), 0) // CHUNK
    qi = lax.broadcasted_iota(jnp.int32, (w, sb), 1) // CHUNK
    in_band = (kj >= qi) & (kj <= qi + N_LEFT_CHUNKS)
    for hh in range(2):
        toe = _toeplitz_rows(base_ref[0, hh:hh + 1, :], w, w, sb)
        bias_ref[:, hh * sb:(hh + 1) * sb] = jnp.where(in_band, toe * LOG2E, NEG)

    lane = lax.broadcasted_iota(jnp.int32, (sb, LANES), 1)
    first = lane < ATT_HEAD_DIM
    n_sub = length // sb
    n_head = min(PAST // sb, n_sub)
    for j in range(n_head):
        n = (j + 1) * sb
        o_ref[j * sb:(j + 1) * sb, :] = _band_subblock(
            q_ref[j * sb:(j + 1) * sb, :], k_ref[0:n, :], vt_ref[:, 0:n], bias_ref[w - n:, :], first)

    n_main = n_sub - n_head
    if n_main == 0:
        return
    assert n_main % 2 == 0

    def scores_into(slot, j):
        r0 = pl.multiple_of(j * sb, sb)
        k0 = pl.multiple_of(j * sb - PAST, sb)
        sc_ref[slot] = _band_scores(q_ref[pl.ds(r0, sb), :], k_ref[pl.ds(k0, w), :], bias_ref[...], first)

    def softmax_into(slot):
        p, inv_l = _band_softmax(sc_ref[slot])
        pb_ref[slot] = p
        il_ref[slot] = jnp.broadcast_to(inv_l, (8, 2 * sb))

    def values_out(slot, j):
        k0 = pl.multiple_of(j * sb - PAST, sb)
        return _band_pv(pb_ref[slot], il_ref[slot, 0:1, :], vt_ref[:, pl.ds(k0, w)])

    last = n_sub - 1
    scores_into(0, n_head)
    scores_into(1, n_head + 1)
    softmax_into(0)

    def trip(per_trip, base):
        def body(i, carry):
            outs = []
            for u in range(per_trip):
                j = base + per_trip * i + u
                slot = u % 2
                scores_into(slot, jnp.minimum(j + 2, last))
                softmax_into(1 - slot)
                outs.append((j, values_out(slot, j)))
            for j, o in outs:
                o_ref[pl.ds(pl.multiple_of(j * sb, sb), sb), :] = o
            return carry
        return body

    n_long = n_main // ATT_PER_TRIP
    if n_long:
        lax.fori_loop(0, n_long, trip(ATT_PER_TRIP, n_head), 0)
    rest = n_main - n_long * ATT_PER_TRIP
    if rest:
        trip(rest, n_head + n_long * ATT_PER_TRIP)(0, 0)


def _band_prompt(q, k, vt, base):
    length = q.shape[0]
    assert length % ATT_SB == 0
    ext = ATT_SB + ATT_WIN
    col = pl.BlockSpec((length, LANES), lambda c: (0, c))
    return pl.pallas_call(
        _band_prompt_kernel,
        grid=(ATT_WIDTH // LANES,),
        in_specs=[col, col, pl.BlockSpec((LANES, length), lambda c: (c, 0)),
                  pl.BlockSpec((1, 2, ext), lambda c: (c, 0, 0))],
        out_specs=col,
        out_shape=jax.ShapeDtypeStruct((length, ATT_WIDTH), BF16),
        scratch_shapes=[pltpu.VMEM((ATT_WIN, 2 * ATT_SB), F32),
                        pltpu.VMEM((2, ATT_WIN, 2 * ATT_SB), F32),
                        pltpu.VMEM((2, ATT_WIN, 2 * ATT_SB), BF16),
                        pltpu.VMEM((2, 8, 2 * ATT_SB), F32)],
        compiler_params=pltpu.CompilerParams(dimension_semantics=("arbitrary",),
                                             vmem_limit_bytes=VMEM_LIMIT),
        name="band_prompt",
    )(q, k, vt, base.reshape(ATT_HEADS // 2, 2, ext))


def _band_sample_kernel(q_ref, kn_ref, vn_ref, kc_ref, vc_ref, base_ref, o_ref, bias_ref):
    lq = q_ref.shape[0]
    lc = kc_ref.shape[3]

    @pl.when(pl.program_id(0) == 0)
    def _():
        for h in range(ATT_HEADS):
            bias_ref[h] = _toeplitz_rows(base_ref[h:h + 1, :], lq, LANES, lc + LANES) * LOG2E

    for h in range(ATT_HEADS):
        sl = slice(h * ATT_HEAD_DIM, (h + 1) * ATT_HEAD_DIM)
        q = q_ref[:, sl]
        sc = _dot(q, kc_ref[0, h].astype(BF16)) + bias_ref[h, :, 0:lc]
        sn = _dot_nt(q, kn_ref[:, sl]) + bias_ref[h, :, lc:lc + lq]
        m = jnp.maximum(jnp.max(sc, axis=-1, keepdims=True), jnp.max(sn, axis=-1, keepdims=True))
        pc = jnp.exp2(sc - m)
        pn = jnp.exp2(sn - m)
        l = jnp.sum(pc, axis=-1, keepdims=True) + jnp.sum(pn, axis=-1, keepdims=True)
        o = _dot_nt(pc.astype(BF16), vc_ref[0, h].astype(BF16)) + _dot(pn.astype(BF16), vn_ref[:, sl])
        o_ref[:, sl] = (o * (1.0 / l)).astype(BF16)


def _band_sample(q, kn, vn, kc, vc, base, *, lq):
    rows = q.shape[0]
    nb = rows // lq
    lc = kc.shape[3]
    assert lq <= LANES and lc % LANES == 0 and base.shape == (ATT_HEADS, lc + 2 * LANES)
    blk = pl.BlockSpec((lq, ATT_WIDTH), lambda b: (b, 0))
    cache = pl.BlockSpec((1, ATT_HEADS, ATT_HEAD_DIM, lc), lambda b: (b, 0, 0, 0))
    return pl.pallas_call(
        _band_sample_kernel,
        grid=(nb,),
        in_specs=[blk, blk, blk, cache, cache, pl.BlockSpec(base.shape, lambda b: (0, 0))],
        out_specs=blk,
        out_shape=jax.ShapeDtypeStruct((rows, ATT_WIDTH), BF16),
        scratch_shapes=[pltpu.VMEM((ATT_HEADS, lq, lc + LANES), F32)],
        compiler_params=pltpu.CompilerParams(dimension_semantics=("arbitrary",),
                                             vmem_limit_bytes=VMEM_LIMIT),
        name="band_sample",
    )(q, kn, vn, kc, vc, base)


def _mem_head_norm(y, g_ref, hd, scale):
    blk = y[:, hd * MEM_HEAD_DIM:(hd + 1) * MEM_HEAD_DIM]
    r = lax.rsqrt(jnp.mean(blk * blk, axis=-1, keepdims=True) + EPS)
    return blk * r * (g_ref[...] * scale)


def _memkv_kernel(mem_ref, gsrc_ref, wk_ref, wv_ref, gk_ref, k_ref, v_ref):
    m = _rms(mem_ref[...], gsrc_ref[...]).astype(BF16)
    k = _dot(m, wk_ref[...])
    for hd in range(MEM_HEADS):
        k_ref[:, hd * MEM_HEAD_DIM:(hd + 1) * MEM_HEAD_DIM] = _mem_head_norm(k, gk_ref, hd, 1.0)
    v_ref[...] = _dot(m, wv_ref[...])


def _memory_kv(mem2d, g_src, w_mk, w_mv, g_mk):
    rows = mem2d.shape[0]

    def full(shape):
        return pl.BlockSpec(shape, lambda i: (0,) * len(shape))

    return pl.pallas_call(
        _memkv_kernel,
        grid=(1,),
        in_specs=[full((rows, D_MODEL)), full((1, D_MODEL)), full((D_MODEL, D_MODEL)),
                  full((D_MODEL, D_MODEL)), full((1, MEM_HEAD_DIM))],
        out_specs=[full((rows, D_MODEL)), full((rows, D_MODEL))],
        out_shape=[jax.ShapeDtypeStruct((rows, D_MODEL), F32)] * 2,
        compiler_params=pltpu.CompilerParams(dimension_semantics=("arbitrary",),
                                             vmem_limit_bytes=VMEM_LIMIT),
        name="memory_kv",
    )(mem2d, g_src, w_mk, w_mv, g_mk)


FF_SLAB = 1024


def _mix_ffn_kernel(x_ref, ys_ref, ya_ref, wo_ref, gmx_ref, wq_ref, gmq_ref, mk_ref, mv_ref,
                    wmo_ref, gffn_ref, w1_ref, w2_ref, y_ref, obuf_ref, *, nb, rb, dense_rows):
    tr = x_ref.shape[0]
    groups = [slice(r0, r0 + dense_rows) for r0 in range(0, tr, dense_rows)]
    h = [x_ref[g, :] + _dot(ys_ref[g, :], wo_ref[0:SSD_WIDTH, :]) + _dot(ya_ref[g, :], wo_ref[SSD_WIDTH:, :])
         for g in groups]
    hn = [_rms(hg, gmx_ref[...]).astype(BF16) for hg in h]
    q = [_dot(hg, wq_ref[...]) for hg in hn]
    for hd in range(MEM_HEADS):
        sl = slice(hd * MEM_HEAD_DIM, (hd + 1) * MEM_HEAD_DIM)
        qn = [_mem_head_norm(qg, gmq_ref, hd, MEM_HEAD_DIM ** -0.5 * LOG2E).astype(BF16) for qg in q]
        for g, qg in zip(groups, qn):
            for r0 in range(g.start, g.stop, rb):
                b = r0 // rb if nb > 1 else 0
                s = _dot_nt(qg[r0 - g.start:r0 - g.start + min(rb, dense_rows), :], mk_ref[b, :, sl].astype(BF16))
                p = jnp.exp2(s - jnp.max(s, axis=-1, keepdims=True))
                o = (_dot(p.astype(BF16), mv_ref[b, :, sl].astype(BF16))
                     * (1.0 / jnp.sum(p, axis=-1, keepdims=True)))
                obuf_ref[r0:r0 + min(rb, dense_rows), sl] = o.astype(BF16)
    h = [hg + _dot(obuf_ref[g, :], wmo_ref[...]) for g, hg in zip(groups, h)]
    hn = [_rms(hg, gffn_ref[...]).astype(BF16) for hg in h]
    for s in range(D_FF // FF_SLAB):
        for gi in range(len(groups)):
            u = jnp.maximum(_dot(hn[gi], w1_ref[:, s * FF_SLAB:(s + 1) * FF_SLAB]), 0.0)
            h[gi] = h[gi] + _dot((u * u).astype(BF16), w2_ref[s * FF_SLAB:(s + 1) * FF_SLAB, :])
    for g, hg in zip(groups, h):
        y_ref[g, :] = hg


def _mix_ffn(x2d, ys, ya, w_out, g_mem_x, w_mq, g_mq, mk, mv, w_mo, g_ffn, w1, w2, *, tr, seq_len):
    rows = x2d.shape[0]
    n = rows // tr
    nb = max(tr // seq_len, 1)
    tiles_per_stream = max(seq_len // tr, 1)
    rb = tr // nb
    assert n * tr == rows and nb * rb == tr and mk.shape[0] * seq_len == rows

    def row(width):
        return pl.BlockSpec((tr, width), lambda i: (i, 0))

    def const(shape):
        return pl.BlockSpec(shape, lambda i: (0,) * len(shape))

    mem = pl.BlockSpec((nb, N_MEM, D_MODEL), lambda i: (i // tiles_per_stream, 0, 0),
                       pipeline_mode=pl.Buffered(1))
    return pl.pallas_call(
        functools.partial(_mix_ffn_kernel, nb=nb, rb=rb, dense_rows=min(tr, MIX_DENSE_ROWS)),
        grid=(n,),
        in_specs=[row(D_MODEL), row(SSD_WIDTH), row(ATT_WIDTH), _resident((SSD_WIDTH + ATT_WIDTH, D_MODEL)),
                  const((1, D_MODEL)), _resident((D_MODEL, D_MODEL)), const((1, MEM_HEAD_DIM)), mem, mem,
                  _resident((D_MODEL, D_MODEL)), const((1, D_MODEL)), _resident((D_MODEL, D_FF)),
                  _resident((D_FF, D_MODEL))],
        out_specs=row(D_MODEL),
        out_shape=jax.ShapeDtypeStruct((rows, D_MODEL), F32),
        scratch_shapes=[pltpu.VMEM((tr, D_MODEL), BF16)],
        compiler_params=pltpu.CompilerParams(dimension_semantics=("arbitrary",),
                                             vmem_limit_bytes=VMEM_LIMIT),
        name="mix_ffn",
    )(x2d, ys, ya, w_out, g_mem_x, w_mq, g_mq, mk, mv, w_mo, g_ffn, w1, w2)


def _toeplitz_base(table, offset, width):
    heads, size = table.shape
    n_far = offset - REL_CLIP
    assert n_far >= 0
    parts = [jnp.broadcast_to(table[:, size - 1:], (heads, n_far)), table[:, ::-1]]
    rest = width - n_far - size
    if rest > 0:
        parts.append(jnp.broadcast_to(table[:, :1], (heads, rest)))
    return jnp.concatenate(parts, axis=1)[:, :width].astype(F32)


def _toeplitz_base_t(table, shift, width):
    heads, size = table.shape
    n_low = shift - REL_CLIP
    assert n_low >= 0
    parts = [jnp.broadcast_to(table[:, :1], (heads, n_low)), table]
    rest = width - n_low - size
    if rest > 0:
        parts.append(jnp.broadcast_to(table[:, size - 1:], (heads, rest)))
    return jnp.concatenate(parts, axis=1)[:, :width].astype(F32)


def _prep_weights(g_mix, w_in, conv_w, conv_b, ssd_A_log, ssd_dt_bias, ssd_D, ssd_g_out, att_g_q, att_g_k,
                  w_out, g_mem_x, g_mem_src, w_mq, w_mk, w_mv, g_mq, g_mk, w_mo, g_ffn, w_ff1, w_ff2):
    w_zx, w_dt, w_qkv = _split_w_in(w_in)
    pad_h = lambda v: jnp.pad(v, (0, DT_PAD - SSD_HEADS)).reshape(1, DT_PAD)
    row = lambda v: v.reshape(1, -1)
    expand = (jnp.arange(LANES)[:, None] == jnp.arange(SSD_WIDTH)[None, :] // SSD_HEAD_DIM).astype(BF16)
    return dict(
        g_mix=row(g_mix), w_zx=w_zx, w_dt=w_dt, w_qkv=w_qkv,
        gq_t=row(jnp.tile(att_g_q, ATT_HEADS)), gk_t=row(jnp.tile(att_g_k, ATT_HEADS)),
        conv_w=conv_w, conv_b=row(conv_b), dtb=pad_h(ssd_dt_bias), alog=pad_h(ssd_A_log),
        dskip=row(jnp.repeat(ssd_D, SSD_HEAD_DIM)), gout=row(ssd_g_out), expand=expand,
        w_out=w_out.astype(BF16), g_mem_x=row(g_mem_x), g_mem_src=row(g_mem_src),
        w_mq=w_mq.astype(BF16), w_mk=w_mk.astype(BF16), w_mv=w_mv.astype(BF16),
        g_mq=row(g_mq), g_mk=row(g_mk), w_mo=w_mo.astype(BF16), g_ffn=row(g_ffn),
        w_ff1=w_ff1.astype(BF16), w_ff2=w_ff2.astype(BF16))


def _layer(x, conv_prev, h0, k_cache, v_cache, mem_k, mem_v, p, rel, *, tr, t_scan):
    b, length, _ = x.shape
    rows = b * length
    x2d = x.reshape(rows, D_MODEL)
    prompt = k_cache is None
    tail_rows = PAST if prompt else rows
    cprev8 = jnp.pad(conv_prev, ((0, 0), (8 - (SSD_CONV - 1), 0), (0, 0)))
    tr_in = 2 * tr if prompt and rows % (2 * tr) == 0 and tail_rows % (2 * tr) == 0 else tr
    gz, xc, xtail, dtraw, q, k, v, k_tail, v_tail = _in_proj(
        x2d, p["g_mix"], p["w_zx"], p["w_dt"], p["w_qkv"], p["gq_t"], p["gk_t"], cprev8,
        p["conv_w"], p["conv_b"], tr=tr_in, seq_len=length, tail_rows=tail_rows, v_feature_major=prompt)
    conv_new = xtail[:, 8 - (SSD_CONV - 1):]

    scan_args = (xc.reshape(b, length, SSD_CONV_DIM), dtraw.reshape(b, length, DT_PAD),
                 gz.reshape(b, length, SSD_WIDTH), h0.reshape(b, SSD_WIDTH, SSD_STATE),
                 p["dtb"], p["alog"], p["dskip"], p["gout"], p["expand"])
    mix_args = (p["w_out"], p["g_mem_x"], p["w_mq"], p["g_mq"], mem_k, mem_v,
                p["w_mo"], p["g_ffn"], p["w_ff1"], p["w_ff2"])
    y_ssd, h_fin = _ssd_mixer(*scan_args, t=t_scan, lb=t_scan if prompt else length)
    if prompt:
        y_att = _band_prompt(q, k, v, _toeplitz_base_t(rel, ATT_SB, ATT_SB + ATT_WIN))
        k_rows = k_tail.reshape(b, PAST, ATT_HEADS, ATT_HEAD_DIM)
        v_rows = v_tail.reshape(b, PAST, ATT_HEADS, ATT_HEAD_DIM)
    else:
        lc = k_cache.shape[1]
        y_att = _band_sample(q, k, v, jnp.transpose(k_cache, (0, 2, 3, 1)),
                             jnp.transpose(v_cache, (0, 2, 3, 1)),
                             _toeplitz_base(rel, lc + LANES, lc + 2 * LANES), lq=length)
        k_rows = k_tail.reshape(b, length, ATT_HEADS, ATT_HEAD_DIM)
        v_rows = v_tail.reshape(b, length, ATT_HEADS, ATT_HEAD_DIM)
    tr_mix = 2 * tr if prompt and rows % (2 * tr) == 0 else tr
    y = _mix_ffn(x2d, y_ssd.reshape(rows, SSD_WIDTH), y_att, *mix_args, tr=tr_mix, seq_len=length)
    return (y.reshape(b, length, D_MODEL), h_fin.reshape(b, SSD_HEADS, SSD_HEAD_DIM, SSD_STATE),
            conv_new, k_rows, v_rows)


def kernel(x_prompt, x_sample, mem_prompt, state_ssd, state_conv, cache_attn_k, cache_attn_v, cache_mem_k,
           cache_mem_v, g_mix, w_in, conv_w, conv_b, ssd_A_log, ssd_dt_bias, ssd_D, ssd_g_out, att_g_q,
           att_g_k, att_rel_bias, w_out, g_mem_x, g_mem_src, w_mq, w_mk, w_mv, g_mq, g_mk, w_mo, g_ffn,
           w_ff1, w_ff2):
    depth = g_mix.shape[0]
    b_p, seq, _ = x_prompt.shape
    b_s, dec_seq, _ = x_sample.shape
    yp, ys = x_prompt, x_sample
    outs = [[] for _ in range(10)]
    for l in range(depth):
        p = _prep_weights(g_mix[l], w_in[l], conv_w[l], conv_b[l], ssd_A_log[l], ssd_dt_bias[l], ssd_D[l],
                          ssd_g_out[l], att_g_q[l], att_g_k[l], w_out[l], g_mem_x[l], g_mem_src[l],
                          w_mq[l], w_mk[l], w_mv[l], g_mq[l], g_mk[l], w_mo[l], g_ffn[l], w_ff1[l], w_ff2[l])
        rel = att_rel_bias[l]
        mk, mv = _memory_kv(mem_prompt.reshape(b_p * N_MEM, D_MODEL), p["g_mem_src"], p["w_mk"], p["w_mv"],
                            p["g_mk"])
        mk = mk.reshape(b_p, N_MEM, D_MODEL)
        mv = mv.reshape(b_p, N_MEM, D_MODEL)
        conv0 = jnp.zeros((b_p, SSD_CONV - 1, SSD_CONV_DIM), F32)
        h00 = jnp.zeros((b_p, SSD_HEADS, SSD_HEAD_DIM, SSD_STATE), F32)
        yp, hp, cp, kp, vp = _layer(yp, conv0, h00, None, None, mk, mv, p, rel,
                                    tr=min(256, seq), t_scan=min(256, seq))
        ys, hs, cs, ks_, vs_ = _layer(ys, state_conv[l], state_ssd[l], cache_attn_k[l], cache_attn_v[l],
                                      cache_mem_k[l].reshape(b_s, N_MEM, D_MODEL),
                                      cache_mem_v[l].reshape(b_s, N_MEM, D_MODEL), p, rel,
                                      tr=b_s * dec_seq, t_scan=128)
        for lst, val in zip(outs, (hp, cp, kp, vp,
                                   mk.reshape(b_p, N_MEM, MEM_HEADS, MEM_HEAD_DIM),
                                   mv.reshape(b_p, N_MEM, MEM_HEADS, MEM_HEAD_DIM),
                                   hs, cs, ks_, vs_)):
            lst.append(val)
    return (yp, ys) + tuple(jnp.stack(o) for o in outs)
```
